```python
import math
import jax, jax.numpy as jnp
from jax import lax
import numpy as np

D_MODEL = 1024
BATCH = 2
SEQ = 8192
DEPTH = 1
DEC_BATCH = 128
DEC_SEQ = 1
PAST_LEN = 8192
PAGE_SIZE = 128

D_INNER = 2 * D_MODEL
SSD_HEAD_DIM = 64
N_SSD_HEADS = D_INNER // SSD_HEAD_DIM
N_SSD_GROUPS = 4
D_STATE = 128
CONV_WIDTH = 4
CONV_DIM = D_INNER + 2 * N_SSD_GROUPS * D_STATE
SSD_CHUNK = 128
ATTN_PATTERNS = ((128, 1), (512, 4), (2048, 16))
N_PATTERNS = 3
HEADS_PER_PATTERN = 4
ATTN_HEAD_DIM = 128
ATTN_QKV_WIDTH = N_PATTERNS * HEADS_PER_PATTERN * ATTN_HEAD_DIM
ATTN_OUT_WIDTH = HEADS_PER_PATTERN * ATTN_HEAD_DIM
ATTN_BLOCK = 128
ALIBI_MAX_EXP = 8.0
D_FF = 2816
PLE_DIM = 256
LN_EPS = 1e-5
RMS_EPS = 1e-5
DEEPNORM_ALPHA = (2.0 * DEPTH) ** 0.25
DEEPNORM_BETA = (8.0 * DEPTH) ** -0.25
IN_SPLITS = (D_INNER, CONV_DIM, N_SSD_HEADS, ATTN_QKV_WIDTH, ATTN_QKV_WIDTH, ATTN_QKV_WIDTH)
IN_COLS = sum(IN_SPLITS) + 2 * D_MODEL

kernel_name = 'ssd_dilated_attn_macaron_deepnorm_step'


def layer_norm(x, w, b):
    xf = x.astype(jnp.float32)
    mu = jnp.mean(xf, -1, keepdims=True)
    var = jnp.mean(jnp.square(xf - mu), -1, keepdims=True)
    y = (xf - mu) * lax.rsqrt(var + LN_EPS) * w.astype(jnp.float32) + b.astype(jnp.float32)
    return y.astype(x.dtype)


def swiglu(x, w13, w2):
    gate, up = jnp.split(x @ w13, 2, axis=-1)
    return (jax.nn.silu(gate) * up) @ w2


def alibi_slopes():
    n = N_PATTERNS * HEADS_PER_PATTERN
    e = jnp.arange(1, n + 1, dtype=jnp.float32)
    return (2.0 ** (-ALIBI_MAX_EXP * e / n)).reshape(N_PATTERNS, HEADS_PER_PATTERN)


def causal_depthwise_conv(x, conv_state, w, b):
    xpad = jnp.concatenate([conv_state.astype(x.dtype), x], axis=1)
    y = lax.conv_general_dilated(xpad, w[:, None, :].astype(xpad.dtype), window_strides=(1,),
                                 padding='VALID', dimension_numbers=('NWC', 'WIO', 'NWC'),
                                 feature_group_count=x.shape[-1])
    return jax.nn.silu(y + b), xpad[:, -(CONV_WIDTH - 1):]


def _pad_time(t, pad):
    return jnp.pad(t, [(0, 0), (0, pad)] + [(0, 0)] * (t.ndim - 2))


def ssd_scan(x, dt, A, B, C, h0):
    bsz, L, H, P = x.shape
    G, N = B.shape[2], B.shape[3]
    HG = H // G
    f32 = jnp.float32
    Q = min(SSD_CHUNK, L)
    pad = (-L) % Q
    x, dt, B, C = [_pad_time(t.astype(f32), pad) for t in (x, dt, B, C)]
    nc = (L + pad) // Q
    xc = x.reshape(bsz, nc, Q, G, HG, P)
    dtc = dt.reshape(bsz, nc, Q, G, HG)
    Bc = B.reshape(bsz, nc, Q, G, N)
    Cc = C.reshape(bsz, nc, Q, G, N)
    cum = jnp.cumsum(dtc * A.astype(f32).reshape(G, HG), axis=2)
    causal = jnp.tril(jnp.ones((Q, Q), bool))
    seg = cum[:, :, :, None] - cum[:, :, None, :]
    decay = jnp.exp(jnp.where(causal[:, :, None, None], seg, -jnp.inf))
    cb = jnp.einsum('bctgn,bcsgn->bctsg', Cc, Bc)
    y_diag = jnp.einsum('bctsgh,bcsghp->bctghp', cb[..., None] * decay * dtc[:, :, None], xc)
    decay_end = jnp.exp(cum[:, :, -1:] - cum)
    chunk_states = jnp.einsum('bcsgn,bcsghp->bcghpn', Bc, xc * (decay_end * dtc)[..., None])
    chunk_decay = jnp.exp(cum[:, :, -1])

    def step(h, inp):
        s, d = inp
        return h * d[..., None, None] + s, h

    h_last, h_in = lax.scan(step, h0.astype(f32).reshape(bsz, G, HG, P, N),
                            (jnp.moveaxis(chunk_states, 1, 0), jnp.moveaxis(chunk_decay, 1, 0)))
    h_in = jnp.moveaxis(h_in, 0, 1)
    y_off = jnp.einsum('bctgn,bcghpn->bctghp', Cc, h_in) * jnp.exp(cum)[..., None]
    y = (y_diag + y_off).reshape(bsz, nc * Q, H, P)[:, :L]
    return y, h_last.reshape(bsz, H, P, N)


def ssd_branch(z, xbc, dt_raw, conv_state, ssm_state, conv_w, conv_b, dt_bias, a_log, d_skip, norm_w):
    b, L, _ = z.shape
    f32 = jnp.float32
    xbc, new_conv = causal_depthwise_conv(xbc, conv_state, conv_w, conv_b)
    xs, Bm, Cm = jnp.split(xbc, [D_INNER, D_INNER + N_SSD_GROUPS * D_STATE], axis=-1)
    xs = xs.reshape(b, L, N_SSD_HEADS, SSD_HEAD_DIM)
    Bm = Bm.reshape(b, L, N_SSD_GROUPS, D_STATE)
    Cm = Cm.reshape(b, L, N_SSD_GROUPS, D_STATE)
    dt = jax.nn.softplus(dt_raw.astype(f32) + dt_bias.astype(f32))
    A = -jnp.exp(a_log.astype(f32))
    y, new_ssm = ssd_scan(xs, dt, A, Bm, Cm, ssm_state)
    y = y + d_skip.astype(f32)[:, None] * xs.astype(f32)
    y = y.reshape(b, L, D_INNER) * jax.nn.silu(z.astype(f32))
    y = y * lax.rsqrt(jnp.mean(jnp.square(y), -1, keepdims=True) + RMS_EPS) * norm_w.astype(f32)
    return y.astype(z.dtype), new_conv, new_ssm


def dilated_attention_prompt(q, k, v, slopes, window, dilation):
    B, S, H, Dh = q.shape
    steps = window // dilation
    L = S // dilation
    nb = -(-L // ATTN_BLOCK)
    Lp = nb * ATTN_BLOCK

    def to_blocks(t):
        t = t.reshape(B, L, dilation, H, Dh).transpose(0, 2, 1, 3, 4)
        t = jnp.pad(t, ((0, 0), (0, 0), (0, Lp - L), (0, 0), (0, 0)))
        return t.reshape(B, dilation, nb, ATTN_BLOCK, H, Dh)

    def with_prev(t):
        prev = jnp.pad(t[:, :, :-1], ((0, 0), (0, 0), (1, 0), (0, 0), (0, 0), (0, 0)))
        return jnp.concatenate([prev, t], axis=3)

    qb = to_blocks(q)
    kk = with_prev(to_blocks(k))
    vv = with_prev(to_blocks(v))
    s = jnp.einsum('brnqhd,brnkhd->brnhqk', qb, kk) * (ATTN_HEAD_DIM ** -0.5)
    a = jnp.arange(ATTN_BLOCK)[:, None]
    c = jnp.arange(2 * ATTN_BLOCK)[None, :]
    dist = ATTN_BLOCK + a - c
    blk = jnp.arange(nb)[:, None, None]
    valid = (dist >= 0) & (dist <= steps) & ((blk - 1) * ATTN_BLOCK + c >= 0)
    bias = -slopes[:, None, None] * (dilation * dist).astype(jnp.float32)
    s = jnp.where(valid[:, None], s + bias, -jnp.inf)
    m = jnp.max(s, -1)
    p = jnp.exp(s - m[..., None])
    l = jnp.sum(p, -1)
    o = jnp.einsum('brnhqk,brnkhd->brnqhd', p, vv)
    m = jnp.swapaxes(m, 3, 4)
    l = jnp.swapaxes(l, 3, 4)
    o = o / l[..., None]

    def from_blocks(t):
        t = t.reshape((B, dilation, Lp) + t.shape[4:])[:, :, :L]
        t = jnp.moveaxis(t, 1, 2)
        return t.reshape((B, S) + t.shape[3:])

    return from_blocks(o), from_blocks(m), from_blocks(l)


def dilated_attention_decode(q, k_new, v_new, kv_buf, slopes, window, dilation):
    Bd, T = q.shape[:2]
    Wb = kv_buf.shape[1]
    steps = window // dilation
    k_all = jnp.concatenate([kv_buf[:, :, 0].astype(jnp.float32), k_new], axis=1)
    v_all = jnp.concatenate([kv_buf[:, :, 1].astype(jnp.float32), v_new], axis=1)
    j = jnp.arange(steps + 1)
    idx = (Wb + jnp.arange(T))[:, None] - dilation * j[None, :]
    valid = idx >= 0
    idx = jnp.maximum(idx, 0)
    kg = k_all[:, idx]
    vg = v_all[:, idx]
    bias = -slopes[:, None] * (dilation * j).astype(jnp.float32)[None, :]
    s = jnp.einsum('bthd,btkhd->bthk', q, kg) * (ATTN_HEAD_DIM ** -0.5) + bias
    s = jnp.where(valid[None, :, None, :], s, -jnp.inf)
    m = jnp.max(s, -1)
    p = jnp.exp(s - m[..., None])
    l = jnp.sum(p, -1)
    o = jnp.einsum('bthk,btkhd->bthd', p, vg) / l[..., None]
    return o, m, l


def combine_patterns(outs, ms, ls):
    o = jnp.stack(outs, 2)
    m = jnp.stack(ms, 2)
    l = jnp.stack(ls, 2)
    wgt = l * jnp.exp(m - jnp.max(m, axis=2, keepdims=True))
    return jnp.sum(wgt[..., None] * o, 2) / jnp.sum(wgt, 2)[..., None]


def trunk_layer(x, p, ssm_state, conv_state, kv_bufs, decode, ln_w, ln_b, ffn_w13, ffn_w2, w_in,
                conv_w, conv_b, dt_bias, a_log, d_skip, ssd_norm_w, w_ssd_out, w_attn_out,
                w_mix_out, w_pe_gate, w_pe_proj):
    b, L, _ = x.shape
    h = layer_norm(DEEPNORM_ALPHA * x + 0.5 * swiglu(x, ffn_w13[0], ffn_w2[0]), ln_w[0], ln_b[0])
    proj = h @ w_in
    cuts = [int(c) for c in np.cumsum(IN_SPLITS)]
    z, xbc, dt_raw, q, k, v, gates = jnp.split(proj, cuts, axis=-1)
    y_ssd, new_conv, new_ssm = ssd_branch(z, xbc, dt_raw, conv_state, ssm_state, conv_w, conv_b,
                                          dt_bias, a_log, d_skip, ssd_norm_w)
    shp = (b, L, N_PATTERNS, HEADS_PER_PATTERN, ATTN_HEAD_DIM)
    k = k.reshape(shp)
    v = v.reshape(shp)
    qf, kf, vf = q.reshape(shp).astype(jnp.float32), k.astype(jnp.float32), v.astype(jnp.float32)
    slopes = alibi_slopes()
    outs, ms, ls, new_bufs = [], [], [], []
    for g, (win, dil) in enumerate(ATTN_PATTERNS):
        kv_new = jnp.stack([k[:, :, g], v[:, :, g]], axis=2)
        if decode:
            o, m, l = dilated_attention_decode(qf[:, :, g], kf[:, :, g], vf[:, :, g], kv_bufs[g],
                                               slopes[g], win, dil)
            total = kv_bufs[g].shape[1] + L
            keep = min(win, total)
            buf = jnp.concatenate([kv_bufs[g].astype(kv_new.dtype), kv_new], axis=1)[:, total - keep:]
        else:
            o, m, l = dilated_attention_prompt(qf[:, :, g], kf[:, :, g], vf[:, :, g], slopes[g], win, dil)
            buf = kv_new[:, L - min(win, L):]
        outs.append(o)
        ms.append(m)
        ls.append(l)
        new_bufs.append(buf)
    attn = combine_patterns(outs, ms, ls).reshape(b, L, ATTN_OUT_WIDTH).astype(x.dtype)
    g_ssd, g_attn = jnp.split(jax.nn.sigmoid(gates), 2, axis=-1)
    mix = (g_ssd * (y_ssd @ w_ssd_out) + g_attn * (attn @ w_attn_out)) @ w_mix_out
    h = layer_norm(DEEPNORM_ALPHA * h + mix, ln_w[1], ln_b[1])
    h = layer_norm(DEEPNORM_ALPHA * h + 0.5 * swiglu(h, ffn_w13[1], ffn_w2[1]), ln_w[2], ln_b[2])
    h = h + jax.nn.sigmoid(h @ w_pe_gate) * (p @ w_pe_proj)
    return h, new_ssm, new_conv, new_bufs


def setup_inputs(seed: int = 0) -> dict:
    key = jax.random.key(seed)
    ks = iter(jax.random.split(key, 40))

    def nrm(shape, scale):
        return scale * jax.random.normal(next(ks), shape, jnp.float32)

    def kv_cache(win):
        return nrm((DEPTH, DEC_BATCH, min(win, PAST_LEN), 2, HEADS_PER_PATTERN, ATTN_HEAD_DIM), 1.0)

    x_prompt = nrm((BATCH, SEQ, D_MODEL), 1.0)
    x_sample = nrm((DEC_BATCH, DEC_SEQ, D_MODEL), 1.0)
    state_ssm = nrm((DEPTH, DEC_BATCH, N_SSD_HEADS, SSD_HEAD_DIM, D_STATE), 0.5)
    state_conv = nrm((DEPTH, DEC_BATCH, CONV_WIDTH - 1, CONV_DIM), 1.0)
    cache_kv_w128 = kv_cache(ATTN_PATTERNS[0][0])
    cache_kv_w512 = kv_cache(ATTN_PATTERNS[1][0])
    cache_kv_w2048 = kv_cache(ATTN_PATTERNS[2][0])
    p_prompt = nrm((DEPTH, BATCH, SEQ, PLE_DIM), 1.0)
    p_sample = nrm((DEPTH, DEC_BATCH, DEC_SEQ, PLE_DIM), 1.0)
    ln_w = 1.0 + nrm((DEPTH, 3, D_MODEL), 0.02)
    ln_b = nrm((DEPTH, 3, D_MODEL), 0.02)
    ffn_w13 = nrm((DEPTH, 2, D_MODEL, 2 * D_FF), D_MODEL ** -0.5)
    ffn_w2 = nrm((DEPTH, 2, D_FF, D_MODEL), DEEPNORM_BETA * D_FF ** -0.5)
    w_in = nrm((DEPTH, D_MODEL, IN_COLS), D_MODEL ** -0.5)
    conv_w = nrm((DEPTH, CONV_WIDTH, CONV_DIM), CONV_WIDTH ** -0.5)
    conv_b = nrm((DEPTH, CONV_DIM), 0.02)
    dt0 = jnp.exp(jax.random.uniform(next(ks), (DEPTH, N_SSD_HEADS), jnp.float32,
                                     math.log(1e-3), math.log(1e-1)))
    dt_bias = dt0 + jnp.log(-jnp.expm1(-dt0))
    a_log = jnp.log(jax.random.uniform(next(ks), (DEPTH, N_SSD_HEADS), jnp.float32, 1.0, 16.0))
    d_skip = 1.0 + nrm((DEPTH, N_SSD_HEADS), 0.02)
    ssd_norm_w = 1.0 + nrm((DEPTH, D_INNER), 0.02)
    w_ssd_out = nrm((DEPTH, D_INNER, D_MODEL), DEEPNORM_BETA * D_INNER ** -0.5)
    w_attn_out = nrm((DEPTH, ATTN_OUT_WIDTH, D_MODEL), DEEPNORM_BETA * ATTN_OUT_WIDTH ** -0.5)
    w_mix_out = nrm((DEPTH, D_MODEL, D_MODEL), DEEPNORM_BETA * D_MODEL ** -0.5)
    w_pe_gate = nrm((DEPTH, D_MODEL, D_MODEL), D_MODEL ** -0.5)
    w_pe_proj = nrm((DEPTH, PLE_DIM, D_MODEL), PLE_DIM ** -0.5)
    return {'x_prompt': x_prompt, 'x_sample': x_sample, 'state_ssm': state_ssm,
            'state_conv': state_conv, 'cache_kv_w128': cache_kv_w128,
            'cache_kv_w512': cache_kv_w512, 'cache_kv_w2048': cache_kv_w2048,
            'p_prompt': p_prompt, 'p_sample': p_sample, 'ln_w': ln_w, 'ln_b': ln_b,
            'ffn_w13': ffn_w13, 'ffn_w2': ffn_w2, 'w_in': w_in, 'conv_w': conv_w,
            'conv_b': conv_b, 'dt_bias': dt_bias, 'a_log': a_log, 'd_skip': d_skip,
            'ssd_norm_w': ssd_norm_w, 'w_ssd_out': w_ssd_out, 'w_attn_out': w_attn_out,
            'w_mix_out': w_mix_out, 'w_pe_gate': w_pe_gate, 'w_pe_proj': w_pe_proj}


def reference(x_prompt, x_sample, state_ssm, state_conv, cache_kv_w128, cache_kv_w512,
              cache_kv_w2048, p_prompt, p_sample, ln_w, ln_b, ffn_w13, ffn_w2, w_in, conv_w,
              conv_b, dt_bias, a_log, d_skip, ssd_norm_w, w_ssd_out, w_attn_out, w_mix_out,
              w_pe_gate, w_pe_proj):
    y_prompt, y_sample = x_prompt, x_sample
    ssm_p, conv_p, kv_p = [], [], [[], [], []]
    ssm_s, conv_s, kv_s = [], [], [[], [], []]
    for i in range(DEPTH):
        wts = (ln_w[i], ln_b[i], ffn_w13[i], ffn_w2[i], w_in[i], conv_w[i], conv_b[i], dt_bias[i],
               a_log[i], d_skip[i], ssd_norm_w[i], w_ssd_out[i], w_attn_out[i], w_mix_out[i],
               w_pe_gate[i], w_pe_proj[i])
        h0 = jnp.zeros((BATCH, N_SSD_HEADS, SSD_HEAD_DIM, D_STATE), jnp.float32)
        c0 = jnp.zeros((BATCH, CONV_WIDTH - 1, CONV_DIM), y_prompt.dtype)
        y_prompt, sp, cp, bp = trunk_layer(y_prompt, p_prompt[i], h0, c0, None, False, *wts)
        bufs = (cache_kv_w128[i], cache_kv_w512[i], cache_kv_w2048[i])
        y_sample, ss, cs, bs = trunk_layer(y_sample, p_sample[i], state_ssm[i], state_conv[i], bufs,
                                           True, *wts)
        ssm_p.append(sp)
        conv_p.append(cp)
        ssm_s.append(ss)
        conv_s.append(cs)
        for g in range(N_PATTERNS):
            kv_p[g].append(bp[g])
            kv_s[g].append(bs[g])
    new_ssm_prompt = jnp.stack(ssm_p)
    new_conv_prompt = jnp.stack(conv_p)
    new_kv_w128_prompt = jnp.stack(kv_p[0])
    new_kv_w512_prompt = jnp.stack(kv_p[1])
    new_kv_w2048_prompt = jnp.stack(kv_p[2])
    new_ssm_sample = jnp.stack(ssm_s)
    new_conv_sample = jnp.stack(conv_s)
    new_kv_w128_sample = jnp.stack(kv_s[0])
    new_kv_w512_sample = jnp.stack(kv_s[1])
    new_kv_w2048_sample = jnp.stack(kv_s[2])
    return (y_prompt, y_sample, new_ssm_prompt, new_conv_prompt, new_kv_w128_prompt,
            new_kv_w512_prompt, new_kv_w2048_prompt, new_ssm_sample, new_conv_sample,
            new_kv_w128_sample, new_kv_w512_sample, new_kv_w2048_sample)
```

```python
import functools
import math

import jax
import jax.numpy as jnp
from jax import lax
from jax.experimental import pallas as pl
from jax.experimental.pallas import tpu as pltpu

f32 = jnp.float32
bf16 = jnp.bfloat16

D_MODEL = 1024
D_INNER = 2 * D_MODEL
SSD_HEAD_DIM = 64
N_SSD_HEADS = D_INNER // SSD_HEAD_DIM
N_SSD_GROUPS = 4
HEADS_PER_GROUP = N_SSD_HEADS // N_SSD_GROUPS
GROUP_WIDTH = HEADS_PER_GROUP * SSD_HEAD_DIM
D_STATE = 128
CONV_WIDTH = 4
CONV_DIM = D_INNER + 2 * N_SSD_GROUPS * D_STATE
SSD_CHUNK = 128
ATTN_PATTERNS = ((128, 1), (512, 4), (2048, 16))
N_PATTERNS = 3
HEADS_PER_PATTERN = 4
ATTN_HEAD_DIM = 128
ATTN_WIDTH = HEADS_PER_PATTERN * ATTN_HEAD_DIM
ATTN_QKV_WIDTH = N_PATTERNS * ATTN_WIDTH
ATTN_BLOCK = 128
ALIBI_MAX_EXP = 8.0
D_FF = 2816
PLE_DIM = 256
LN_EPS = 1e-5
RMS_EPS = 1e-5
DEPTH = 1
DEEPNORM_ALPHA = (2.0 * DEPTH) ** 0.25
IN_SPLITS = (D_INNER, CONV_DIM, N_SSD_HEADS, ATTN_QKV_WIDTH, ATTN_QKV_WIDTH, ATTN_QKV_WIDTH)

LANES = 128
VMEM_LIMIT = 56 * 1024 * 1024
NEG_INF = float("-inf")


def _cparams(n_axes):
    return pltpu.CompilerParams(dimension_semantics=("arbitrary",) * n_axes, vmem_limit_bytes=VMEM_LIMIT)


def _resident(shape):
    nd = len(shape)
    return pl.BlockSpec(shape, lambda *_: (0,) * nd, pipeline_mode=pl.Buffered(1))


def _layer_norm(y, w, b):
    mu = jnp.mean(y, axis=-1, keepdims=True)
    yc = y - mu
    var = jnp.mean(yc * yc, axis=-1, keepdims=True)
    return yc * lax.rsqrt(var + LN_EPS) * w + b


def _silu(x):
    return x * jax.nn.sigmoid(x)


def _softplus(x):
    return jnp.maximum(x, 0.0) + jnp.log1p(jnp.exp(-jnp.abs(x)))


def _dot(a, b):
    return jnp.dot(a, b, preferred_element_type=f32)


def _dot_nt(a, b):
    return lax.dot_general(a, b, (((1,), (1,)), ((), ())), preferred_element_type=f32)


def _dot_exact(a, b):
    return jnp.dot(a, b, precision=lax.Precision.HIGHEST, preferred_element_type=f32)


def _ffn_ln_kernel(*refs, nf, with_pe):
    if with_pe:
        x_ref, w1_ref, w3_ref, w2_ref, lnw_ref, lnb_ref, p_ref, wpg_ref, wpp_ref, o_ref, xb_ref, acc_ref = refs
    else:
        x_ref, w1_ref, w3_ref, w2_ref, lnw_ref, lnb_ref, o_ref, xb_ref, acc_ref = refs
    f = pl.program_id(1)

    @pl.when(f == 0)
    def _():
        xb_ref[...] = x_ref[...].astype(bf16)
        acc_ref[...] = jnp.zeros_like(acc_ref)

    xb = xb_ref[...]
    gate = _dot(xb, w1_ref[...])
    up = _dot(xb, w3_ref[...])
    act = (_silu(gate) * up).astype(bf16)
    acc_ref[...] += _dot(act, w2_ref[...])

    @pl.when(f == nf - 1)
    def _():
        h = _layer_norm(DEEPNORM_ALPHA * x_ref[...] + 0.5 * acc_ref[...], lnw_ref[...], lnb_ref[...])
        if with_pe:
            pe_gate = jax.nn.sigmoid(_dot(h.astype(bf16), wpg_ref[...]))
            h = h + pe_gate * _dot(p_ref[...].astype(bf16), wpp_ref[...])
        o_ref[...] = h


def _ffn_ln(x, w13, w2, lnw, lnb, pe=None, *, tm):
    m = x.shape[0]
    nf = 2
    tf = D_FF // nf
    in_specs = [
        pl.BlockSpec((tm, D_MODEL), lambda i, f: (i, 0)),
        pl.BlockSpec((D_MODEL, tf), lambda i, f: (0, f)),
        pl.BlockSpec((D_MODEL, tf), lambda i, f: (0, f + nf)),
        pl.BlockSpec((tf, D_MODEL), lambda i, f: (f, 0)),
        _resident((1, D_MODEL)),
        _resident((1, D_MODEL)),
    ]
    args = [x, w13, w13, w2, lnw, lnb]
    if pe is not None:
        p, wpg, wpp = pe
        in_specs += [pl.BlockSpec((tm, PLE_DIM), lambda i, f: (i, 0)),
                     _resident((D_MODEL, D_MODEL)), _resident((PLE_DIM, D_MODEL))]
        args += [p, wpg, wpp]
    return pl.pallas_call(
        functools.partial(_ffn_ln_kernel, nf=nf, with_pe=pe is not None),
        grid=(m // tm, nf),
        in_specs=in_specs,
        out_specs=pl.BlockSpec((tm, D_MODEL), lambda i, f: (i, 0)),
        out_shape=jax.ShapeDtypeStruct((m, D_MODEL), f32),
        scratch_shapes=[pltpu.VMEM((tm, D_MODEL), bf16), pltpu.VMEM((tm, D_MODEL), f32)],
        compiler_params=_cparams(2),
        name="ffn_ln_pe" if pe is not None else "ffn_ln",
    )(*args)


def _matmul_kernel(x_ref, w_ref, o_ref):
    o_ref[...] = _dot(x_ref[...].astype(bf16), w_ref[...]).astype(o_ref.dtype)


def _matmul(x, w, out_dtype, *, tm):
    m, k = x.shape
    n = w.shape[1]
    return pl.pallas_call(
        _matmul_kernel,
        grid=(m // tm,),
        in_specs=[pl.BlockSpec((tm, k), lambda i: (i, 0)), _resident((k, n))],
        out_specs=pl.BlockSpec((tm, n), lambda i: (i, 0)),
        out_shape=jax.ShapeDtypeStruct((m, n), out_dtype),
        compiler_params=_cparams(1),
        name="proj_matmul",
    )(x, w)


def _ssd_prompt_kernel(z_ref, xbc_ref, h1_ref, wdt_ref, cw_ref, cb_ref, dtb_ref, alog_ref, dsk_ref, nw_ref,
                       e_ref, y_ref, so_ref,
                       xpad_ref, xc_ref, st_ref, cum_ref, cumt_ref, cume_ref, dte_ref, xw_ref, yacc_ref, *, nc):
    c = pl.program_id(1)
    q = SSD_CHUNK

    @pl.when(c == 0)
    def _():
        xpad_ref[0:8, :] = jnp.zeros((8, CONV_DIM), f32)
        st_ref[...] = jnp.zeros_like(st_ref)

    xpad_ref[8:8 + q, :] = xbc_ref[...].astype(f32)
    for j in range(CONV_DIM // 512):
        cs = slice(j * 512, (j + 1) * 512)
        acc = cb_ref[:, cs]
        for w in range(CONV_WIDTH):
            acc = acc + xpad_ref[pl.ds(8 - (CONV_WIDTH - 1) + w, q), cs] * cw_ref[w:w + 1, cs]
        xc_ref[:, cs] = _silu(acc)
    xpad_ref[0:8, :] = xpad_ref[q:q + 8, :]

    lane = lax.broadcasted_iota(jnp.int32, (q, LANES), 1)
    row = lax.broadcasted_iota(jnp.int32, (q, LANES), 0)
    dt = _softplus(_dot(h1_ref[...].astype(bf16), wdt_ref[...]) + dtb_ref[...])
    dt = jnp.where(lane < N_SSD_HEADS, dt, 0.0)
    d_a = dt * (-jnp.exp(alog_ref[...]))
    causal = row >= lane
    cum = _dot_exact(causal.astype(f32), d_a)
    cum_ref[...] = cum
    cumt_ref[...] = cum.T
    cume_ref[...] = _dot_exact(cum, e_ref[...])
    dte_ref[...] = _dot_exact(dt, e_ref[...])

    lo_half = lane < SSD_HEAD_DIM
    cume_last = cume_ref[q - 1:q, :]
    for g in range(N_SSD_GROUPS):
        b_g = xc_ref[:, D_INNER + g * D_STATE:D_INNER + (g + 1) * D_STATE].astype(bf16)
        c_g = xc_ref[:, D_INNER + (N_SSD_GROUPS + g) * D_STATE:D_INNER + (N_SSD_GROUPS + g + 1) * D_STATE].astype(bf16)
        cb = _dot_nt(c_g, b_g)
        y_off = _dot(c_g, st_ref[g].astype(bf16))
        for jj in range(HEADS_PER_GROUP // 2):
            j = g * (HEADS_PER_GROUP // 2) + jj
            cs = slice(j * LANES, (j + 1) * LANES)
            x_p = xc_ref[:, cs]
            cume = cume_ref[:, cs]
            xdt = x_p * dte_ref[:, cs]
            ls = []
            for h in (2 * j, 2 * j + 1):
                seg = cum_ref[:, h:h + 1] - cumt_ref[h:h + 1, :]
                ls.append(jnp.exp(jnp.where(causal, seg, NEG_INF)) * cb)
            l2 = jnp.concatenate(ls, axis=1).astype(bf16)
            x2 = jnp.concatenate([jnp.where(lo_half, xdt, 0.0), jnp.where(lo_half, 0.0, xdt)], axis=0).astype(bf16)
            y = _dot(l2, x2) + y_off[:, jj * LANES:(jj + 1) * LANES] * jnp.exp(cume) + dsk_ref[:, cs] * x_p
            yacc_ref[:, cs] = y * _silu(z_ref[:, cs].astype(f32))
            xw_ref[:, cs] = (xdt * jnp.exp(cume_last[:, cs] - cume)).astype(bf16)
        gs = slice(g * GROUP_WIDTH, (g + 1) * GROUP_WIDTH)
        b_t = xc_ref[:, D_INNER + g * D_STATE:D_INNER + (g + 1) * D_STATE].T.astype(bf16)
        st_ref[g] = st_ref[g] * jnp.exp(cume_last[:, gs]) + _dot(b_t, xw_ref[:, gs])

    y = yacc_ref[...]
    y = y * lax.rsqrt(jnp.mean(y * y, axis=-1, keepdims=True) + RMS_EPS) * nw_ref[...]
    y_ref[...] = y.astype(y_ref.dtype)

    @pl.when(c == nc - 1)
    def _():
        for g in range(N_SSD_GROUPS):
            so_ref[g * GROUP_WIDTH:(g + 1) * GROUP_WIDTH, :] = st_ref[g].T


def _ssd_prompt(z, xbc, h1, wdt, cw, cb, dtb, alog, dsk, nw, e, *, batch):
    m = z.shape[0]
    nc = m // batch // SSD_CHUNK
    q = SSD_CHUNK
    row_map = lambda b, c: (b * nc + c, 0)
    return pl.pallas_call(
        functools.partial(_ssd_prompt_kernel, nc=nc),
        grid=(batch, nc),
        in_specs=[pl.BlockSpec((q, D_INNER), row_map), pl.BlockSpec((q, CONV_DIM), row_map),
                  pl.BlockSpec((q, D_MODEL), row_map),
                  _resident(wdt.shape), _resident(cw.shape), _resident(cb.shape), _resident(dtb.shape),
                  _resident(alog.shape), _resident(dsk.shape), _resident(nw.shape), _resident(e.shape)],
        out_specs=[pl.BlockSpec((q, D_INNER), row_map),
                   pl.BlockSpec((None, D_INNER, D_STATE), lambda b, c: (b, 0, 0))],
        out_shape=[jax.ShapeDtypeStruct((m, D_INNER), bf16),
                   jax.ShapeDtypeStruct((batch, D_INNER, D_STATE), f32)],
        scratch_shapes=[pltpu.VMEM((q + 8, CONV_DIM), f32), pltpu.VMEM((q, CONV_DIM), f32),
                        pltpu.VMEM((N_SSD_GROUPS, D_STATE, GROUP_WIDTH), f32),
                        pltpu.VMEM((q, LANES), f32), pltpu.VMEM((LANES, q), f32),
                        pltpu.VMEM((q, D_INNER), f32), pltpu.VMEM((q, D_INNER), f32),
                        pltpu.VMEM((q, D_INNER), bf16), pltpu.VMEM((q, D_INNER), f32)],
        compiler_params=_cparams(2),
        name="ssd_prompt",
    )(z, xbc, h1, wdt, cw, cb, dtb, alog, dsk, nw, e)


def _ssd_decode_kernel(z_ref, xbc_ref, h1_ref, cst_ref, s_ref, wdt_ref, cw_ref, cb_ref, dtb_ref, alog_ref, dsk_ref,
                       nw_ref, e_ref, y_ref, so_ref, dect_ref, xdtt_ref, yt_ref, *, bb):
    acc = cb_ref[...] + xbc_ref[...] * cw_ref[CONV_WIDTH - 1:CONV_WIDTH, :]
    for w in range(CONV_WIDTH - 1):
        acc = acc + cst_ref[w] * cw_ref[w:w + 1, :]
    xc = _silu(acc)
    xs = xc[:, :D_INNER]
    lane = lax.broadcasted_iota(jnp.int32, (bb, LANES), 1)
    dt = _softplus(_dot(h1_ref[...].astype(bf16), wdt_ref[...]) + dtb_ref[...])
    dt = jnp.where(lane < N_SSD_HEADS, dt, 0.0)
    d_a = dt * (-jnp.exp(alog_ref[...]))
    decay_e = jnp.exp(_dot_exact(d_a, e_ref[...]))
    xdt = xs * _dot_exact(dt, e_ref[...])
    pad = jnp.zeros((LANES - bb, D_INNER), f32)
    dect_ref[...] = jnp.concatenate([decay_e, pad], axis=0).T
    xdtt_ref[...] = jnp.concatenate([xdt, pad], axis=0).T
    yt_ref[...] = jnp.zeros_like(yt_ref)
    lane_s = lax.broadcasted_iota(jnp.int32, (GROUP_WIDTH, LANES), 1)
    for j in range(bb):
        for g in range(N_SSD_GROUPS):
            gs = slice(g * GROUP_WIDTH, (g + 1) * GROUP_WIDTH)
            b_row = xc[j:j + 1, D_INNER + g * D_STATE:D_INNER + (g + 1) * D_STATE]
            c_row = xc[j:j + 1, D_INNER + (N_SSD_GROUPS + g) * D_STATE:D_INNER + (N_SSD_GROUPS + g + 1) * D_STATE]
            new = s_ref[j, gs, :] * dect_ref[gs, j:j + 1] + xdtt_ref[gs, j:j + 1] * b_row
            so_ref[j, gs, :] = new
            col = jnp.sum(new * c_row, axis=-1, keepdims=True)
            yt_ref[gs, :] = jnp.where(lane_s == j, col, yt_ref[gs, :])
    y = yt_ref[...].T[:bb, :] + dsk_ref[...] * xs
    y = y * _silu(z_ref[...])
    y = y * lax.rsqrt(jnp.mean(y * y, axis=-1, keepdims=True) + RMS_EPS) * nw_ref[...]
    y_ref[...] = y


def _ssd_decode(z, xbc, h1, conv_state, ssm_state, wdt, cw, cb, dtb, alog, dsk, nw, e, *, bb):
    n = z.shape[0]
    return pl.pallas_call(
        functools.partial(_ssd_decode_kernel, bb=bb),
        grid=(n // bb,),
        in_specs=[pl.BlockSpec((bb, D_INNER), lambda i: (i, 0)), pl.BlockSpec((bb, CONV_DIM), lambda i: (i, 0)),
                  pl.BlockSpec((bb, D_MODEL), lambda i: (i, 0)),
                  pl.BlockSpec((CONV_WIDTH - 1, bb, CONV_DIM), lambda i: (0, i, 0)),
                  pl.BlockSpec((bb, D_INNER, D_STATE), lambda i: (i, 0, 0)),
                  _resident(wdt.shape), _resident(cw.shape), _resident(cb.shape), _resident(dtb.shape),
                  _resident(alog.shape), _resident(dsk.shape), _resident(nw.shape), _resident(e.shape)],
        out_specs=[pl.BlockSpec((bb, D_INNER), lambda i: (i, 0)),
                   pl.BlockSpec((bb, D_INNER, D_STATE), lambda i: (i, 0, 0))],
        out_shape=[jax.ShapeDtypeStruct((n, D_INNER), f32),
                   jax.ShapeDtypeStruct((n, D_INNER, D_STATE), f32)],
        scratch_shapes=[pltpu.VMEM((D_INNER, LANES), f32)] * 3,
        compiler_params=_cparams(1),
        name="ssd_decode",
    )(z, xbc, h1, jnp.swapaxes(conv_state, 0, 1), ssm_state, wdt, cw, cb, dtb, alog, dsk, nw, e)


def _attn_prompt_kernel(slope_ref, q_ref, k_ref, v_ref, o_ref, st_ref, kprev_ref, vprev_ref, *, dilation, rows):
    n = pl.program_id(2)
    blk = ATTN_BLOCK

    @pl.when(n == 0)
    def _():
        kprev_ref[...] = jnp.zeros_like(kprev_ref)
        vprev_ref[...] = jnp.zeros_like(vprev_ref)

    a = lax.broadcasted_iota(jnp.int32, (blk, blk), 0)
    c = lax.broadcasted_iota(jnp.int32, (blk, blk), 1)
    dist_prev = (dilation * (blk + a - c)).astype(f32)
    dist_cur = (dilation * (a - c)).astype(f32)
    valid_cur = c <= a
    lane = lax.broadcasted_iota(jnp.int32, (blk, LANES), 1)
    scale = ATTN_HEAD_DIM ** -0.5
    for i in range(rows // blk):
        rs = slice(i * blk, (i + 1) * blk)
        valid_prev = (c >= a) if i > 0 else (c - a + jnp.where(n > 0, 0, -2 * blk)) >= 0
        stat = jnp.zeros((blk, LANES), f32)
        for h in range(HEADS_PER_PATTERN):
            hs = slice(h * ATTN_HEAD_DIM, (h + 1) * ATTN_HEAD_DIM)
            slope = slope_ref[h:h + 1, :]
            q = q_ref[rs, hs]
            if i > 0:
                kp = k_ref[(i - 1) * blk:i * blk, hs]
                vp = v_ref[(i - 1) * blk:i * blk, hs]
            else:
                kp = kprev_ref[:, hs]
                vp = vprev_ref[:, hs]
            s_p = jnp.where(valid_prev, _dot_nt(q, kp) * scale - slope * dist_prev, NEG_INF)
            s_c = jnp.where(valid_cur, _dot_nt(q, k_ref[rs, hs]) * scale - slope * dist_cur, NEG_INF)
            m = jnp.maximum(jnp.max(s_p, axis=-1, keepdims=True), jnp.max(s_c, axis=-1, keepdims=True))
            p_p = jnp.exp(s_p - m)
            p_c = jnp.exp(s_c - m)
            l = jnp.sum(p_p, axis=-1, keepdims=True) + jnp.sum(p_c, axis=-1, keepdims=True)
            o = (_dot(p_p.astype(bf16), vp) + _dot(p_c.astype(bf16), v_ref[rs, hs])) / l
            o_ref[rs, hs] = o.astype(o_ref.dtype)
            stat = jnp.where(lane == h, m + jnp.log(l), stat)
        st_ref[rs, :] = stat
    kprev_ref[...] = k_ref[rows - blk:rows, :]
    vprev_ref[...] = v_ref[rows - blk:rows, :]


def _attn_prompt(qkv, slopes, g, *, batch, seq):
    window, dilation = ATTN_PATTERNS[g]
    assert window // dilation == ATTN_BLOCK
    length = seq // dilation
    rows = min(length, 512)
    view = qkv.reshape(batch, length, dilation * 3 * ATTN_QKV_WIDTH)
    nblk = 3 * N_PATTERNS
    spec = lambda off: pl.BlockSpec((None, rows, ATTN_WIDTH), lambda b, r, n: (b, n, r * nblk + off + g))
    o, st = pl.pallas_call(
        functools.partial(_attn_prompt_kernel, dilation=dilation, rows=rows),
        grid=(batch, dilation, length // rows),
        in_specs=[_resident(slopes.shape), spec(0), spec(N_PATTERNS), spec(2 * N_PATTERNS)],
        out_specs=[pl.BlockSpec((None, rows, ATTN_WIDTH), lambda b, r, n: (b, n, r)),
                   pl.BlockSpec((None, rows, LANES), lambda b, r, n: (b, n, r))],
        out_shape=[jax.ShapeDtypeStruct((batch, length, dilation * ATTN_WIDTH), bf16),
                   jax.ShapeDtypeStruct((batch, length, dilation * LANES), f32)],
        scratch_shapes=[pltpu.VMEM((ATTN_BLOCK, ATTN_WIDTH), bf16), pltpu.VMEM((ATTN_BLOCK, ATTN_WIDTH), bf16)],
        compiler_params=_cparams(3),
        name=f"attn_prompt_d{dilation}",
    )(slopes, view, view, view)
    return o.reshape(batch * seq, ATTN_WIDTH), st.reshape(batch * seq, LANES)


def _attn_decode_kernel(bias_ref, qkv_ref, c0_ref, c1_ref, c2_ref, o_ref):
    scale = ATTN_HEAD_DIM ** -0.5
    outs, ms, ls = [], [], []
    for g, c_ref in enumerate((c0_ref, c1_ref, c2_ref)):
        q = qkv_ref[:, 0, g]
        k_new = qkv_ref[:, 1, g]
        v_new = qkv_ref[:, 2, g]
        s = jnp.sum(c_ref[:, :, 0] * q[:, None], axis=-1, keepdims=True) * scale + bias_ref[g]
        s_new = jnp.sum(k_new * q, axis=-1, keepdims=True) * scale
        m = jnp.maximum(jnp.max(s, axis=1), s_new)
        p = jnp.exp(s - m[:, None])
        p_new = jnp.exp(s_new - m)
        l = jnp.sum(p, axis=1) + p_new
        o = (jnp.sum(p * c_ref[:, :, 1], axis=1) + p_new * v_new) / l
        outs.append(o)
        ms.append(m)
        ls.append(l)
    m_all = jnp.maximum(jnp.maximum(ms[0], ms[1]), ms[2])
    wgts = [l * jnp.exp(m - m_all) for m, l in zip(ms, ls)]
    num = wgts[0] * outs[0] + wgts[1] * outs[1] + wgts[2] * outs[2]
    o_ref[...] = num / (wgts[0] + wgts[1] + wgts[2])


def _attn_decode(qkv, caches, bias, *, bb):
    n = qkv.shape[0]
    in_specs = [_resident(bias.shape),
                pl.BlockSpec((bb, 3, N_PATTERNS, HEADS_PER_PATTERN, ATTN_HEAD_DIM), lambda i: (i, 0, 0, 0, 0))]
    views = []
    for g, (window, dilation) in enumerate(ATTN_PATTERNS):
        assert caches[g].shape[1] == window and window // dilation == ATTN_BLOCK
        views.append(caches[g].reshape(n, ATTN_BLOCK, dilation, 2, HEADS_PER_PATTERN, ATTN_HEAD_DIM))
        in_specs.append(pl.BlockSpec((bb, ATTN_BLOCK, None, 2, HEADS_PER_PATTERN, ATTN_HEAD_DIM),
                                     lambda i: (i, 0, 0, 0, 0, 0)))
    return pl.pallas_call(
        _attn_decode_kernel,
        grid=(n // bb,),
        in_specs=in_specs,
        out_specs=pl.BlockSpec((bb, HEADS_PER_PATTERN, ATTN_HEAD_DIM), lambda i: (i, 0, 0)),
        out_shape=jax.ShapeDtypeStruct((n, HEADS_PER_PATTERN, ATTN_HEAD_DIM), f32),
        compiler_params=_cparams(1),
        name="attn_decode",
    )(bias, qkv, *views)


def _post_mix_kernel(*refs, n_pat):
    h1_ref, y_ref = refs[0], refs[1]
    o_refs = refs[2:2 + n_pat]
    st_refs = refs[2 + n_pat:2 + 2 * n_pat] if n_pat > 1 else ()
    wg_ref, wso_ref, wao_ref, wmo_ref, lnw_ref, lnb_ref, out_ref = refs[-7:]
    h1 = h1_ref[...]
    gates = jax.nn.sigmoid(_dot(h1.astype(bf16), wg_ref[...]))
    if n_pat > 1:
        heads = []
        for h in range(HEADS_PER_PATTERN):
            hs = slice(h * ATTN_HEAD_DIM, (h + 1) * ATTN_HEAD_DIM)
            lse = [st[:, h:h + 1] for st in st_refs]
            top = functools.reduce(jnp.maximum, lse)
            wgt = [jnp.exp(v - top) for v in lse]
            num = sum(w * o[:, hs].astype(f32) for w, o in zip(wgt, o_refs))
            heads.append(num / sum(wgt))
        attn = jnp.concatenate(heads, axis=1)
    else:
        attn = o_refs[0][...]
    t_ssd = _dot(y_ref[...].astype(bf16), wso_ref[...])
    t_attn = _dot(attn.astype(bf16), wao_ref[...])
    u = gates[:, :D_MODEL] * t_ssd + gates[:, D_MODEL:] * t_attn
    mix = _dot(u.astype(bf16), wmo_ref[...])
    out_ref[...] = _layer_norm(DEEPNORM_ALPHA * h1 + mix, lnw_ref[...], lnb_ref[...])


def _post_mix(h1, y_ssd, outs, stats, wg, wso, wao, wmo, lnw, lnb, *, tm):
    m = h1.shape[0]
    n_pat = len(outs)
    row = lambda width: pl.BlockSpec((tm, width), lambda i: (i, 0))
    in_specs = [row(D_MODEL), row(D_INNER)] + [row(ATTN_WIDTH)] * n_pat + [row(LANES)] * len(stats)
    in_specs += [_resident(w.shape) for w in (wg, wso, wao, wmo, lnw, lnb)]
    return pl.pallas_call(
        functools.partial(_post_mix_kernel, n_pat=n_pat),
        grid=(m // tm,),
        in_specs=in_specs,
        out_specs=row(D_MODEL),
        out_shape=jax.ShapeDtypeStruct((m, D_MODEL), f32),
        compiler_params=_cparams(1),
        name="post_mix",
    )(h1, y_ssd, *outs, *stats, wg, wso, wao, wmo, lnw, lnb)


def _kv_shift_kernel(*refs, n_chunks):
    caches, news, outs, sem = refs[0:3], refs[3:6], refs[6:9], refs[9]
    copies = []
    for g in range(N_PATTERNS):
        n, w = caches[g].shape[0], caches[g].shape[1]
        step = n // n_chunks
        for j in range(n_chunks):
            bs = pl.ds(j * step, step)
            copies.append(pltpu.make_async_copy(caches[g].at[bs, pl.ds(1, w - 1)], outs[g].at[bs, pl.ds(0, w - 1)],
                                                sem.at[g, j]))
        copies.append(pltpu.make_async_copy(news[g], outs[g].at[:, pl.ds(w - 1, 1)], sem.at[g, n_chunks]))
    for cp in copies:
        cp.start()
    for cp in copies:
        cp.wait()


def _kv_shift(caches, news):
    n_chunks = 8
    any_spec = pl.BlockSpec(memory_space=pl.ANY)
    return pl.pallas_call(
        functools.partial(_kv_shift_kernel, n_chunks=n_chunks),
        in_specs=[any_spec] * 6,
        out_specs=[any_spec] * 3,
        out_shape=[jax.ShapeDtypeStruct(c.shape, c.dtype) for c in caches],
        scratch_shapes=[pltpu.SemaphoreType.DMA((N_PATTERNS, n_chunks + 1))],
        name="kv_shift",
    )(*caches, *news)


def _alibi_slopes():
    n = N_PATTERNS * HEADS_PER_PATTERN
    e = jnp.arange(1, n + 1, dtype=f32)
    return (2.0 ** (-ALIBI_MAX_EXP * e / n)).reshape(N_PATTERNS, HEADS_PER_PATTERN)


def _prepare_weights(ln_w, ln_b, ffn_w13, ffn_w2, w_in, conv_w, conv_b, dt_bias, a_log, d_skip, ssd_norm_w,
                     w_ssd_out, w_attn_out, w_mix_out, w_pe_gate, w_pe_proj):
    cuts = [0]
    for s in IN_SPLITS:
        cuts.append(cuts[-1] + s)
    wz, wxbc, wdt, wq, wk, wv = (w_in[:, cuts[i]:cuts[i + 1]] for i in range(6))
    pad_heads = LANES - N_SSD_HEADS
    head_of_col = jnp.arange(D_INNER, dtype=jnp.int32) // SSD_HEAD_DIM
    return dict(
        ln_w=[ln_w[i:i + 1] for i in range(3)], ln_b=[ln_b[i:i + 1] for i in range(3)],
        w13=[ffn_w13[i].astype(bf16) for i in range(2)], w2=[ffn_w2[i].astype(bf16) for i in range(2)],
        wz=wz.astype(bf16), wxbc=wxbc.astype(bf16),
        wdt=jnp.pad(wdt, ((0, 0), (0, pad_heads))).astype(bf16),
        wqkv=jnp.concatenate([wq, wk, wv], axis=1).astype(bf16),
        wg=w_in[:, cuts[6]:].astype(bf16),
        cw=conv_w, cb=conv_b[None, :],
        dtb=jnp.pad(dt_bias, (0, pad_heads))[None, :], alog=jnp.pad(a_log, (0, pad_heads))[None, :],
        dsk=jnp.repeat(d_skip, SSD_HEAD_DIM)[None, :], nw=ssd_norm_w[None, :],
        expand=(jnp.arange(LANES, dtype=jnp.int32)[:, None] == head_of_col[None, :]).astype(f32),
        wso=w_ssd_out.astype(bf16), wao=w_attn_out.astype(bf16), wmo=w_mix_out.astype(bf16),
        wpg=w_pe_gate.astype(bf16), wpp=w_pe_proj.astype(bf16),
    )


def _ssd_args(w):
    return (w["wdt"], w["cw"], w["cb"], w["dtb"], w["alog"], w["dsk"], w["nw"], w["expand"])


def _layer_tail(h1, y_ssd, outs, stats, p, w, *, tm):
    h2 = _post_mix(h1, y_ssd, outs, stats, w["wg"], w["wso"], w["wao"], w["wmo"], w["ln_w"][1], w["ln_b"][1], tm=tm)
    return _ffn_ln(h2, w["w13"][1], w["w2"][1], w["ln_w"][2], w["ln_b"][2], pe=(p, w["wpg"], w["wpp"]), tm=tm)


def _prompt_layer(x, p, w):
    batch, seq, _ = x.shape
    assert seq % SSD_CHUNK == 0 and all(seq % (ATTN_BLOCK * d) == 0 and seq >= win for win, d in ATTN_PATTERNS)
    tm = 512
    x2 = x.reshape(batch * seq, D_MODEL)
    h1 = _ffn_ln(x2, w["w13"][0], w["w2"][0], w["ln_w"][0], w["ln_b"][0], tm=tm)
    z = _matmul(h1, w["wz"], bf16, tm=tm)
    xbc = _matmul(h1, w["wxbc"], bf16, tm=tm)
    qkv = _matmul(h1, w["wqkv"], bf16, tm=tm)
    y_ssd, new_ssm = _ssd_prompt(z, xbc, h1, *_ssd_args(w), batch=batch)
    slopes = _alibi_slopes()
    outs, stats = [], []
    for g in range(N_PATTERNS):
        o, st = _attn_prompt(qkv, jnp.broadcast_to(slopes[g][:, None], (HEADS_PER_PATTERN, LANES)), g,
                             batch=batch, seq=seq)
        outs.append(o)
        stats.append(st)
    y = _layer_tail(h1, y_ssd, outs, stats, p.reshape(batch * seq, PLE_DIM), w, tm=tm)
    new_conv = xbc.reshape(batch, seq, CONV_DIM)[:, seq - (CONV_WIDTH - 1):].astype(f32)
    qkv5 = qkv.reshape(batch, seq, 3, N_PATTERNS, HEADS_PER_PATTERN, ATTN_HEAD_DIM)
    new_kv = [qkv5[:, seq - win:, 1:3, g].astype(f32) for g, (win, _) in enumerate(ATTN_PATTERNS)]
    return (y.reshape(batch, seq, D_MODEL), new_ssm.reshape(batch, N_SSD_HEADS, SSD_HEAD_DIM, D_STATE),
            new_conv, new_kv)


def _decode_layer(x, p, ssm_state, conv_state, caches, w):
    n, t, _ = x.shape
    assert t == 1
    tm = n
    bb = 8
    h1 = _ffn_ln(x.reshape(n, D_MODEL), w["w13"][0], w["w2"][0], w["ln_w"][0], w["ln_b"][0], tm=tm)
    z = _matmul(h1, w["wz"], f32, tm=tm)
    xbc = _matmul(h1, w["wxbc"], f32, tm=tm)
    qkv = _matmul(h1, w["wqkv"], f32, tm=tm)
    y_ssd, new_ssm = _ssd_decode(z, xbc, h1, conv_state, ssm_state.reshape(n, D_INNER, D_STATE), *_ssd_args(w), bb=bb)
    slopes = _alibi_slopes()
    steps_back = (ATTN_BLOCK - jnp.arange(ATTN_BLOCK, dtype=jnp.int32))
    bias = jnp.stack([-slopes[g][None, :] * (d * steps_back).astype(f32)[:, None]
                      for g, (_, d) in enumerate(ATTN_PATTERNS)])
    bias = jnp.broadcast_to(bias[..., None], bias.shape + (1,))
    qkv5 = qkv.reshape(n, 3, N_PATTERNS, HEADS_PER_PATTERN, ATTN_HEAD_DIM)
    attn = _attn_decode(qkv5, caches, bias, bb=bb).reshape(n, ATTN_WIDTH)
    y = _layer_tail(h1, y_ssd, [attn], [], p.reshape(n, PLE_DIM), w, tm=tm)
    new_conv = jnp.concatenate([conv_state[:, 1:], xbc[:, None, :]], axis=1)
    news = [qkv5[:, 1:3, g][:, None] for g in range(N_PATTERNS)]
    new_kv = _kv_shift(caches, news)
    return y.reshape(n, 1, D_MODEL), new_ssm.reshape(n, N_SSD_HEADS, SSD_HEAD_DIM, D_STATE), new_conv, new_kv


def kernel(x_prompt, x_sample, state_ssm, state_conv, cache_kv_w128, cache_kv_w512, cache_kv_w2048, p_prompt, p_sample, ln_w, ln_b, ffn_w13, ffn_w2, w_in, conv_w, conv_b, dt_bias, a_log, d_skip, ssd_norm_w, w_ssd_out, w_attn_out, w_mix_out, w_pe_gate, w_pe_proj):
    assert ln_w.shape[0] == DEPTH
    w = _prepare_weights(ln_w[0], ln_b[0], ffn_w13[0], ffn_w2[0], w_in[0], conv_w[0], conv_b[0], dt_bias[0], a_log[0],
                         d_skip[0], ssd_norm_w[0], w_ssd_out[0], w_attn_out[0], w_mix_out[0], w_pe_gate[0],
                         w_pe_proj[0])
    y_p, ssm_p, conv_p, kv_p = _prompt_layer(x_prompt, p_prompt[0], w)
    caches = (cache_kv_w128[0], cache_kv_w512[0], cache_kv_w2048[0])
    y_s, ssm_s, conv_s, kv_s = _decode_layer(x_sample, p_sample[0], state_ssm[0], state_conv[0], caches, w)
    lead = lambda a: a[None]
    return (y_p, y_s, lead(ssm_p), lead(conv_p), lead(kv_p[0]), lead(kv_p[1]), lead(kv_p[2]),
            lead(ssm_s), lead(conv_s), lead(kv_s[0]), lead(kv_s[1]), lead(kv_s[2]))
```

```python
import functools
import math

import jax
import jax.numpy as jnp
from jax import lax
from jax.experimental import pallas as pl
from jax.experimental.pallas import tpu as pltpu

f32 = jnp.float32
bf16 = jnp.bfloat16

D_MODEL = 1024
D_INNER = 2 * D_MODEL
SSD_HEAD_DIM = 64
N_SSD_HEADS = D_INNER // SSD_HEAD_DIM
N_SSD_GROUPS = 4
HEADS_PER_GROUP = N_SSD_HEADS // N_SSD_GROUPS
GROUP_WIDTH = HEADS_PER_GROUP * SSD_HEAD_DIM
D_STATE = 128
CONV_WIDTH = 4
CONV_DIM = D_INNER + 2 * N_SSD_GROUPS * D_STATE
SSD_CHUNK = 128
ATTN_PATTERNS = ((128, 1), (512, 4), (2048, 16))
N_PATTERNS = 3
HEADS_PER_PATTERN = 4
ATTN_HEAD_DIM = 128
ATTN_WIDTH = HEADS_PER_PATTERN * ATTN_HEAD_DIM
ATTN_QKV_WIDTH = N_PATTERNS * ATTN_WIDTH
PATTERN_QKV_WIDTH = 3 * ATTN_WIDTH
ATTN_BLOCK = 128
ALIBI_MAX_EXP = 8.0
D_FF = 2816
PLE_DIM = 256
LN_EPS = 1e-5
RMS_EPS = 1e-5
DEPTH = 1
DEEPNORM_ALPHA = (2.0 * DEPTH) ** 0.25
IN_SPLITS = (D_INNER, CONV_DIM, N_SSD_HEADS, ATTN_QKV_WIDTH, ATTN_QKV_WIDTH, ATTN_QKV_WIDTH)

LANES = 128
VMEM_LIMIT = 56 * 1024 * 1024
NEG_INF = float("-inf")


def _cparams(n_axes):
    return pltpu.CompilerParams(dimension_semantics=("arbitrary",) * n_axes, vmem_limit_bytes=VMEM_LIMIT)


def _resident(shape):
    nd = len(shape)
    return pl.BlockSpec(shape, lambda *_: (0,) * nd, pipeline_mode=pl.Buffered(1))


def _layer_norm(y, w, b):
    mu = jnp.mean(y, axis=-1, keepdims=True)
    yc = y - mu
    var = jnp.mean(yc * yc, axis=-1, keepdims=True)
    return yc * lax.rsqrt(var + LN_EPS) * w + b


def _silu(x):
    return x * jax.nn.sigmoid(x)


def _softplus(x):
    return jnp.maximum(x, 0.0) + jnp.log1p(jnp.exp(-jnp.abs(x)))


def _dot(a, b):
    return jnp.dot(a, b, preferred_element_type=f32)


def _dot_nt(a, b):
    return lax.dot_general(a, b, (((1,), (1,)), ((), ())), preferred_element_type=f32)


def _dot_exact(a, b):
    return jnp.dot(a, b, precision=lax.Precision.HIGHEST, preferred_element_type=f32)


def _ffn_ln_kernel(*refs, nf, with_pe):
    if with_pe:
        x_ref, w1_ref, w3_ref, w2_ref, lnw_ref, lnb_ref, p_ref, wpg_ref, wpp_ref, o_ref, xb_ref, acc_ref = refs
    else:
        x_ref, w1_ref, w3_ref, w2_ref, lnw_ref, lnb_ref, o_ref, xb_ref, acc_ref = refs
    f = pl.program_id(1)

    @pl.when(f == 0)
    def _():
        xb_ref[...] = x_ref[...].astype(bf16)
        acc_ref[...] = jnp.zeros_like(acc_ref)

    xb = xb_ref[...]
    gate = _dot(xb, w1_ref[...])
    up = _dot(xb, w3_ref[...])
    act = (_silu(gate) * up).astype(bf16)
    acc_ref[...] += _dot(act, w2_ref[...])

    @pl.when(f == nf - 1)
    def _():
        h = _layer_norm(DEEPNORM_ALPHA * x_ref[...] + 0.5 * acc_ref[...], lnw_ref[...], lnb_ref[...])
        if with_pe:
            pe_gate = jax.nn.sigmoid(_dot(h.astype(bf16), wpg_ref[...]))
            h = h + pe_gate * _dot(p_ref[...].astype(bf16), wpp_ref[...])
        o_ref[...] = h


def _ffn_ln(x, w13, w2, lnw, lnb, pe=None, *, tm):
    m = x.shape[0]
    nf = 2
    tf = D_FF // nf
    in_specs = [
        pl.BlockSpec((tm, D_MODEL), lambda i, f: (i, 0)),
        pl.BlockSpec((D_MODEL, tf), lambda i, f: (0, f)),
        pl.BlockSpec((D_MODEL, tf), lambda i, f: (0, f + nf)),
        pl.BlockSpec((tf, D_MODEL), lambda i, f: (f, 0)),
        _resident((1, D_MODEL)),
        _resident((1, D_MODEL)),
    ]
    args = [x, w13, w13, w2, lnw, lnb]
    if pe is not None:
        p, wpg, wpp = pe
        in_specs += [pl.BlockSpec((tm, PLE_DIM), lambda i, f: (i, 0)),
                     _resident((D_MODEL, D_MODEL)), _resident((PLE_DIM, D_MODEL))]
        args += [p, wpg, wpp]
    return pl.pallas_call(
        functools.partial(_ffn_ln_kernel, nf=nf, with_pe=pe is not None),
        grid=(m // tm, nf),
        in_specs=in_specs,
        out_specs=pl.BlockSpec((tm, D_MODEL), lambda i, f: (i, 0)),
        out_shape=jax.ShapeDtypeStruct((m, D_MODEL), f32),
        scratch_shapes=[pltpu.VMEM((tm, D_MODEL), bf16), pltpu.VMEM((tm, D_MODEL), f32)],
        compiler_params=_cparams(2),
        name="ffn_ln_pe" if pe is not None else "ffn_ln",
    )(*args)


def _matmul_kernel(x_ref, w_ref, o_ref):
    o_ref[...] = _dot(x_ref[...].astype(bf16), w_ref[...]).astype(o_ref.dtype)


def _matmul(x, w, out_dtype, *, tm):
    m, k = x.shape
    n = w.shape[1]
    return pl.pallas_call(
        _matmul_kernel,
        grid=(m // tm,),
        in_specs=[pl.BlockSpec((tm, k), lambda i: (i, 0)), _resident((k, n))],
        out_specs=pl.BlockSpec((tm, n), lambda i: (i, 0)),
        out_shape=jax.ShapeDtypeStruct((m, n), out_dtype),
        compiler_params=_cparams(1),
        name="proj_matmul",
    )(x, w)


def _qkv_proj_kernel(x_ref, w_ref, *rest):
    outs, res_ref = rest[:N_PATTERNS], rest[N_PATTERNS]
    tm = x_ref.shape[0]
    xb = x_ref[...].astype(bf16)
    wide = 2 * LANES
    for c2 in range(N_PATTERNS * PATTERN_QKV_WIDTH // wide):
        res = _dot(xb, w_ref[:, c2 * wide:(c2 + 1) * wide])
        res_ref[2 * c2] = res[:, :LANES]
        res_ref[2 * c2 + 1] = res[:, LANES:]
    blocks = PATTERN_QKV_WIDTH // LANES
    for g, (_, d) in enumerate(ATTN_PATTERNS):
        for cb in range(blocks):
            cs = slice(cb * LANES, (cb + 1) * LANES)
            for r in range(d):
                outs[g][r, :, cs] = res_ref[g * blocks + cb, pl.ds(r, tm // d, stride=d), :].astype(bf16)


def _qkv_proj(x, w, *, batch, seq, tm):
    m, k = x.shape
    steps = seq // tm
    return pl.pallas_call(
        _qkv_proj_kernel,
        grid=(m // tm,),
        in_specs=[pl.BlockSpec((tm, k), lambda i: (i, 0)), _resident(w.shape)],
        out_specs=[pl.BlockSpec((None, d, tm // d, PATTERN_QKV_WIDTH), lambda i: (i // steps, 0, i % steps, 0))
                   for _, d in ATTN_PATTERNS],
        out_shape=[jax.ShapeDtypeStruct((batch, d, seq // d, PATTERN_QKV_WIDTH), bf16) for _, d in ATTN_PATTERNS],
        scratch_shapes=[pltpu.VMEM((N_PATTERNS * PATTERN_QKV_WIDTH // LANES, tm, LANES), f32)],
        compiler_params=_cparams(1),
        name="qkv_proj",
    )(x, w)


def _ssd_prompt_kernel(z_ref, xbc_ref, h1_ref, wdt_ref, cw_ref, cb_ref, dtb_ref, alog_ref, dsk_ref, nw_ref,
                       e_ref, y_ref, so_ref,
                       xpad_ref, xc_ref, st_ref, cum_ref, cumt_ref, cume_ref, dte_ref, xw_ref, yacc_ref, *, nc):
    c = pl.program_id(1)
    q = SSD_CHUNK

    @pl.when(c == 0)
    def _():
        xpad_ref[0:8, :] = jnp.zeros((8, CONV_DIM), f32)
        st_ref[...] = jnp.zeros_like(st_ref)

    xpad_ref[8:8 + q, :] = xbc_ref[...].astype(f32)
    for j in range(CONV_DIM // 512):
        cs = slice(j * 512, (j + 1) * 512)
        acc = cb_ref[:, cs]
        for w in range(CONV_WIDTH):
            acc = acc + xpad_ref[pl.ds(8 - (CONV_WIDTH - 1) + w, q), cs] * cw_ref[w:w + 1, cs]
        xc_ref[:, cs] = _silu(acc)
    xpad_ref[0:8, :] = xpad_ref[q:q + 8, :]

    lane = lax.broadcasted_iota(jnp.int32, (q, LANES), 1)
    row = lax.broadcasted_iota(jnp.int32, (q, LANES), 0)
    dt = _softplus(_dot(h1_ref[...].astype(bf16), wdt_ref[...]) + dtb_ref[...])
    dt = jnp.where(lane < N_SSD_HEADS, dt, 0.0)
    d_a = dt * (-jnp.exp(alog_ref[...]))
    causal = row >= lane
    cum = _dot_exact(causal.astype(f32), d_a)
    cum_ref[...] = cum
    cumt_ref[...] = cum.T
    cume_ref[...] = _dot_exact(cum, e_ref[...])
    dte_ref[...] = _dot_exact(dt, e_ref[...])

    lo_half = lane < SSD_HEAD_DIM
    cume_last = cume_ref[q - 1:q, :]
    for g in range(N_SSD_GROUPS):
        b_g = xc_ref[:, D_INNER + g * D_STATE:D_INNER + (g + 1) * D_STATE].astype(bf16)
        c_g = xc_ref[:, D_INNER + (N_SSD_GROUPS + g) * D_STATE:D_INNER + (N_SSD_GROUPS + g + 1) * D_STATE].astype(bf16)
        cb = _dot_nt(c_g, b_g)
        y_off = _dot(c_g, st_ref[g].astype(bf16))
        for jj in range(HEADS_PER_GROUP // 2):
            j = g * (HEADS_PER_GROUP // 2) + jj
            cs = slice(j * LANES, (j + 1) * LANES)
            x_p = xc_ref[:, cs]
            cume = cume_ref[:, cs]
            xdt = x_p * dte_ref[:, cs]
            ls = []
            for h in (2 * j, 2 * j + 1):
                seg = cum_ref[:, h:h + 1] - cumt_ref[h:h + 1, :]
                ls.append(jnp.exp(jnp.where(causal, seg, NEG_INF)) * cb)
            l2 = jnp.concatenate(ls, axis=1).astype(bf16)
            x2 = jnp.concatenate([jnp.where(lo_half, xdt, 0.0), jnp.where(lo_half, 0.0, xdt)], axis=0).astype(bf16)
            y = _dot(l2, x2) + y_off[:, jj * LANES:(jj + 1) * LANES] * jnp.exp(cume) + dsk_ref[:, cs] * x_p
            yacc_ref[:, cs] = y * _silu(z_ref[:, cs].astype(f32))
            xw_ref[:, cs] = (xdt * jnp.exp(cume_last[:, cs] - cume)).astype(bf16)
        gs = slice(g * GROUP_WIDTH, (g + 1) * GROUP_WIDTH)
        b_t = xc_ref[:, D_INNER + g * D_STATE:D_INNER + (g + 1) * D_STATE].T.astype(bf16)
        st_ref[g] = st_ref[g] * jnp.exp(cume_last[:, gs]) + _dot(b_t, xw_ref[:, gs])

    y = yacc_ref[...]
    y = y * lax.rsqrt(jnp.mean(y * y, axis=-1, keepdims=True) + RMS_EPS) * nw_ref[...]
    y_ref[...] = y.astype(y_ref.dtype)

    @pl.when(c == nc - 1)
    def _():
        for g in range(N_SSD_GROUPS):
            so_ref[g * GROUP_WIDTH:(g + 1) * GROUP_WIDTH, :] = st_ref[g].T


def _ssd_prompt(z, xbc, h1, wdt, cw, cb, dtb, alog, dsk, nw, e, *, batch):
    m = z.shape[0]
    nc = m // batch // SSD_CHUNK
    q = SSD_CHUNK
    row_map = lambda b, c: (b * nc + c, 0)
    return pl.pallas_call(
        functools.partial(_ssd_prompt_kernel, nc=nc),
        grid=(batch, nc),
        in_specs=[pl.BlockSpec((q, D_INNER), row_map), pl.BlockSpec((q, CONV_DIM), row_map),
                  pl.BlockSpec((q, D_MODEL), row_map),
                  _resident(wdt.shape), _resident(cw.shape), _resident(cb.shape), _resident(dtb.shape),
                  _resident(alog.shape), _resident(dsk.shape), _resident(nw.shape), _resident(e.shape)],
        out_specs=[pl.BlockSpec((q, D_INNER), row_map),
                   pl.BlockSpec((None, D_INNER, D_STATE), lambda b, c: (b, 0, 0))],
        out_shape=[jax.ShapeDtypeStruct((m, D_INNER), bf16),
                   jax.ShapeDtypeStruct((batch, D_INNER, D_STATE), f32)],
        scratch_shapes=[pltpu.VMEM((q + 8, CONV_DIM), f32), pltpu.VMEM((q, CONV_DIM), f32),
                        pltpu.VMEM((N_SSD_GROUPS, D_STATE, GROUP_WIDTH), f32),
                        pltpu.VMEM((q, LANES), f32), pltpu.VMEM((LANES, q), f32),
                        pltpu.VMEM((q, D_INNER), f32), pltpu.VMEM((q, D_INNER), f32),
                        pltpu.VMEM((q, D_INNER), bf16), pltpu.VMEM((q, D_INNER), f32)],
        compiler_params=_cparams(2),
        name="ssd_prompt",
    )(z, xbc, h1, wdt, cw, cb, dtb, alog, dsk, nw, e)


def _ssd_decode_kernel(z_ref, xbc_ref, h1_ref, cst_ref, s_ref, wdt_ref, cw_ref, cb_ref, dtb_ref, alog_ref, dsk_ref,
                       nw_ref, e_ref, y_ref, so_ref, dect_ref, xdtt_ref, yt_ref, *, bb):
    acc = cb_ref[...] + xbc_ref[...] * cw_ref[CONV_WIDTH - 1:CONV_WIDTH, :]
    for w in range(CONV_WIDTH - 1):
        acc = acc + cst_ref[w] * cw_ref[w:w + 1, :]
    xc = _silu(acc)
    xs = xc[:, :D_INNER]
    lane = lax.broadcasted_iota(jnp.int32, (bb, LANES), 1)
    dt = _softplus(_dot(h1_ref[...].astype(bf16), wdt_ref[...]) + dtb_ref[...])
    dt = jnp.where(lane < N_SSD_HEADS, dt, 0.0)
    d_a = dt * (-jnp.exp(alog_ref[...]))
    decay_e = jnp.exp(_dot_exact(d_a, e_ref[...]))
    xdt = xs * _dot_exact(dt, e_ref[...])
    pad = jnp.zeros((LANES - bb, D_INNER), f32)
    dect_ref[...] = jnp.concatenate([decay_e, pad], axis=0).T
    xdtt_ref[...] = jnp.concatenate([xdt, pad], axis=0).T
    yt_ref[...] = jnp.zeros_like(yt_ref)
    lane_s = lax.broadcasted_iota(jnp.int32, (GROUP_WIDTH, LANES), 1)
    for j in range(bb):
        for g in range(N_SSD_GROUPS):
            gs = slice(g * GROUP_WIDTH, (g + 1) * GROUP_WIDTH)
            b_row = xc[j:j + 1, D_INNER + g * D_STATE:D_INNER + (g + 1) * D_STATE]
            c_row = xc[j:j + 1, D_INNER + (N_SSD_GROUPS + g) * D_STATE:D_INNER + (N_SSD_GROUPS + g + 1) * D_STATE]
            new = s_ref[j, gs, :] * dect_ref[gs, j:j + 1] + xdtt_ref[gs, j:j + 1] * b_row
            so_ref[j, gs, :] = new
            col = jnp.sum(new * c_row, axis=-1, keepdims=True)
            yt_ref[gs, :] = jnp.where(lane_s == j, col, yt_ref[gs, :])
    y = yt_ref[...].T[:bb, :] + dsk_ref[...] * xs
    y = y * _silu(z_ref[...])
    y = y * lax.rsqrt(jnp.mean(y * y, axis=-1, keepdims=True) + RMS_EPS) * nw_ref[...]
    y_ref[...] = y


def _ssd_decode(z, xbc, h1, conv_state, ssm_state, wdt, cw, cb, dtb, alog, dsk, nw, e, *, bb):
    n = z.shape[0]
    return pl.pallas_call(
        functools.partial(_ssd_decode_kernel, bb=bb),
        grid=(n // bb,),
        in_specs=[pl.BlockSpec((bb, D_INNER), lambda i: (i, 0)), pl.BlockSpec((bb, CONV_DIM), lambda i: (i, 0)),
                  pl.BlockSpec((bb, D_MODEL), lambda i: (i, 0)),
                  pl.BlockSpec((CONV_WIDTH - 1, bb, CONV_DIM), lambda i: (0, i, 0)),
                  pl.BlockSpec((bb, D_INNER, D_STATE), lambda i: (i, 0, 0)),
                  _resident(wdt.shape), _resident(cw.shape), _resident(cb.shape), _resident(dtb.shape),
                  _resident(alog.shape), _resident(dsk.shape), _resident(nw.shape), _resident(e.shape)],
        out_specs=[pl.BlockSpec((bb, D_INNER), lambda i: (i, 0)),
                   pl.BlockSpec((bb, D_INNER, D_STATE), lambda i: (i, 0, 0))],
        out_shape=[jax.ShapeDtypeStruct((n, D_INNER), f32),
                   jax.ShapeDtypeStruct((n, D_INNER, D_STATE), f32)],
        scratch_shapes=[pltpu.VMEM((D_INNER, LANES), f32)] * 3,
        compiler_params=_cparams(1),
        name="ssd_decode",
    )(z, xbc, h1, jnp.swapaxes(conv_state, 0, 1), ssm_state, wdt, cw, cb, dtb, alog, dsk, nw, e)


def _attn_prompt_kernel(slope_ref, q_ref, k_ref, v_ref, o_ref, st_ref, kprev_ref, vprev_ref, *, dilation, rows):
    n = pl.program_id(2)
    blk = ATTN_BLOCK

    @pl.when(n == 0)
    def _():
        kprev_ref[...] = jnp.zeros_like(kprev_ref)
        vprev_ref[...] = jnp.zeros_like(vprev_ref)

    a = lax.broadcasted_iota(jnp.int32, (blk, blk), 0)
    c = lax.broadcasted_iota(jnp.int32, (blk, blk), 1)
    dist_prev = (dilation * (blk + a - c)).astype(f32)
    dist_cur = (dilation * (a - c)).astype(f32)
    valid_cur = c <= a
    lane = lax.broadcasted_iota(jnp.int32, (blk, LANES), 1)
    scale = ATTN_HEAD_DIM ** -0.5
    for i in range(rows // blk):
        rs = slice(i * blk, (i + 1) * blk)
        valid_prev = (c >= a) if i > 0 else (c - a + jnp.where(n > 0, 0, -2 * blk)) >= 0
        stat = jnp.zeros((blk, LANES), f32)
        for h in range(HEADS_PER_PATTERN):
            hs = slice(h * ATTN_HEAD_DIM, (h + 1) * ATTN_HEAD_DIM)
            slope = slope_ref[h:h + 1, :]
            q = q_ref[rs, hs]
            if i > 0:
                kp = k_ref[(i - 1) * blk:i * blk, hs]
                vp = v_ref[(i - 1) * blk:i * blk, hs]
            else:
                kp = kprev_ref[:, hs]
                vp = vprev_ref[:, hs]
            s_p = jnp.where(valid_prev, _dot_nt(q, kp) * scale - slope * dist_prev, NEG_INF)
            s_c = jnp.where(valid_cur, _dot_nt(q, k_ref[rs, hs]) * scale - slope * dist_cur, NEG_INF)
            m = jnp.maximum(jnp.max(s_p, axis=-1, keepdims=True), jnp.max(s_c, axis=-1, keepdims=True))
            p_p = jnp.exp(s_p - m)
            p_c = jnp.exp(s_c - m)
            l = jnp.sum(p_p, axis=-1, keepdims=True) + jnp.sum(p_c, axis=-1, keepdims=True)
            o = (_dot(p_p.astype(bf16), vp) + _dot(p_c.astype(bf16), v_ref[rs, hs])) / l
            o_ref[rs, hs] = o.astype(o_ref.dtype)
            stat = jnp.where(lane == h, m + jnp.log(l), stat)
        st_ref[rs, :] = stat
    kprev_ref[...] = k_ref[rows - blk:rows, :]
    vprev_ref[...] = v_ref[rows - blk:rows, :]


def _attn_prompt(qkv, slopes, g):
    window, dilation = ATTN_PATTERNS[g]
    assert window // dilation == ATTN_BLOCK
    batch, _, length, _ = qkv.shape
    rows = min(length, 512)
    spec = lambda width, off: pl.BlockSpec((None, None, rows, width), lambda b, r, n: (b, r, n, off))
    return pl.pallas_call(
        functools.partial(_attn_prompt_kernel, dilation=dilation, rows=rows),
        grid=(batch, dilation, length // rows),
        in_specs=[_resident(slopes.shape), spec(ATTN_WIDTH, 0), spec(ATTN_WIDTH, 1), spec(ATTN_WIDTH, 2)],
        out_specs=[spec(ATTN_WIDTH, 0), spec(LANES, 0)],
        out_shape=[jax.ShapeDtypeStruct((batch, dilation, length, ATTN_WIDTH), bf16),
                   jax.ShapeDtypeStruct((batch, dilation, length, LANES), f32)],
        scratch_shapes=[pltpu.VMEM((ATTN_BLOCK, ATTN_WIDTH), bf16), pltpu.VMEM((ATTN_BLOCK, ATTN_WIDTH), bf16)],
        compiler_params=_cparams(3),
        name=f"attn_prompt_d{dilation}",
    )(slopes, qkv, qkv, qkv)


def _attn_decode_kernel(bias_ref, qkv_ref, c0_ref, c1_ref, c2_ref, o_ref):
    scale = ATTN_HEAD_DIM ** -0.5
    outs, ms, ls = [], [], []
    for g, c_ref in enumerate((c0_ref, c1_ref, c2_ref)):
        q = qkv_ref[:, g, 0]
        k_new = qkv_ref[:, g, 1]
        v_new = qkv_ref[:, g, 2]
        s = jnp.sum(c_ref[:, :, 0] * q[:, None], axis=-1, keepdims=True) * scale + bias_ref[g]
        s_new = jnp.sum(k_new * q, axis=-1, keepdims=True) * scale
        m = jnp.maximum(jnp.max(s, axis=1), s_new)
        p = jnp.exp(s - m[:, None])
        p_new = jnp.exp(s_new - m)
        l = jnp.sum(p, axis=1) + p_new
        o = (jnp.sum(p * c_ref[:, :, 1], axis=1) + p_new * v_new) / l
        outs.append(o)
        ms.append(m)
        ls.append(l)
    m_all = jnp.maximum(jnp.maximum(ms[0], ms[1]), ms[2])
    wgts = [l * jnp.exp(m - m_all) for m, l in zip(ms, ls)]
    num = wgts[0] * outs[0] + wgts[1] * outs[1] + wgts[2] * outs[2]
    o_ref[...] = num / (wgts[0] + wgts[1] + wgts[2])


def _attn_decode(qkv, caches, bias, *, bb):
    n = qkv.shape[0]
    in_specs = [_resident(bias.shape),
                pl.BlockSpec((bb, N_PATTERNS, 3, HEADS_PER_PATTERN, ATTN_HEAD_DIM), lambda i: (i, 0, 0, 0, 0))]
    views = []
    for g, (window, dilation) in enumerate(ATTN_PATTERNS):
        assert caches[g].shape[1] == window and window // dilation == ATTN_BLOCK
        views.append(caches[g].reshape(n, ATTN_BLOCK, dilation, 2, HEADS_PER_PATTERN, ATTN_HEAD_DIM))
        in_specs.append(pl.BlockSpec((bb, ATTN_BLOCK, None, 2, HEADS_PER_PATTERN, ATTN_HEAD_DIM),
                                     lambda i: (i, 0, 0, 0, 0, 0)))
    return pl.pallas_call(
        _attn_decode_kernel,
        grid=(n // bb,),
        in_specs=in_specs,
        out_specs=pl.BlockSpec((bb, HEADS_PER_PATTERN, ATTN_HEAD_DIM), lambda i: (i, 0, 0)),
        out_shape=jax.ShapeDtypeStruct((n, HEADS_PER_PATTERN, ATTN_HEAD_DIM), f32),
        compiler_params=_cparams(1),
        name="attn_decode",
    )(bias, qkv, *views)


def _post_mix_kernel(*refs, dilations):
    n_pat = len(dilations)
    merged = n_pat > 1
    h1_ref, y_ref = refs[0], refs[1]
    o_refs = refs[2:2 + n_pat]
    st_refs = refs[2 + n_pat:2 + 2 * n_pat] if merged else ()
    k = 2 + n_pat + len(st_refs)
    wg_ref, wso_ref, wao_ref, wmo_ref, lnw_ref, lnb_ref, out_ref = refs[k:k + 7]
    scratch = refs[k + 7:]
    tm = h1_ref.shape[0]
    h1 = h1_ref[...]
    gates = jax.nn.sigmoid(_dot(h1.astype(bf16), wg_ref[...]))
    if merged:
        o_heads, lses = [], []
        for g, d in enumerate(dilations):
            if d == 1:
                o_heads.append([o_refs[g][:, h * ATTN_HEAD_DIM:(h + 1) * ATTN_HEAD_DIM].astype(f32)
                                for h in range(HEADS_PER_PATTERN)])
                lses.append(st_refs[g])
                continue
            o_scr, st_scr = scratch[0], scratch[1]
            scratch = scratch[2:]
            for r in range(d):
                st_scr[pl.ds(r, tm // d, stride=d), :] = st_refs[g][r]
                for h in range(HEADS_PER_PATTERN):
                    o_scr[h, pl.ds(r, tm // d, stride=d), :] = (
                        o_refs[g][r, :, h * ATTN_HEAD_DIM:(h + 1) * ATTN_HEAD_DIM].astype(f32))
            o_heads.append([o_scr[h] for h in range(HEADS_PER_PATTERN)])
            lses.append(st_scr)
        heads = []
        for h in range(HEADS_PER_PATTERN):
            lse = [st[:, h:h + 1] for st in lses]
            top = functools.reduce(jnp.maximum, lse)
            wgt = [jnp.exp(v - top) for v in lse]
            num = sum(w * o[h] for w, o in zip(wgt, o_heads))
            heads.append(num / sum(wgt))
        attn = jnp.concatenate(heads, axis=1)
    else:
        attn = o_refs[0][...]
    t_ssd = _dot(y_ref[...].astype(bf16), wso_ref[...])
    t_attn = _dot(attn.astype(bf16), wao_ref[...])
    u = gates[:, :D_MODEL] * t_ssd + gates[:, D_MODEL:] * t_attn
    mix = _dot(u.astype(bf16), wmo_ref[...])
    out_ref[...] = _layer_norm(DEEPNORM_ALPHA * h1 + mix, lnw_ref[...], lnb_ref[...])


def _post_mix(h1, y_ssd, outs, stats, dilations, wg, wso, wao, wmo, lnw, lnb, *, tm, seq):
    m = h1.shape[0]
    steps = seq // tm
    row = lambda width: pl.BlockSpec((tm, width), lambda i: (i, 0))

    def pattern_spec(width, d):
        if d == 1:
            return row(width)
        return pl.BlockSpec((None, d, tm // d, width), lambda i: (i // steps, 0, i % steps, 0))

    in_specs = [row(D_MODEL), row(D_INNER)] + [pattern_spec(ATTN_WIDTH, d) for d in dilations]
    in_specs += [pattern_spec(LANES, d) for d in dilations[:len(stats)]]
    in_specs += [_resident(w.shape) for w in (wg, wso, wao, wmo, lnw, lnb)]
    scratch = []
    for d in dilations:
        if d > 1:
            scratch += [pltpu.VMEM((HEADS_PER_PATTERN, tm, ATTN_HEAD_DIM), f32), pltpu.VMEM((tm, LANES), f32)]
    return pl.pallas_call(
        functools.partial(_post_mix_kernel, dilations=tuple(dilations)),
        grid=(m // tm,),
        in_specs=in_specs,
        out_specs=row(D_MODEL),
        out_shape=jax.ShapeDtypeStruct((m, D_MODEL), f32),
        scratch_shapes=scratch,
        compiler_params=_cparams(1),
        name="post_mix",
    )(h1, y_ssd, *outs, *stats, wg, wso, wao, wmo, lnw, lnb)


SHIFT_CHUNK = 256
SHIFT_BLOCK_BYTES = 8 * 1024 * 1024


def _kv_shift_kernel(c_ref, new_ref, o_ref, *, key_rows):
    bb, rows, _ = c_ref.shape
    keep = rows - key_rows
    n_chunks = pl.cdiv(keep, SHIFT_CHUNK)
    for b in range(bb):
        def body(k, carry):
            r0 = pl.multiple_of(jnp.minimum(k * SHIFT_CHUNK, keep - SHIFT_CHUNK), 8)
            o_ref[b, pl.ds(r0, SHIFT_CHUNK), :] = c_ref[b, pl.ds(r0 + key_rows, SHIFT_CHUNK), :]
            return carry
        lax.fori_loop(0, n_chunks, body, 0)
        o_ref[b, keep:rows, :] = new_ref[b]


def _kv_shift(cache, new):
    n, w = cache.shape[:2]
    key_rows = 2 * HEADS_PER_PATTERN * ATTN_HEAD_DIM // LANES
    rows = w * key_rows
    assert key_rows % 8 == 0 and rows - key_rows >= SHIFT_CHUNK
    bb = max(1, min(n, SHIFT_BLOCK_BYTES // (rows * LANES * 4)))
    assert n % bb == 0
    out = pl.pallas_call(
        functools.partial(_kv_shift_kernel, key_rows=key_rows),
        grid=(n // bb,),
        in_specs=[pl.BlockSpec((bb, rows, LANES), lambda i: (i, 0, 0)),
                  pl.BlockSpec((bb, key_rows, LANES), lambda i: (i, 0, 0))],
        out_specs=pl.BlockSpec((bb, rows, LANES), lambda i: (i, 0, 0)),
        out_shape=jax.ShapeDtypeStruct((n, rows, LANES), cache.dtype),
        compiler_params=_cparams(1),
        name=f"kv_shift_w{w}",
    )(cache.reshape(n, rows, LANES), new.reshape(n, key_rows, LANES))
    return out.reshape(cache.shape)


def _alibi_slopes():
    n = N_PATTERNS * HEADS_PER_PATTERN
    e = jnp.arange(1, n + 1, dtype=f32)
    return (2.0 ** (-ALIBI_MAX_EXP * e / n)).reshape(N_PATTERNS, HEADS_PER_PATTERN)


def _prepare_weights(ln_w, ln_b, ffn_w13, ffn_w2, w_in, conv_w, conv_b, dt_bias, a_log, d_skip, ssd_norm_w,
                     w_ssd_out, w_attn_out, w_mix_out, w_pe_gate, w_pe_proj):
    cuts = [0]
    for s in IN_SPLITS:
        cuts.append(cuts[-1] + s)
    wz, wxbc, wdt, wq, wk, wv = (w_in[:, cuts[i]:cuts[i + 1]] for i in range(6))
    pad_heads = LANES - N_SSD_HEADS
    head_of_col = jnp.arange(D_INNER, dtype=jnp.int32) // SSD_HEAD_DIM
    return dict(
        ln_w=[ln_w[i:i + 1] for i in range(3)], ln_b=[ln_b[i:i + 1] for i in range(3)],
        w13=[ffn_w13[i].astype(bf16) for i in range(2)], w2=[ffn_w2[i].astype(bf16) for i in range(2)],
        wz=wz.astype(bf16), wxbc=wxbc.astype(bf16),
        wdt=jnp.pad(wdt, ((0, 0), (0, pad_heads))).astype(bf16),
        wqkv=jnp.concatenate([m[:, g * ATTN_WIDTH:(g + 1) * ATTN_WIDTH] for g in range(N_PATTERNS)
                              for m in (wq, wk, wv)], axis=1).astype(bf16),
        wg=w_in[:, cuts[6]:].astype(bf16),
        cw=conv_w, cb=conv_b[None, :],
        dtb=jnp.pad(dt_bias, (0, pad_heads))[None, :], alog=jnp.pad(a_log, (0, pad_heads))[None, :],
        dsk=jnp.repeat(d_skip, SSD_HEAD_DIM)[None, :], nw=ssd_norm_w[None, :],
        expand=(jnp.arange(LANES, dtype=jnp.int32)[:, None] == head_of_col[None, :]).astype(f32),
        wso=w_ssd_out.astype(bf16), wao=w_attn_out.astype(bf16), wmo=w_mix_out.astype(bf16),
        wpg=w_pe_gate.astype(bf16), wpp=w_pe_proj.astype(bf16),
    )


def _ssd_args(w):
    return (w["wdt"], w["cw"], w["cb"], w["dtb"], w["alog"], w["dsk"], w["nw"], w["expand"])


def _layer_tail(h1, y_ssd, outs, stats, dilations, p, w, *, tm, seq):
    h2 = _post_mix(h1, y_ssd, outs, stats, dilations, w["wg"], w["wso"], w["wao"], w["wmo"], w["ln_w"][1],
                   w["ln_b"][1], tm=tm, seq=seq)
    return _ffn_ln(h2, w["w13"][1], w["w2"][1], w["ln_w"][2], w["ln_b"][2], pe=(p, w["wpg"], w["wpp"]), tm=tm)


def _prompt_layer(x, p, w):
    batch, seq, _ = x.shape
    assert seq % SSD_CHUNK == 0 and all(seq % (ATTN_BLOCK * d) == 0 and seq >= win for win, d in ATTN_PATTERNS)
    tm = 512
    x2 = x.reshape(batch * seq, D_MODEL)
    h1 = _ffn_ln(x2, w["w13"][0], w["w2"][0], w["ln_w"][0], w["ln_b"][0], tm=tm)
    z = _matmul(h1, w["wz"], bf16, tm=tm)
    xbc = _matmul(h1, w["wxbc"], bf16, tm=tm)
    qkv = _qkv_proj(h1, w["wqkv"], batch=batch, seq=seq, tm=tm)
    y_ssd, new_ssm = _ssd_prompt(z, xbc, h1, *_ssd_args(w), batch=batch)
    slopes = _alibi_slopes()
    dilations = [d for _, d in ATTN_PATTERNS]
    outs, stats, new_kv = [], [], []
    for g, (win, d) in enumerate(ATTN_PATTERNS):
        o, st = _attn_prompt(qkv[g], jnp.broadcast_to(slopes[g][:, None], (HEADS_PER_PATTERN, LANES)), g)
        if d == 1:
            o, st = o.reshape(batch * seq, ATTN_WIDTH), st.reshape(batch * seq, LANES)
        outs.append(o)
        stats.append(st)
        tail = qkv[g][:, :, seq // d - win // d:, ATTN_WIDTH:]
        new_kv.append(jnp.swapaxes(tail, 1, 2).reshape(batch, win, 2, HEADS_PER_PATTERN, ATTN_HEAD_DIM).astype(f32))
    y = _layer_tail(h1, y_ssd, outs, stats, dilations, p.reshape(batch * seq, PLE_DIM), w, tm=tm, seq=seq)
    new_conv = xbc.reshape(batch, seq, CONV_DIM)[:, seq - (CONV_WIDTH - 1):].astype(f32)
    return (y.reshape(batch, seq, D_MODEL), new_ssm.reshape(batch, N_SSD_HEADS, SSD_HEAD_DIM, D_STATE),
            new_conv, new_kv)


def _decode_layer(x, p, ssm_state, conv_state, caches, w):
    n, t, _ = x.shape
    assert t == 1
    tm = n
    bb = 8
    h1 = _ffn_ln(x.reshape(n, D_MODEL), w["w13"][0], w["w2"][0], w["ln_w"][0], w["ln_b"][0], tm=tm)
    z = _matmul(h1, w["wz"], f32, tm=tm)
    xbc = _matmul(h1, w["wxbc"], f32, tm=tm)
    qkv = _matmul(h1, w["wqkv"], f32, tm=tm)
    y_ssd, new_ssm = _ssd_decode(z, xbc, h1, conv_state, ssm_state.reshape(n, D_INNER, D_STATE), *_ssd_args(w), bb=bb)
    slopes = _alibi_slopes()
    steps_back = (ATTN_BLOCK - jnp.arange(ATTN_BLOCK, dtype=jnp.int32))
    bias = jnp.stack([-slopes[g][None, :] * (d * steps_back).astype(f32)[:, None]
                      for g, (_, d) in enumerate(ATTN_PATTERNS)])
    bias = jnp.broadcast_to(bias[..., None], bias.shape + (1,))
    qkv5 = qkv.reshape(n, N_PATTERNS, 3, HEADS_PER_PATTERN, ATTN_HEAD_DIM)
    attn = _attn_decode(qkv5, caches, bias, bb=bb).reshape(n, ATTN_WIDTH)
    y = _layer_tail(h1, y_ssd, [attn], [], [1], p.reshape(n, PLE_DIM), w, tm=tm, seq=n)
    new_conv = jnp.concatenate([conv_state[:, 1:], xbc[:, None, :]], axis=1)
    new_kv = [_kv_shift(caches[g], qkv5[:, g, 1:3]) for g in range(N_PATTERNS)]
    return y.reshape(n, 1, D_MODEL), new_ssm.reshape(n, N_SSD_HEADS, SSD_HEAD_DIM, D_STATE), new_conv, new_kv


def kernel(x_prompt, x_sample, state_ssm, state_conv, cache_kv_w128, cache_kv_w512, cache_kv_w2048, p_prompt, p_sample, ln_w, ln_b, ffn_w13, ffn_w2, w_in, conv_w, conv_b, dt_bias, a_log, d_skip, ssd_norm_w, w_ssd_out, w_attn_out, w_mix_out, w_pe_gate, w_pe_proj):
    assert ln_w.shape[0] == DEPTH
    w = _prepare_weights(ln_w[0], ln_b[0], ffn_w13[0], ffn_w2[0], w_in[0], conv_w[0], conv_b[0], dt_bias[0], a_log[0],
                         d_skip[0], ssd_norm_w[0], w_ssd_out[0], w_attn_out[0], w_mix_out[0], w_pe_gate[0],
                         w_pe_proj[0])
    y_p, ssm_p, conv_p, kv_p = _prompt_layer(x_prompt, p_prompt[0], w)
    caches = (cache_kv_w128[0], cache_kv_w512[0], cache_kv_w2048[0])
    y_s, ssm_s, conv_s, kv_s = _decode_layer(x_sample, p_sample[0], state_ssm[0], state_conv[0], caches, w)
    lead = lambda a: a[None]
    return (y_p, y_s, lead(ssm_p), lead(conv_p), lead(kv_p[0]), lead(kv_p[1]), lead(kv_p[2]),
            lead(ssm_s), lead(conv_s), lead(kv_s[0]), lead(kv_s[1]), lead(kv_s[2]))
```

```python
import functools

import jax
import jax.numpy as jnp
from jax import lax
from jax.experimental import pallas as pl
from jax.experimental.pallas import tpu as pltpu

f32 = jnp.float32
bf16 = jnp.bfloat16

D_MODEL = 1024
D_INNER = 2 * D_MODEL
SSD_HEAD_DIM = 64
N_SSD_HEADS = D_INNER // SSD_HEAD_DIM
N_SSD_GROUPS = 4
HEADS_PER_GROUP = N_SSD_HEADS // N_SSD_GROUPS
GROUP_WIDTH = HEADS_PER_GROUP * SSD_HEAD_DIM
D_STATE = 128
CONV_WIDTH = 4
CONV_DIM = D_INNER + 2 * N_SSD_GROUPS * D_STATE
SSD_CHUNK = 128
ATTN_PATTERNS = ((128, 1), (512, 4), (2048, 16))
N_PATTERNS = 3
HEADS_PER_PATTERN = 4
ATTN_HEAD_DIM = 128
ATTN_WIDTH = HEADS_PER_PATTERN * ATTN_HEAD_DIM
ATTN_QKV_WIDTH = N_PATTERNS * ATTN_WIDTH
PATTERN_QKV_WIDTH = 3 * ATTN_WIDTH
ATTN_BLOCK = 128
ALIBI_MAX_EXP = 8.0
D_FF = 2816
PLE_DIM = 256
LN_EPS = 1e-5
RMS_EPS = 1e-5
DEPTH = 1
DEEPNORM_ALPHA = (2.0 * DEPTH) ** 0.25
IN_SPLITS = (D_INNER, CONV_DIM, N_SSD_HEADS, ATTN_QKV_WIDTH, ATTN_QKV_WIDTH, ATTN_QKV_WIDTH)

LANES = 128
VMEM_LIMIT = 56 * 1024 * 1024
NEG_INF = float("-inf")


def _cparams(n_axes):
    return pltpu.CompilerParams(dimension_semantics=("arbitrary",) * n_axes, vmem_limit_bytes=VMEM_LIMIT)


def _resident(shape):
    nd = len(shape)
    return pl.BlockSpec(shape, lambda *_: (0,) * nd, pipeline_mode=pl.Buffered(1))


def _layer_norm(y, w, b):
    mu = jnp.mean(y, axis=-1, keepdims=True)
    yc = y - mu
    var = jnp.mean(yc * yc, axis=-1, keepdims=True)
    return yc * lax.rsqrt(var + LN_EPS) * w + b


def _silu(x):
    return x * jax.nn.sigmoid(x)


def _softplus(x):
    return jnp.maximum(x, 0.0) + jnp.log1p(jnp.exp(-jnp.abs(x)))


def _dot(a, b):
    return jnp.dot(a, b, preferred_element_type=f32)


def _dot_nt(a, b):
    return lax.dot_general(a, b, (((1,), (1,)), ((), ())), preferred_element_type=f32)


def _dot_tn(a, b):
    return lax.dot_general(a, b, (((0,), (0,)), ((), ())), preferred_element_type=f32)


def _dot_exact(a, b):
    return jnp.dot(a, b, precision=lax.Precision.HIGHEST, preferred_element_type=f32)


FFN_CHUNKS = (512, 512, 512, 512, 512, 256)


def _ffn_ln_kernel(*refs, with_pe):
    if with_pe:
        x_ref, w13_ref, w2_ref, lnw_ref, lnb_ref, p_ref, wpg_ref, wpp_ref, o_ref = refs
    else:
        x_ref, w13_ref, w2_ref, lnw_ref, lnb_ref, o_ref = refs
    x = x_ref[...]
    xb = x.astype(bf16)
    acc = None
    c0 = 0
    for width in FFN_CHUNKS:
        gate = _dot(xb, w13_ref[:, c0:c0 + width])
        up = _dot(xb, w13_ref[:, D_FF + c0:D_FF + c0 + width])
        part = _dot((_silu(gate) * up).astype(bf16), w2_ref[c0:c0 + width, :])
        acc = part if acc is None else acc + part
        c0 += width
    h = _layer_norm(DEEPNORM_ALPHA * x + 0.5 * acc, lnw_ref[...], lnb_ref[...])
    if with_pe:
        pe_gate = jax.nn.sigmoid(_dot(h.astype(bf16), wpg_ref[...]))
        h = h + pe_gate * _dot(p_ref[...].astype(bf16), wpp_ref[...])
    o_ref[...] = h


def _ffn_ln(x, w13, w2, lnw, lnb, pe=None, *, tm):
    m = x.shape[0]
    assert sum(FFN_CHUNKS) == D_FF
    row = lambda width: pl.BlockSpec((tm, width), lambda i: (i, 0))
    in_specs = [row(D_MODEL), _resident(w13.shape), _resident(w2.shape), _resident(lnw.shape), _resident(lnb.shape)]
    args = [x, w13, w2, lnw, lnb]
    if pe is not None:
        p, wpg, wpp = pe
        in_specs += [row(PLE_DIM), _resident(wpg.shape), _resident(wpp.shape)]
        args += [p, wpg, wpp]
    return pl.pallas_call(
        functools.partial(_ffn_ln_kernel, with_pe=pe is not None),
        grid=(m // tm,),
        in_specs=in_specs,
        out_specs=row(D_MODEL),
        out_shape=jax.ShapeDtypeStruct((m, D_MODEL), f32),
        compiler_params=_cparams(1),
        name="ffn_ln_pe" if pe is not None else "ffn_ln",
    )(*args)


def _matmul_kernel(x_ref, w_ref, o_ref):
    o_ref[...] = _dot(x_ref[...].astype(bf16), w_ref[...]).astype(o_ref.dtype)


def _matmul(x, w, out_dtype, *, tm):
    m, k = x.shape
    n = w.shape[1]
    return pl.pallas_call(
        _matmul_kernel,
        grid=(m // tm,),
        in_specs=[pl.BlockSpec((tm, k), lambda i: (i, 0)), _resident((k, n))],
        out_specs=pl.BlockSpec((tm, n), lambda i: (i, 0)),
        out_shape=jax.ShapeDtypeStruct((m, n), out_dtype),
        compiler_params=_cparams(1),
        name="proj_matmul",
    )(x, w)


def _qkv_proj_kernel(x_ref, w_ref, *rest):
    outs, res_ref = rest[:N_PATTERNS], rest[N_PATTERNS]
    tm = x_ref.shape[0]
    xb = x_ref[...].astype(bf16)
    wide = 2 * LANES
    for c2 in range(N_PATTERNS * PATTERN_QKV_WIDTH // wide):
        res = _dot(xb, w_ref[:, c2 * wide:(c2 + 1) * wide])
        res_ref[2 * c2] = res[:, :LANES]
        res_ref[2 * c2 + 1] = res[:, LANES:]
    blocks = PATTERN_QKV_WIDTH // LANES
    for g, (_, d) in enumerate(ATTN_PATTERNS):
        for cb in range(blocks):
            cs = slice(cb * LANES, (cb + 1) * LANES)
            for r in range(d):
                outs[g][r, :, cs] = res_ref[g * blocks + cb, pl.ds(r, tm // d, stride=d), :].astype(bf16)


def _qkv_proj(x, w, *, batch, seq, tm):
    m, k = x.shape
    steps = seq // tm
    return pl.pallas_call(
        _qkv_proj_kernel,
        grid=(m // tm,),
        in_specs=[pl.BlockSpec((tm, k), lambda i: (i, 0)), _resident(w.shape)],
        out_specs=[pl.BlockSpec((None, d, tm // d, PATTERN_QKV_WIDTH), lambda i: (i // steps, 0, i % steps, 0))
                   for _, d in ATTN_PATTERNS],
        out_shape=[jax.ShapeDtypeStruct((batch, d, seq // d, PATTERN_QKV_WIDTH), bf16) for _, d in ATTN_PATTERNS],
        scratch_shapes=[pltpu.VMEM((N_PATTERNS * PATTERN_QKV_WIDTH // LANES, tm, LANES), f32)],
        compiler_params=_cparams(1),
        name="qkv_proj",
    )(x, w)


CONV_ROWS = 64


def _proj_zx_kernel(x_ref, wz_ref, wx_ref, cw_ref, cb_ref, z_ref, xc_ref, tail_ref, xpad_ref, *, steps):
    i = pl.program_id(0)
    tm = x_ref.shape[0]
    halo = 8
    xb = x_ref[...].astype(bf16)
    z_ref[...] = _dot(xb, wz_ref[...]).astype(z_ref.dtype)

    @pl.when(i % steps == 0)
    def _():
        xpad_ref[0:halo, :] = jnp.zeros((halo, CONV_DIM), f32)

    for j in range(CONV_DIM // 512):
        xpad_ref[halo:halo + tm, j * 512:(j + 1) * 512] = _dot(xb, wx_ref[:, j * 512:(j + 1) * 512])
        for cb in range(2 * j, 2 * j + 2):
            cs = slice(cb * 256, (cb + 1) * 256)
            taps = [cw_ref[w:w + 1, cs] for w in range(CONV_WIDTH)]
            bias = cb_ref[:, cs]
            for r0 in range(0, tm, CONV_ROWS):
                blk = xpad_ref[r0:r0 + halo + CONV_ROWS, cs]
                acc = bias + blk[halo:] * taps[CONV_WIDTH - 1]
                for k in range(1, CONV_WIDTH):
                    acc = acc + pltpu.roll(blk, k, 0)[halo:] * taps[CONV_WIDTH - 1 - k]
                xc_ref[r0:r0 + CONV_ROWS, cs] = _silu(acc).astype(xc_ref.dtype)
    tail_ref[...] = xpad_ref[tm:tm + halo, :]
    xpad_ref[0:halo, :] = xpad_ref[tm:tm + halo, :]


def _proj_zx(x, wz, wx, cw, cb, *, batch, seq, tm):
    m, k = x.shape
    steps = seq // tm
    return pl.pallas_call(
        functools.partial(_proj_zx_kernel, steps=steps),
        grid=(m // tm,),
        in_specs=[pl.BlockSpec((tm, k), lambda i: (i, 0)), _resident(wz.shape), _resident(wx.shape),
                  _resident(cw.shape), _resident(cb.shape)],
        out_specs=[pl.BlockSpec((tm, D_INNER), lambda i: (i, 0)), pl.BlockSpec((tm, CONV_DIM), lambda i: (i, 0)),
                   pl.BlockSpec((None, 8, CONV_DIM), lambda i: (i // steps, 0, 0))],
        out_shape=[jax.ShapeDtypeStruct((m, D_INNER), bf16), jax.ShapeDtypeStruct((m, CONV_DIM), bf16),
                   jax.ShapeDtypeStruct((batch, 8, CONV_DIM), f32)],
        scratch_shapes=[pltpu.VMEM((tm + 8, CONV_DIM), f32)],
        compiler_params=_cparams(1),
        name="proj_zx",
    )(x, wz, wx, cw, cb)


def _split3(x):
    hi = x.astype(bf16)
    r = x - hi.astype(f32)
    mid = r.astype(bf16)
    return hi, mid, (r - mid.astype(f32)).astype(bf16)


def _ssd_prompt_kernel(z_ref, xc_ref, h1_ref, wdt_ref, dtb_ref, alog_ref, dsk_ref, nw_ref, y_ref, so_ref,
                       st_ref, dt_ref, cum_ref, cumt_ref, xw_ref, yacc_ref, *, nc):
    c = pl.program_id(1)
    q = SSD_CHUNK

    @pl.when(c == 0)
    def _():
        st_ref[...] = jnp.zeros_like(st_ref)

    lane = lax.broadcasted_iota(jnp.int32, (q, LANES), 1)
    row = lax.broadcasted_iota(jnp.int32, (q, LANES), 0)
    dt = _softplus(_dot(h1_ref[...].astype(bf16), wdt_ref[...]) + dtb_ref[...])
    dt = jnp.where(lane < N_SSD_HEADS, dt, 0.0)
    d_a = dt * (-jnp.exp(alog_ref[...]))
    causal = row >= lane
    c3 = _dot(causal.astype(bf16), jnp.concatenate(_split3(d_a), axis=1))
    cum = c3[:, :LANES] + c3[:, LANES:2 * LANES] + c3[:, 2 * LANES:]
    dt_ref[...] = dt
    cum_ref[...] = cum
    cumt_ref[...] = cum.T

    lo_half = lane < SSD_HEAD_DIM
    lo_half_row = lo_half[0:1, :]
    for g in range(N_SSD_GROUPS):
        b_g = xc_ref[:, D_INNER + g * D_STATE:D_INNER + (g + 1) * D_STATE]
        c_g = xc_ref[:, D_INNER + (N_SSD_GROUPS + g) * D_STATE:D_INNER + (N_SSD_GROUPS + g + 1) * D_STATE]
        cb = _dot_nt(c_g, b_g)
        y_off = _dot(c_g, st_ref[g].astype(bf16))
        chunk_decay = []
        for jj in range(HEADS_PER_GROUP // 2):
            j = g * (HEADS_PER_GROUP // 2) + jj
            cs = slice(j * LANES, (j + 1) * LANES)
            ha, hb = 2 * j, 2 * j + 1
            x_p = xc_ref[:, cs].astype(f32)
            cum_a, cum_b = cum_ref[:, ha:ha + 1], cum_ref[:, hb:hb + 1]
            cume = jnp.where(lo_half, cum_a, cum_b)
            cume_last = jnp.where(lo_half_row, cum_a[q - 1:q, :], cum_b[q - 1:q, :])
            xdt = x_p * jnp.where(lo_half, dt_ref[:, ha:ha + 1], dt_ref[:, hb:hb + 1])
            ls = []
            for h, cum_h in ((ha, cum_a), (hb, cum_b)):
                seg = cum_h - cumt_ref[h:h + 1, :]
                ls.append(jnp.exp(jnp.where(causal, seg, NEG_INF)) * cb)
            l2 = jnp.concatenate(ls, axis=1).astype(bf16)
            x2 = jnp.concatenate([jnp.where(lo_half, xdt, 0.0), jnp.where(lo_half, 0.0, xdt)], axis=0).astype(bf16)
            y = _dot(l2, x2) + y_off[:, jj * LANES:(jj + 1) * LANES] * jnp.exp(cume) + dsk_ref[:, cs] * x_p
            yacc_ref[:, cs] = y * _silu(z_ref[:, cs].astype(f32))
            xw_ref[:, cs] = (xdt * jnp.exp(cume_last - cume)).astype(bf16)
            chunk_decay.append(jnp.exp(cume_last))
        gs = slice(g * GROUP_WIDTH, (g + 1) * GROUP_WIDTH)
        st_ref[g] = st_ref[g] * jnp.concatenate(chunk_decay, axis=1) + _dot_tn(b_g, xw_ref[:, gs])

    y = yacc_ref[...]
    y = y * lax.rsqrt(jnp.mean(y * y, axis=-1, keepdims=True) + RMS_EPS) * nw_ref[...]
    y_ref[...] = y.astype(y_ref.dtype)

    @pl.when(c == nc - 1)
    def _():
        for g in range(N_SSD_GROUPS):
            so_ref[g * GROUP_WIDTH:(g + 1) * GROUP_WIDTH, :] = st_ref[g].T


def _ssd_prompt(z, xc, h1, wdt, dtb, alog, dsk, nw, *, batch):
    m = z.shape[0]
    nc = m // batch // SSD_CHUNK
    q = SSD_CHUNK
    row_map = lambda b, c: (b * nc + c, 0)
    return pl.pallas_call(
        functools.partial(_ssd_prompt_kernel, nc=nc),
        grid=(batch, nc),
        in_specs=[pl.BlockSpec((q, D_INNER), row_map), pl.BlockSpec((q, CONV_DIM), row_map),
                  pl.BlockSpec((q, D_MODEL), row_map),
                  _resident(wdt.shape), _resident(dtb.shape), _resident(alog.shape), _resident(dsk.shape),
                  _resident(nw.shape)],
        out_specs=[pl.BlockSpec((q, D_INNER), row_map),
                   pl.BlockSpec((None, D_INNER, D_STATE), lambda b, c: (b, 0, 0))],
        out_shape=[jax.ShapeDtypeStruct((m, D_INNER), bf16),
                   jax.ShapeDtypeStruct((batch, D_INNER, D_STATE), f32)],
        scratch_shapes=[pltpu.VMEM((N_SSD_GROUPS, D_STATE, GROUP_WIDTH), f32),
                        pltpu.VMEM((q, LANES), f32), pltpu.VMEM((q, LANES), f32), pltpu.VMEM((LANES, q), f32),
                        pltpu.VMEM((q, D_INNER), bf16), pltpu.VMEM((q, D_INNER), f32)],
        compiler_params=_cparams(2),
        name="ssd_prompt",
    )(z, xc, h1, wdt, dtb, alog, dsk, nw)


def _ssd_decode_kernel(z_ref, xbc_ref, h1_ref, cst_ref, s_ref, wdt_ref, cw_ref, cb_ref, dtb_ref, alog_ref, dsk_ref,
                       nw_ref, e_ref, y_ref, so_ref, dect_ref, xdtt_ref, yt_ref, *, bb):
    acc = cb_ref[...] + xbc_ref[...] * cw_ref[CONV_WIDTH - 1:CONV_WIDTH, :]
    for w in range(CONV_WIDTH - 1):
        acc = acc + cst_ref[w] * cw_ref[w:w + 1, :]
    xc = _silu(acc)
    xs = xc[:, :D_INNER]
    lane = lax.broadcasted_iota(jnp.int32, (bb, LANES), 1)
    dt = _softplus(_dot(h1_ref[...].astype(bf16), wdt_ref[...]) + dtb_ref[...])
    dt = jnp.where(lane < N_SSD_HEADS, dt, 0.0)
    d_a = dt * (-jnp.exp(alog_ref[...]))
    decay_e = jnp.exp(_dot_exact(d_a, e_ref[...]))
    xdt = xs * _dot_exact(dt, e_ref[...])
    pad = jnp.zeros((LANES - bb, D_INNER), f32)
    dect_ref[...] = jnp.concatenate([decay_e, pad], axis=0).T
    xdtt_ref[...] = jnp.concatenate([xdt, pad], axis=0).T
    yt_ref[...] = jnp.zeros_like(yt_ref)
    lane_s = lax.broadcasted_iota(jnp.int32, (GROUP_WIDTH, LANES), 1)
    for j in range(bb):
        for g in range(N_SSD_GROUPS):
            gs = slice(g * GROUP_WIDTH, (g + 1) * GROUP_WIDTH)
            b_row = xc[j:j + 1, D_INNER + g * D_STATE:D_INNER + (g + 1) * D_STATE]
            c_row = xc[j:j + 1, D_INNER + (N_SSD_GROUPS + g) * D_STATE:D_INNER + (N_SSD_GROUPS + g + 1) * D_STATE]
            new = s_ref[j, gs, :] * dect_ref[gs, j:j + 1] + xdtt_ref[gs, j:j + 1] * b_row
            so_ref[j, gs, :] = new
            col = jnp.sum(new * c_row, axis=-1, keepdims=True)
            yt_ref[gs, :] = jnp.where(lane_s == j, col, yt_ref[gs, :])
    y = yt_ref[...].T[:bb, :] + dsk_ref[...] * xs
    y = y * _silu(z_ref[...])
    y = y * lax.rsqrt(jnp.mean(y * y, axis=-1, keepdims=True) + RMS_EPS) * nw_ref[...]
    y_ref[...] = y


def _ssd_decode(z, xbc, h1, conv_state, ssm_state, wdt, cw, cb, dtb, alog, dsk, nw, e, *, bb):
    n = z.shape[0]
    return pl.pallas_call(
        functools.partial(_ssd_decode_kernel, bb=bb),
        grid=(n // bb,),
        in_specs=[pl.BlockSpec((bb, D_INNER), lambda i: (i, 0)), pl.BlockSpec((bb, CONV_DIM), lambda i: (i, 0)),
                  pl.BlockSpec((bb, D_MODEL), lambda i: (i, 0)),
                  pl.BlockSpec((CONV_WIDTH - 1, bb, CONV_DIM), lambda i: (0, i, 0)),
                  pl.BlockSpec((bb, D_INNER, D_STATE), lambda i: (i, 0, 0)),
                  _resident(wdt.shape), _resident(cw.shape), _resident(cb.shape), _resident(dtb.shape),
                  _resident(alog.shape), _resident(dsk.shape), _resident(nw.shape), _resident(e.shape)],
        out_specs=[pl.BlockSpec((bb, D_INNER), lambda i: (i, 0)),
                   pl.BlockSpec((bb, D_INNER, D_STATE), lambda i: (i, 0, 0))],
        out_shape=[jax.ShapeDtypeStruct((n, D_INNER), f32),
                   jax.ShapeDtypeStruct((n, D_INNER, D_STATE), f32)],
        scratch_shapes=[pltpu.VMEM((D_INNER, LANES), f32)] * 3,
        compiler_params=_cparams(1),
        name="ssd_decode",
    )(z, xbc, h1, jnp.swapaxes(conv_state, 0, 1), ssm_state, wdt, cw, cb, dtb, alog, dsk, nw, e)


def _attn_prompt_kernel(slope_ref, q_ref, k_ref, v_ref, o_ref, st_ref, kprev_ref, vprev_ref, *, dilation, rows):
    n = pl.program_id(2)
    blk = ATTN_BLOCK
    nb = rows // blk

    @pl.when(n == 0)
    def _():
        kprev_ref[...] = jnp.zeros_like(kprev_ref)
        vprev_ref[...] = jnp.zeros_like(vprev_ref)

    a = lax.broadcasted_iota(jnp.int32, (blk, blk), 0)
    c = lax.broadcasted_iota(jnp.int32, (blk, blk), 1)
    dist_prev = (dilation * (blk + a - c)).astype(f32)
    dist_cur = (dilation * (a - c)).astype(f32)
    valid_cur = c <= a
    valid_prev = c >= a
    valid_prev_first = (c - a + jnp.where(n > 0, 0, -2 * blk)) >= 0
    bias_cur, bias_prev, bias_prev_first = [], [], []
    for h in range(HEADS_PER_PATTERN):
        slope = slope_ref[h:h + 1, :]
        bias_cur.append(jnp.where(valid_cur, -slope * dist_cur, NEG_INF))
        bias_prev.append(jnp.where(valid_prev, -slope * dist_prev, NEG_INF))
        bias_prev_first.append(jnp.where(valid_prev_first, -slope * dist_prev, NEG_INF))

    qs, kcs, vcs, kps, vps, bcs, bps = [], [], [], [], [], [], []
    for i in range(nb):
        rs = slice(i * blk, (i + 1) * blk)
        ps = slice((i - 1) * blk, i * blk)
        for h in range(HEADS_PER_PATTERN):
            hs = slice(h * ATTN_HEAD_DIM, (h + 1) * ATTN_HEAD_DIM)
            qs.append(q_ref[rs, hs])
            kcs.append(k_ref[rs, hs])
            vcs.append(v_ref[rs, hs])
            kps.append(k_ref[ps, hs] if i > 0 else kprev_ref[:, hs])
            vps.append(v_ref[ps, hs] if i > 0 else vprev_ref[:, hs])
            bcs.append(bias_cur[h])
            bps.append(bias_prev[h] if i > 0 else bias_prev_first[h])
    q3 = jnp.stack(qs)
    scale = ATTN_HEAD_DIM ** -0.5
    qk = lambda x, y: jnp.einsum("bqd,bkd->bqk", x, y, preferred_element_type=f32)
    pv = lambda x, y: jnp.einsum("bqk,bkd->bqd", x, y, preferred_element_type=f32)
    s_p = qk(q3, jnp.stack(kps)) * scale + jnp.stack(bps)
    s_c = qk(q3, jnp.stack(kcs)) * scale + jnp.stack(bcs)
    m = jnp.maximum(jnp.max(s_p, axis=-1, keepdims=True), jnp.max(s_c, axis=-1, keepdims=True))
    p_p = jnp.exp(s_p - m)
    p_c = jnp.exp(s_c - m)
    l = jnp.sum(p_p, axis=-1, keepdims=True) + jnp.sum(p_c, axis=-1, keepdims=True)
    o = (pv(p_p.astype(bf16), jnp.stack(vps)) + pv(p_c.astype(bf16), jnp.stack(vcs))) / l
    lse = m + jnp.log(l)
    lane = lax.broadcasted_iota(jnp.int32, (blk, LANES), 1)
    for i in range(nb):
        rs = slice(i * blk, (i + 1) * blk)
        stat = jnp.zeros((blk, LANES), f32)
        for h in range(HEADS_PER_PATTERN):
            idx = i * HEADS_PER_PATTERN + h
            o_ref[rs, h * ATTN_HEAD_DIM:(h + 1) * ATTN_HEAD_DIM] = o[idx].astype(o_ref.dtype)
            stat = jnp.where(lane == h, lse[idx], stat)
        st_ref[rs, :] = stat
    kprev_ref[...] = k_ref[rows - blk:rows, :]
    vprev_ref[...] = v_ref[rows - blk:rows, :]


def _attn_prompt(qkv, slopes, g):
    window, dilation = ATTN_PATTERNS[g]
    assert window // dilation == ATTN_BLOCK
    batch, _, length, _ = qkv.shape
    rows = min(length, 512)
    spec = lambda width, off: pl.BlockSpec((None, None, rows, width), lambda b, r, n: (b, r, n, off))
    return pl.pallas_call(
        functools.partial(_attn_prompt_kernel, dilation=dilation, rows=rows),
        grid=(batch, dilation, length // rows),
        in_specs=[_resident(slopes.shape), spec(ATTN_WIDTH, 0), spec(ATTN_WIDTH, 1), spec(ATTN_WIDTH, 2)],
        out_specs=[spec(ATTN_WIDTH, 0), spec(LANES, 0)],
        out_shape=[jax.ShapeDtypeStruct((batch, dilation, length, ATTN_WIDTH), bf16),
                   jax.ShapeDtypeStruct((batch, dilation, length, LANES), f32)],
        scratch_shapes=[pltpu.VMEM((ATTN_BLOCK, ATTN_WIDTH), bf16), pltpu.VMEM((ATTN_BLOCK, ATTN_WIDTH), bf16)],
        compiler_params=_cparams(3),
        name=f"attn_prompt_d{dilation}",
    )(slopes, qkv, qkv, qkv)


def _attn_decode_kernel(bias_ref, qkv_ref, c0_ref, c1_ref, c2_ref, o_ref):
    scale = ATTN_HEAD_DIM ** -0.5
    outs, ms, ls = [], [], []
    for g, c_ref in enumerate((c0_ref, c1_ref, c2_ref)):
        q = qkv_ref[:, g, 0]
        k_new = qkv_ref[:, g, 1]
        v_new = qkv_ref[:, g, 2]
        s = jnp.sum(c_ref[:, :, 0] * q[:, None], axis=-1, keepdims=True) * scale + bias_ref[g]
        s_new = jnp.sum(k_new * q, axis=-1, keepdims=True) * scale
        m = jnp.maximum(jnp.max(s, axis=1), s_new)
        p = jnp.exp(s - m[:, None])
        p_new = jnp.exp(s_new - m)
        l = jnp.sum(p, axis=1) + p_new
        o = (jnp.sum(p * c_ref[:, :, 1], axis=1) + p_new * v_new) / l
        outs.append(o)
        ms.append(m)
        ls.append(l)
    m_all = jnp.maximum(jnp.maximum(ms[0], ms[1]), ms[2])
    wgts = [l * jnp.exp(m - m_all) for m, l in zip(ms, ls)]
    num = wgts[0] * outs[0] + wgts[1] * outs[1] + wgts[2] * outs[2]
    o_ref[...] = num / (wgts[0] + wgts[1] + wgts[2])


def _attn_decode(qkv, caches, bias, *, bb):
    n = qkv.shape[0]
    in_specs = [_resident(bias.shape),
                pl.BlockSpec((bb, N_PATTERNS, 3, HEADS_PER_PATTERN, ATTN_HEAD_DIM), lambda i: (i, 0, 0, 0, 0))]
    views = []
    for g, (window, dilation) in enumerate(ATTN_PATTERNS):
        assert caches[g].shape[1] == window and window // dilation == ATTN_BLOCK
        views.append(caches[g].reshape(n, ATTN_BLOCK, dilation, 2, HEADS_PER_PATTERN, ATTN_HEAD_DIM))
        in_specs.append(pl.BlockSpec((bb, ATTN_BLOCK, None, 2, HEADS_PER_PATTERN, ATTN_HEAD_DIM),
                                     lambda i: (i, 0, 0, 0, 0, 0)))
    return pl.pallas_call(
        _attn_decode_kernel,
        grid=(n // bb,),
        in_specs=in_specs,
        out_specs=pl.BlockSpec((bb, HEADS_PER_PATTERN, ATTN_HEAD_DIM), lambda i: (i, 0, 0)),
        out_shape=jax.ShapeDtypeStruct((n, HEADS_PER_PATTERN, ATTN_HEAD_DIM), f32),
        compiler_params=_cparams(1),
        name="attn_decode",
    )(bias, qkv, *views)


def _post_mix_kernel(*refs, dilations):
    n_pat = len(dilations)
    merged = n_pat > 1
    h1_ref, y_ref = refs[0], refs[1]
    o_refs = refs[2:2 + n_pat]
    st_refs = refs[2 + n_pat:2 + 2 * n_pat] if merged else ()
    k = 2 + n_pat + len(st_refs)
    wg_ref, wso_ref, wao_ref, wmo_ref, lnw_ref, lnb_ref, out_ref = refs[k:k + 7]
    scratch = refs[k + 7:]
    tm = h1_ref.shape[0]
    h1 = h1_ref[...]
    gates = jax.nn.sigmoid(_dot(h1.astype(bf16), wg_ref[...]))
    if merged:
        o_heads, lses = [], []
        for g, d in enumerate(dilations):
            if d == 1:
                o_heads.append([o_refs[g][:, h * ATTN_HEAD_DIM:(h + 1) * ATTN_HEAD_DIM].astype(f32)
                                for h in range(HEADS_PER_PATTERN)])
                lses.append(st_refs[g])
                continue
            o_scr, st_scr = scratch[0], scratch[1]
            scratch = scratch[2:]
            for r in range(d):
                st_scr[pl.ds(r, tm // d, stride=d), :] = st_refs[g][r]
                for h in range(HEADS_PER_PATTERN):
                    o_scr[h, pl.ds(r, tm // d, stride=d), :] = (
                        o_refs[g][r, :, h * ATTN_HEAD_DIM:(h + 1) * ATTN_HEAD_DIM].astype(f32))
            o_heads.append([o_scr[h] for h in range(HEADS_PER_PATTERN)])
            lses.append(st_scr)
        heads = []
        for h in range(HEADS_PER_PATTERN):
            lse = [st[:, h:h + 1] for st in lses]
            top = functools.reduce(jnp.maximum, lse)
            wgt = [jnp.exp(v - top) for v in lse]
            num = sum(w * o[h] for w, o in zip(wgt, o_heads))
            heads.append(num / sum(wgt))
        attn = jnp.concatenate(heads, axis=1)
    else:
        attn = o_refs[0][...]
    t_ssd = _dot(y_ref[...].astype(bf16), wso_ref[...])
    t_attn = _dot(attn.astype(bf16), wao_ref[...])
    u = gates[:, :D_MODEL] * t_ssd + gates[:, D_MODEL:] * t_attn
    mix = _dot(u.astype(bf16), wmo_ref[...])
    out_ref[...] = _layer_norm(DEEPNORM_ALPHA * h1 + mix, lnw_ref[...], lnb_ref[...])


def _post_mix(h1, y_ssd, outs, stats, dilations, wg, wso, wao, wmo, lnw, lnb, *, tm, seq):
    m = h1.shape[0]
    steps = seq // tm
    row = lambda width: pl.BlockSpec((tm, width), lambda i: (i, 0))

    def pattern_spec(width, d):
        if d == 1:
            return row(width)
        return pl.BlockSpec((None, d, tm // d, width), lambda i: (i // steps, 0, i % steps, 0))

    in_specs = [row(D_MODEL), row(D_INNER)] + [pattern_spec(ATTN_WIDTH, d) for d in dilations]
    in_specs += [pattern_spec(LANES, d) for d in dilations[:len(stats)]]
    in_specs += [_resident(w.shape) for w in (wg, wso, wao, wmo, lnw, lnb)]
    scratch = []
    for d in dilations:
        if d > 1:
            scratch += [pltpu.VMEM((HEADS_PER_PATTERN, tm, ATTN_HEAD_DIM), f32), pltpu.VMEM((tm, LANES), f32)]
    return pl.pallas_call(
        functools.partial(_post_mix_kernel, dilations=tuple(dilations)),
        grid=(m // tm,),
        in_specs=in_specs,
        out_specs=row(D_MODEL),
        out_shape=jax.ShapeDtypeStruct((m, D_MODEL), f32),
        scratch_shapes=scratch,
        compiler_params=_cparams(1),
        name="post_mix",
    )(h1, y_ssd, *outs, *stats, wg, wso, wao, wmo, lnw, lnb)


SHIFT_CHUNK = 256
SHIFT_BLOCK_BYTES = 8 * 1024 * 1024


def _kv_shift_kernel(c_ref, new_ref, o_ref, *, key_rows):
    bb, rows, _ = c_ref.shape
    keep = rows - key_rows
    n_chunks = pl.cdiv(keep, SHIFT_CHUNK)
    for b in range(bb):
        def body(k, carry):
            r0 = pl.multiple_of(jnp.minimum(k * SHIFT_CHUNK, keep - SHIFT_CHUNK), 8)
            o_ref[b, pl.ds(r0, SHIFT_CHUNK), :] = c_ref[b, pl.ds(r0 + key_rows, SHIFT_CHUNK), :]
            return carry
        lax.fori_loop(0, n_chunks, body, 0)
        o_ref[b, keep:rows, :] = new_ref[b]


def _kv_shift(cache, new):
    n, w = cache.shape[:2]
    key_rows = 2 * HEADS_PER_PATTERN * ATTN_HEAD_DIM // LANES
    rows = w * key_rows
    assert key_rows % 8 == 0 and rows - key_rows >= SHIFT_CHUNK
    bb = max(1, min(n, SHIFT_BLOCK_BYTES // (rows * LANES * 4)))
    assert n % bb == 0
    out = pl.pallas_call(
        functools.partial(_kv_shift_kernel, key_rows=key_rows),
        grid=(n // bb,),
        in_specs=[pl.BlockSpec((bb, rows, LANES), lambda i: (i, 0, 0)),
                  pl.BlockSpec((bb, key_rows, LANES), lambda i: (i, 0, 0))],
        out_specs=pl.BlockSpec((bb, rows, LANES), lambda i: (i, 0, 0)),
        out_shape=jax.ShapeDtypeStruct((n, rows, LANES), cache.dtype),
        compiler_params=_cparams(1),
        name=f"kv_shift_w{w}",
    )(cache.reshape(n, rows, LANES), new.reshape(n, key_rows, LANES))
    return out.reshape(cache.shape)


def _alibi_slopes():
    n = N_PATTERNS * HEADS_PER_PATTERN
    e = jnp.arange(1, n + 1, dtype=f32)
    return (2.0 ** (-ALIBI_MAX_EXP * e / n)).reshape(N_PATTERNS, HEADS_PER_PATTERN)


def _prepare_weights(ln_w, ln_b, ffn_w13, ffn_w2, w_in, conv_w, conv_b, dt_bias, a_log, d_skip, ssd_norm_w,
                     w_ssd_out, w_attn_out, w_mix_out, w_pe_gate, w_pe_proj):
    cuts = [0]
    for s in IN_SPLITS:
        cuts.append(cuts[-1] + s)
    wz, wxbc, wdt, wq, wk, wv = (w_in[:, cuts[i]:cuts[i + 1]] for i in range(6))
    pad_heads = LANES - N_SSD_HEADS
    head_of_col = jnp.arange(D_INNER, dtype=jnp.int32) // SSD_HEAD_DIM
    return dict(
        ln_w=[ln_w[i:i + 1] for i in range(3)], ln_b=[ln_b[i:i + 1] for i in range(3)],
        w13=[ffn_w13[i].astype(bf16) for i in range(2)], w2=[ffn_w2[i].astype(bf16) for i in range(2)],
        wz=wz.astype(bf16), wxbc=wxbc.astype(bf16),
        wdt=jnp.pad(wdt, ((0, 0), (0, pad_heads))).astype(bf16),
        wqkv=jnp.concatenate([m[:, g * ATTN_WIDTH:(g + 1) * ATTN_WIDTH] for g in range(N_PATTERNS)
                              for m in (wq, wk, wv)], axis=1).astype(bf16),
        wg=w_in[:, cuts[6]:].astype(bf16),
        cw=conv_w, cb=conv_b[None, :],
        dtb=jnp.pad(dt_bias, (0, pad_heads))[None, :], alog=jnp.pad(a_log, (0, pad_heads))[None, :],
        dsk=jnp.repeat(d_skip, SSD_HEAD_DIM)[None, :], nw=ssd_norm_w[None, :],
        expand=(jnp.arange(LANES, dtype=jnp.int32)[:, None] == head_of_col[None, :]).astype(f32),
        wso=w_ssd_out.astype(bf16), wao=w_attn_out.astype(bf16), wmo=w_mix_out.astype(bf16),
        wpg=w_pe_gate.astype(bf16), wpp=w_pe_proj.astype(bf16),
    )


def _ssd_args(w):
    return (w["wdt"], w["cw"], w["cb"], w["dtb"], w["alog"], w["dsk"], w["nw"], w["expand"])


def _layer_tail(h1, y_ssd, outs, stats, dilations, p, w, *, tm, seq):
    h2 = _post_mix(h1, y_ssd, outs, stats, dilations, w["wg"], w["wso"], w["wao"], w["wmo"], w["ln_w"][1],
                   w["ln_b"][1], tm=tm, seq=seq)
    return _ffn_ln(h2, w["w13"][1], w["w2"][1], w["ln_w"][2], w["ln_b"][2], pe=(p, w["wpg"], w["wpp"]), tm=tm)


def _prompt_layer(x, p, w):
    batch, seq, _ = x.shape
    assert seq % SSD_CHUNK == 0 and all(seq % (ATTN_BLOCK * d) == 0 and seq >= win for win, d in ATTN_PATTERNS)
    tm = 512
    x2 = x.reshape(batch * seq, D_MODEL)
    h1 = _ffn_ln(x2, w["w13"][0], w["w2"][0], w["ln_w"][0], w["ln_b"][0], tm=tm)
    z, xc, xbc_tail = _proj_zx(h1, w["wz"], w["wxbc"], w["cw"], w["cb"], batch=batch, seq=seq, tm=tm)
    qkv = _qkv_proj(h1, w["wqkv"], batch=batch, seq=seq, tm=tm)
    y_ssd, new_ssm = _ssd_prompt(z, xc, h1, w["wdt"], w["dtb"], w["alog"], w["dsk"], w["nw"], batch=batch)
    slopes = _alibi_slopes()
    dilations = [d for _, d in ATTN_PATTERNS]
    outs, stats, new_kv = [], [], []
    for g, (win, d) in enumerate(ATTN_PATTERNS):
        o, st = _attn_prompt(qkv[g], jnp.broadcast_to(slopes[g][:, None], (HEADS_PER_PATTERN, LANES)), g)
        if d == 1:
            o, st = o.reshape(batch * seq, ATTN_WIDTH), st.reshape(batch * seq, LANES)
        outs.append(o)
        stats.append(st)
        tail = qkv[g][:, :, seq // d - win // d:, ATTN_WIDTH:]
        new_kv.append(jnp.swapaxes(tail, 1, 2).reshape(batch, win, 2, HEADS_PER_PATTERN, ATTN_HEAD_DIM).astype(f32))
    y = _layer_tail(h1, y_ssd, outs, stats, dilations, p.reshape(batch * seq, PLE_DIM), w, tm=tm, seq=seq)
    new_conv = xbc_tail[:, 8 - (CONV_WIDTH - 1):]
    return (y.reshape(batch, seq, D_MODEL), new_ssm.reshape(batch, N_SSD_HEADS, SSD_HEAD_DIM, D_STATE),
            new_conv, new_kv)


def _decode_layer(x, p, ssm_state, conv_state, caches, w):
    n, t, _ = x.shape
    assert t == 1
    tm = n
    bb = 8
    h1 = _ffn_ln(x.reshape(n, D_MODEL), w["w13"][0], w["w2"][0], w["ln_w"][0], w["ln_b"][0], tm=tm)
    z = _matmul(h1, w["wz"], f32, tm=tm)
    xbc = _matmul(h1, w["wxbc"], f32, tm=tm)
    qkv = _matmul(h1, w["wqkv"], f32, tm=tm)
    y_ssd, new_ssm = _ssd_decode(z, xbc, h1, conv_state, ssm_state.reshape(n, D_INNER, D_STATE), *_ssd_args(w), bb=bb)
    slopes = _alibi_slopes()
    steps_back = (ATTN_BLOCK - jnp.arange(ATTN_BLOCK, dtype=jnp.int32))
    bias = jnp.stack([-slopes[g][None, :] * (d * steps_back).astype(f32)[:, None]
                      for g, (_, d) in enumerate(ATTN_PATTERNS)])
    bias = jnp.broadcast_to(bias[..., None], bias.shape + (1,))
    qkv5 = qkv.reshape(n, N_PATTERNS, 3, HEADS_PER_PATTERN, ATTN_HEAD_DIM)
    attn = _attn_decode(qkv5, caches, bias, bb=bb).reshape(n, ATTN_WIDTH)
    y = _layer_tail(h1, y_ssd, [attn], [], [1], p.reshape(n, PLE_DIM), w, tm=tm, seq=n)
    new_conv = jnp.concatenate([conv_state[:, 1:], xbc[:, None, :]], axis=1)
    new_kv = [_kv_shift(caches[g], qkv5[:, g, 1:3]) for g in range(N_PATTERNS)]
    return y.reshape(n, 1, D_MODEL), new_ssm.reshape(n, N_SSD_HEADS, SSD_HEAD_DIM, D_STATE), new_conv, new_kv


def kernel(x_prompt, x_sample, state_ssm, state_conv, cache_kv_w128, cache_kv_w512, cache_kv_w2048, p_prompt, p_sample, ln_w, ln_b, ffn_w13, ffn_w2, w_in, conv_w, conv_b, dt_bias, a_log, d_skip, ssd_norm_w, w_ssd_out, w_attn_out, w_mix_out, w_pe_gate, w_pe_proj):
    assert ln_w.shape[0] == DEPTH
    w = _prepare_weights(ln_w[0], ln_b[0], ffn_w13[0], ffn_w2[0], w_in[0], conv_w[0], conv_b[0], dt_bias[0], a_log[0],
                         d_skip[0], ssd_norm_w[0], w_ssd_out[0], w_attn_out[0], w_mix_out[0], w_pe_gate[0],
                         w_pe_proj[0])
    y_p, ssm_p, conv_p, kv_p = _prompt_layer(x_prompt, p_prompt[0], w)
    caches = (cache_kv_w128[0], cache_kv_w512[0], cache_kv_w2048[0])
    y_s, ssm_s, conv_s, kv_s = _decode_layer(x_sample, p_sample[0], state_ssm[0], state_conv[0], caches, w)
    lead = lambda a: a[None]
    return (y_p, y_s, lead(ssm_p), lead(conv_p), lead(kv_p[0]), lead(kv_p[1]), lead(kv_p[2]),
            lead(ssm_s), lead(conv_s), lead(kv_s[0]), lead(kv_s[1]), lead(kv_s[2]))
```

```python
import functools

import jax
import jax.numpy as jnp
from jax import lax
from jax.experimental import pallas as pl
from jax.experimental.pallas import tpu as pltpu

f32 = jnp.float32
bf16 = jnp.bfloat16

D_MODEL = 1024
D_INNER = 2 * D_MODEL
SSD_HEAD_DIM = 64
N_SSD_HEADS = D_INNER // SSD_HEAD_DIM
N_SSD_GROUPS = 4
HEADS_PER_GROUP = N_SSD_HEADS // N_SSD_GROUPS
GROUP_WIDTH = HEADS_PER_GROUP * SSD_HEAD_DIM
D_STATE = 128
CONV_WIDTH = 4
CONV_DIM = D_INNER + 2 * N_SSD_GROUPS * D_STATE
SSD_CHUNK = 128
ATTN_PATTERNS = ((128, 1), (512, 4), (2048, 16))
N_PATTERNS = 3
HEADS_PER_PATTERN = 4
ATTN_HEAD_DIM = 128
ATTN_WIDTH = HEADS_PER_PATTERN * ATTN_HEAD_DIM
ATTN_QKV_WIDTH = N_PATTERNS * ATTN_WIDTH
PATTERN_QKV_WIDTH = 3 * ATTN_WIDTH
ATTN_BLOCK = 128
ALIBI_MAX_EXP = 8.0
D_FF = 2816
PLE_DIM = 256
LN_EPS = 1e-5
RMS_EPS = 1e-5
DEPTH = 1
DEEPNORM_ALPHA = (2.0 * DEPTH) ** 0.25
IN_SPLITS = (D_INNER, CONV_DIM, N_SSD_HEADS, ATTN_QKV_WIDTH, ATTN_QKV_WIDTH, ATTN_QKV_WIDTH)

LANES = 128
VMEM_LIMIT = 56 * 1024 * 1024
NEG_INF = float("-inf")


def _cparams(n_axes):
    return pltpu.CompilerParams(dimension_semantics=("arbitrary",) * n_axes, vmem_limit_bytes=VMEM_LIMIT)


def _resident(shape):
    nd = len(shape)
    return pl.BlockSpec(shape, lambda *_: (0,) * nd, pipeline_mode=pl.Buffered(1))


def _layer_norm(y, w, b):
    mu = jnp.mean(y, axis=-1, keepdims=True)
    yc = y - mu
    var = jnp.mean(yc * yc, axis=-1, keepdims=True)
    return yc * lax.rsqrt(var + LN_EPS) * w + b


def _silu(x):
    return x * jax.nn.sigmoid(x)


def _softplus(x):
    return jnp.maximum(x, 0.0) + jnp.log1p(jnp.exp(-jnp.abs(x)))


def _dot(a, b):
    return jnp.dot(a, b, preferred_element_type=f32)


def _dot_nt(a, b):
    return lax.dot_general(a, b, (((1,), (1,)), ((), ())), preferred_element_type=f32)


def _dot_tn(a, b):
    return lax.dot_general(a, b, (((0,), (0,)), ((), ())), preferred_element_type=f32)


def _dot_exact(a, b):
    return jnp.dot(a, b, precision=lax.Precision.HIGHEST, preferred_element_type=f32)


def _carry_slot(chunk, base, cache_ref, new_ref, out_ref, buf_ref, sems):
    return lax.rem(chunk, 2), base + chunk


def _carry_in(chunk, *refs):
    _, cache_ref, new_ref, _, buf_ref, sems = refs
    rows, key_rows = cache_ref.shape[1], new_ref.shape[1]
    slot, b = _carry_slot(chunk, *refs)
    return (pltpu.make_async_copy(cache_ref.at[b, pl.ds(key_rows, rows - key_rows)],
                                  buf_ref.at[slot, pl.ds(0, rows - key_rows)], sems.at[0, slot]),
            pltpu.make_async_copy(new_ref.at[b], buf_ref.at[slot, pl.ds(rows - key_rows, key_rows)], sems.at[1, slot]))


def _carry_out(chunk, *refs):
    _, _, _, out_ref, buf_ref, sems = refs
    slot, b = _carry_slot(chunk, *refs)
    return pltpu.make_async_copy(buf_ref.at[slot], out_ref.at[b], sems.at[2, slot])


def _carry_begin(step, n_chunks, *refs):
    @pl.when(step == 0)
    def _():
        for cp in _carry_in(step, *refs):
            cp.start()

    @pl.when(step < n_chunks)
    def _():
        for cp in _carry_in(step, *refs):
            cp.wait()
        _carry_out(step, *refs).start()

        @pl.when(step > 0)
        def _():
            _carry_out(step - 1, *refs).wait()

        @pl.when(step + 1 < n_chunks)
        def _():
            for cp in _carry_in(step + 1, *refs):
                cp.start()


def _carry_end(step, n_chunks, *refs):
    @pl.when(step == n_chunks - 1)
    def _():
        _carry_out(step, *refs).wait()


def _call(body, *, grid, in_specs, out_specs, out_shape, scratch_shapes=(), name, args, carry=None):
    out_specs, out_shape = list(out_specs), list(out_shape)
    if carry is None:
        res = pl.pallas_call(body, grid=grid, in_specs=list(in_specs), out_specs=out_specs, out_shape=out_shape,
                             scratch_shapes=list(scratch_shapes), compiler_params=_cparams(len(grid)), name=name)(*args)
        return list(res), None
    cache, new, prev, base, count = carry
    assert len(grid) == 1 and 0 < count <= grid[0]
    n_in, n_out, n_scr = len(in_specs), len(out_specs), len(scratch_shapes)
    n_cin = 2 if prev is None else 3

    def kernel(*refs):
        ins = refs[:n_in]
        cache_ref, new_ref = refs[n_in:n_in + 2]
        outs = refs[n_in + n_cin:n_in + n_cin + n_out]
        out_cache_ref = refs[n_in + n_cin + n_out]
        scratch = refs[n_in + n_cin + n_out + 1:n_in + n_cin + n_out + 1 + n_scr]
        buf_ref, sems = refs[-2:]
        step = pl.program_id(0)
        carry_refs = (base, cache_ref, new_ref, out_cache_ref, buf_ref, sems)
        _carry_begin(step, count, *carry_refs)
        body(*ins, *outs, *scratch)
        _carry_end(step, count, *carry_refs)

    any_spec = pl.BlockSpec(memory_space=pl.ANY)
    cargs = (cache, new) if prev is None else (cache, new, prev)
    res = pl.pallas_call(
        kernel, grid=grid,
        in_specs=list(in_specs) + [any_spec] * n_cin,
        out_specs=out_specs + [any_spec],
        out_shape=out_shape + [jax.ShapeDtypeStruct(cache.shape, cache.dtype)],
        scratch_shapes=list(scratch_shapes) + [pltpu.VMEM((2,) + cache.shape[1:], cache.dtype),
                                               pltpu.SemaphoreType.DMA((3, 2))],
        input_output_aliases={} if prev is None else {n_in + 2: n_out},
        compiler_params=_cparams(1), name=name + "_carry")(*args, *cargs)
    return list(res[:n_out]), res[n_out]


FFN_CHUNKS = (512, 512, 512, 512, 512, 256)


def _ffn_ln_kernel(*refs, with_pe):
    if with_pe:
        x_ref, w13_ref, w2_ref, lnw_ref, lnb_ref, p_ref, wpg_ref, wpp_ref, o_ref = refs
    else:
        x_ref, w13_ref, w2_ref, lnw_ref, lnb_ref, o_ref = refs
    x = x_ref[...]
    xb = x.astype(bf16)
    acc = None
    c0 = 0
    for width in FFN_CHUNKS:
        gate = _dot(xb, w13_ref[:, c0:c0 + width])
        up = _dot(xb, w13_ref[:, D_FF + c0:D_FF + c0 + width])
        part = _dot((_silu(gate) * up).astype(bf16), w2_ref[c0:c0 + width, :])
        acc = part if acc is None else acc + part
        c0 += width
    h = _layer_norm(DEEPNORM_ALPHA * x + 0.5 * acc, lnw_ref[...], lnb_ref[...])
    if with_pe:
        pe_gate = jax.nn.sigmoid(_dot(h.astype(bf16), wpg_ref[...]))
        h = h + pe_gate * _dot(p_ref[...].astype(bf16), wpp_ref[...])
    o_ref[...] = h


def _ffn_ln(x, w13, w2, lnw, lnb, pe=None, *, tm, carry=None):
    m = x.shape[0]
    assert sum(FFN_CHUNKS) == D_FF
    row = lambda width: pl.BlockSpec((tm, width), lambda i: (i, 0))
    in_specs = [row(D_MODEL), _resident(w13.shape), _resident(w2.shape), _resident(lnw.shape), _resident(lnb.shape)]
    args = [x, w13, w2, lnw, lnb]
    if pe is not None:
        p, wpg, wpp = pe
        in_specs += [row(PLE_DIM), _resident(wpg.shape), _resident(wpp.shape)]
        args += [p, wpg, wpp]
    return _call(functools.partial(_ffn_ln_kernel, with_pe=pe is not None), grid=(m // tm,), in_specs=in_specs,
                 out_specs=[row(D_MODEL)], out_shape=[jax.ShapeDtypeStruct((m, D_MODEL), f32)],
                 name="ffn_ln_pe" if pe is not None else "ffn_ln", args=args, carry=carry)


def _matmul_kernel(x_ref, w_ref, o_ref):
    o_ref[...] = _dot(x_ref[...].astype(bf16), w_ref[...]).astype(o_ref.dtype)


def _matmul(x, w, out_dtype, *, tm):
    m, k = x.shape
    n = w.shape[1]
    return pl.pallas_call(
        _matmul_kernel,
        grid=(m // tm,),
        in_specs=[pl.BlockSpec((tm, k), lambda i: (i, 0)), _resident((k, n))],
        out_specs=pl.BlockSpec((tm, n), lambda i: (i, 0)),
        out_shape=jax.ShapeDtypeStruct((m, n), out_dtype),
        compiler_params=_cparams(1),
        name="proj_matmul",
    )(x, w)


def _qkv_proj_kernel(x_ref, w_ref, *rest):
    outs, res_ref = rest[:N_PATTERNS], rest[N_PATTERNS]
    tm = x_ref.shape[0]
    xb = x_ref[...].astype(bf16)
    wide = 2 * LANES
    for c2 in range(N_PATTERNS * PATTERN_QKV_WIDTH // wide):
        res = _dot(xb, w_ref[:, c2 * wide:(c2 + 1) * wide])
        res_ref[2 * c2] = res[:, :LANES]
        res_ref[2 * c2 + 1] = res[:, LANES:]
    blocks = PATTERN_QKV_WIDTH // LANES
    for g, (_, d) in enumerate(ATTN_PATTERNS):
        for cb in range(blocks):
            cs = slice(cb * LANES, (cb + 1) * LANES)
            for r in range(d):
                outs[g][r, :, cs] = res_ref[g * blocks + cb, pl.ds(r, tm // d, stride=d), :].astype(bf16)


def _qkv_proj(x, w, *, batch, seq, tm, carry=None):
    m, k = x.shape
    steps = seq // tm
    return _call(
        _qkv_proj_kernel,
        grid=(m // tm,),
        in_specs=[pl.BlockSpec((tm, k), lambda i: (i, 0)), _resident(w.shape)],
        out_specs=[pl.BlockSpec((None, d, tm // d, PATTERN_QKV_WIDTH), lambda i: (i // steps, 0, i % steps, 0))
                   for _, d in ATTN_PATTERNS],
        out_shape=[jax.ShapeDtypeStruct((batch, d, seq // d, PATTERN_QKV_WIDTH), bf16) for _, d in ATTN_PATTERNS],
        scratch_shapes=[pltpu.VMEM((N_PATTERNS * PATTERN_QKV_WIDTH // LANES, tm, LANES), f32)],
        name="qkv_proj", args=(x, w), carry=carry)


CONV_ROWS = 64


def _proj_zx_kernel(x_ref, wz_ref, wx_ref, cw_ref, cb_ref, z_ref, xc_ref, tail_ref, xpad_ref, *, steps):
    i = pl.program_id(0)
    tm = x_ref.shape[0]
    halo = 8
    xb = x_ref[...].astype(bf16)
    z_ref[...] = _dot(xb, wz_ref[...]).astype(z_ref.dtype)

    @pl.when(i % steps == 0)
    def _():
        xpad_ref[0:halo, :] = jnp.zeros((halo, CONV_DIM), f32)

    for j in range(CONV_DIM // 512):
        xpad_ref[halo:halo + tm, j * 512:(j + 1) * 512] = _dot(xb, wx_ref[:, j * 512:(j + 1) * 512])
        for cb in range(2 * j, 2 * j + 2):
            cs = slice(cb * 256, (cb + 1) * 256)
            taps = [cw_ref[w:w + 1, cs] for w in range(CONV_WIDTH)]
            bias = cb_ref[:, cs]
            for r0 in range(0, tm, CONV_ROWS):
                blk = xpad_ref[r0:r0 + halo + CONV_ROWS, cs]
                acc = bias + blk[halo:] * taps[CONV_WIDTH - 1]
                for k in range(1, CONV_WIDTH):
                    acc = acc + pltpu.roll(blk, k, 0)[halo:] * taps[CONV_WIDTH - 1 - k]
                xc_ref[r0:r0 + CONV_ROWS, cs] = _silu(acc).astype(xc_ref.dtype)
    tail_ref[...] = xpad_ref[tm:tm + halo, :]
    xpad_ref[0:halo, :] = xpad_ref[tm:tm + halo, :]


def _proj_zx(x, wz, wx, cw, cb, *, batch, seq, tm):
    m, k = x.shape
    steps = seq // tm
    return pl.pallas_call(
        functools.partial(_proj_zx_kernel, steps=steps),
        grid=(m // tm,),
        in_specs=[pl.BlockSpec((tm, k), lambda i: (i, 0)), _resident(wz.shape), _resident(wx.shape),
                  _resident(cw.shape), _resident(cb.shape)],
        out_specs=[pl.BlockSpec((tm, D_INNER), lambda i: (i, 0)), pl.BlockSpec((tm, CONV_DIM), lambda i: (i, 0)),
                   pl.BlockSpec((None, 8, CONV_DIM), lambda i: (i // steps, 0, 0))],
        out_shape=[jax.ShapeDtypeStruct((m, D_INNER), bf16), jax.ShapeDtypeStruct((m, CONV_DIM), bf16),
                   jax.ShapeDtypeStruct((batch, 8, CONV_DIM), f32)],
        scratch_shapes=[pltpu.VMEM((tm + 8, CONV_DIM), f32)],
        compiler_params=_cparams(1),
        name="proj_zx",
    )(x, wz, wx, cw, cb)


def _split3(x):
    hi = x.astype(bf16)
    r = x - hi.astype(f32)
    mid = r.astype(bf16)
    return hi, mid, (r - mid.astype(f32)).astype(bf16)


def _ssd_prompt_kernel(z_ref, xc_ref, h1_ref, wdt_ref, dtb_ref, alog_ref, dsk_ref, nw_ref, y_ref, so_ref,
                       st_ref, dt_ref, cum_ref, cumt_ref, xw_ref, yacc_ref, *, nc):
    c = pl.program_id(1)
    q = SSD_CHUNK

    @pl.when(c == 0)
    def _():
        st_ref[...] = jnp.zeros_like(st_ref)

    lane = lax.broadcasted_iota(jnp.int32, (q, LANES), 1)
    row = lax.broadcasted_iota(jnp.int32, (q, LANES), 0)
    dt = _softplus(_dot(h1_ref[...].astype(bf16), wdt_ref[...]) + dtb_ref[...])
    dt = jnp.where(lane < N_SSD_HEADS, dt, 0.0)
    d_a = dt * (-jnp.exp(alog_ref[...]))
    causal = row >= lane
    c3 = _dot(causal.astype(bf16), jnp.concatenate(_split3(d_a), axis=1))
    cum = c3[:, :LANES] + c3[:, LANES:2 * LANES] + c3[:, 2 * LANES:]
    dt_ref[...] = dt
    cum_ref[...] = cum
    cumt_ref[...] = cum.T

    lo_half = lane < SSD_HEAD_DIM
    lo_half_row = lo_half[0:1, :]
    for g in range(N_SSD_GROUPS):
        b_g = xc_ref[:, D_INNER + g * D_STATE:D_INNER + (g + 1) * D_STATE]
        c_g = xc_ref[:, D_INNER + (N_SSD_GROUPS + g) * D_STATE:D_INNER + (N_SSD_GROUPS + g + 1) * D_STATE]
        cb = _dot_nt(c_g, b_g)
        y_off = _dot(c_g, st_ref[g].astype(bf16))
        chunk_decay = []
        for jj in range(HEADS_PER_GROUP // 2):
            j = g * (HEADS_PER_GROUP // 2) + jj
            cs = slice(j * LANES, (j + 1) * LANES)
            ha, hb = 2 * j, 2 * j + 1
            x_p = xc_ref[:, cs].astype(f32)
            cum_a, cum_b = cum_ref[:, ha:ha + 1], cum_ref[:, hb:hb + 1]
            cume = jnp.where(lo_half, cum_a, cum_b)
            cume_last = jnp.where(lo_half_row, cum_a[q - 1:q, :], cum_b[q - 1:q, :])
            xdt = x_p * jnp.where(lo_half, dt_ref[:, ha:ha + 1], dt_ref[:, hb:hb + 1])
            ls = []
            for h, cum_h in ((ha, cum_a), (hb, cum_b)):
                seg = cum_h - cumt_ref[h:h + 1, :]
                ls.append(jnp.exp(jnp.where(causal, seg, NEG_INF)) * cb)
            l2 = jnp.concatenate(ls, axis=1).astype(bf16)
            x2 = jnp.concatenate([jnp.where(lo_half, xdt, 0.0), jnp.where(lo_half, 0.0, xdt)], axis=0).astype(bf16)
            y = _dot(l2, x2) + y_off[:, jj * LANES:(jj + 1) * LANES] * jnp.exp(cume) + dsk_ref[:, cs] * x_p
            yacc_ref[:, cs] = y * _silu(z_ref[:, cs].astype(f32))
            xw_ref[:, cs] = (xdt * jnp.exp(cume_last - cume)).astype(bf16)
            chunk_decay.append(jnp.exp(cume_last))
        gs = slice(g * GROUP_WIDTH, (g + 1) * GROUP_WIDTH)
        st_ref[g] = st_ref[g] * jnp.concatenate(chunk_decay, axis=1) + _dot_tn(b_g, xw_ref[:, gs])

    y = yacc_ref[...]
    y = y * lax.rsqrt(jnp.mean(y * y, axis=-1, keepdims=True) + RMS_EPS) * nw_ref[...]
    y_ref[...] = y.astype(y_ref.dtype)

    @pl.when(c == nc - 1)
    def _():
        for g in range(N_SSD_GROUPS):
            so_ref[g * GROUP_WIDTH:(g + 1) * GROUP_WIDTH, :] = st_ref[g].T


def _ssd_prompt(z, xc, h1, wdt, dtb, alog, dsk, nw, *, batch):
    m = z.shape[0]
    nc = m // batch // SSD_CHUNK
    q = SSD_CHUNK
    row_map = lambda b, c: (b * nc + c, 0)
    return pl.pallas_call(
        functools.partial(_ssd_prompt_kernel, nc=nc),
        grid=(batch, nc),
        in_specs=[pl.BlockSpec((q, D_INNER), row_map), pl.BlockSpec((q, CONV_DIM), row_map),
                  pl.BlockSpec((q, D_MODEL), row_map),
                  _resident(wdt.shape), _resident(dtb.shape), _resident(alog.shape), _resident(dsk.shape),
                  _resident(nw.shape)],
        out_specs=[pl.BlockSpec((q, D_INNER), row_map),
                   pl.BlockSpec((None, D_INNER, D_STATE), lambda b, c: (b, 0, 0))],
        out_shape=[jax.ShapeDtypeStruct((m, D_INNER), bf16),
                   jax.ShapeDtypeStruct((batch, D_INNER, D_STATE), f32)],
        scratch_shapes=[pltpu.VMEM((N_SSD_GROUPS, D_STATE, GROUP_WIDTH), f32),
                        pltpu.VMEM((q, LANES), f32), pltpu.VMEM((q, LANES), f32), pltpu.VMEM((LANES, q), f32),
                        pltpu.VMEM((q, D_INNER), bf16), pltpu.VMEM((q, D_INNER), f32)],
        compiler_params=_cparams(2),
        name="ssd_prompt",
    )(z, xc, h1, wdt, dtb, alog, dsk, nw)


def _ssd_decode_kernel(z_ref, xbc_ref, h1_ref, cst_ref, s_ref, wdt_ref, cw_ref, cb_ref, dtb_ref, alog_ref, dsk_ref,
                       nw_ref, e_ref, y_ref, so_ref, dect_ref, xdtt_ref, yt_ref, *, bb):
    acc = cb_ref[...] + xbc_ref[...] * cw_ref[CONV_WIDTH - 1:CONV_WIDTH, :]
    for w in range(CONV_WIDTH - 1):
        acc = acc + cst_ref[w] * cw_ref[w:w + 1, :]
    xc = _silu(acc)
    xs = xc[:, :D_INNER]
    lane = lax.broadcasted_iota(jnp.int32, (bb, LANES), 1)
    dt = _softplus(_dot(h1_ref[...].astype(bf16), wdt_ref[...]) + dtb_ref[...])
    dt = jnp.where(lane < N_SSD_HEADS, dt, 0.0)
    d_a = dt * (-jnp.exp(alog_ref[...]))
    decay_e = jnp.exp(_dot_exact(d_a, e_ref[...]))
    xdt = xs * _dot_exact(dt, e_ref[...])
    pad = jnp.zeros((LANES - bb, D_INNER), f32)
    dect_ref[...] = jnp.concatenate([decay_e, pad], axis=0).T
    xdtt_ref[...] = jnp.concatenate([xdt, pad], axis=0).T
    yt_ref[...] = jnp.zeros_like(yt_ref)
    lane_s = lax.broadcasted_iota(jnp.int32, (GROUP_WIDTH, LANES), 1)
    for j in range(bb):
        for g in range(N_SSD_GROUPS):
            gs = slice(g * GROUP_WIDTH, (g + 1) * GROUP_WIDTH)
            b_row = xc[j:j + 1, D_INNER + g * D_STATE:D_INNER + (g + 1) * D_STATE]
            c_row = xc[j:j + 1, D_INNER + (N_SSD_GROUPS + g) * D_STATE:D_INNER + (N_SSD_GROUPS + g + 1) * D_STATE]
            new = s_ref[j, gs, :] * dect_ref[gs, j:j + 1] + xdtt_ref[gs, j:j + 1] * b_row
            so_ref[j, gs, :] = new
            col = jnp.sum(new * c_row, axis=-1, keepdims=True)
            yt_ref[gs, :] = jnp.where(lane_s == j, col, yt_ref[gs, :])
    y = yt_ref[...].T[:bb, :] + dsk_ref[...] * xs
    y = y * _silu(z_ref[...])
    y = y * lax.rsqrt(jnp.mean(y * y, axis=-1, keepdims=True) + RMS_EPS) * nw_ref[...]
    y_ref[...] = y


def _ssd_decode(z, xbc, h1, conv_state, ssm_state, wdt, cw, cb, dtb, alog, dsk, nw, e, *, bb):
    n = z.shape[0]
    return pl.pallas_call(
        functools.partial(_ssd_decode_kernel, bb=bb),
        grid=(n // bb,),
        in_specs=[pl.BlockSpec((bb, D_INNER), lambda i: (i, 0)), pl.BlockSpec((bb, CONV_DIM), lambda i: (i, 0)),
                  pl.BlockSpec((bb, D_MODEL), lambda i: (i, 0)),
                  pl.BlockSpec((CONV_WIDTH - 1, bb, CONV_DIM), lambda i: (0, i, 0)),
                  pl.BlockSpec((bb, D_INNER, D_STATE), lambda i: (i, 0, 0)),
                  _resident(wdt.shape), _resident(cw.shape), _resident(cb.shape), _resident(dtb.shape),
                  _resident(alog.shape), _resident(dsk.shape), _resident(nw.shape), _resident(e.shape)],
        out_specs=[pl.BlockSpec((bb, D_INNER), lambda i: (i, 0)),
                   pl.BlockSpec((bb, D_INNER, D_STATE), lambda i: (i, 0, 0))],
        out_shape=[jax.ShapeDtypeStruct((n, D_INNER), f32),
                   jax.ShapeDtypeStruct((n, D_INNER, D_STATE), f32)],
        scratch_shapes=[pltpu.VMEM((D_INNER, LANES), f32)] * 3,
        compiler_params=_cparams(1),
        name="ssd_decode",
    )(z, xbc, h1, jnp.swapaxes(conv_state, 0, 1), ssm_state, wdt, cw, cb, dtb, alog, dsk, nw, e)


def _attn_prompt_kernel(slope_ref, q_ref, k_ref, v_ref, o_ref, st_ref, kprev_ref, vprev_ref, *, dilation, rows):
    n = pl.program_id(2)
    blk = ATTN_BLOCK
    nb = rows // blk

    @pl.when(n == 0)
    def _():
        kprev_ref[...] = jnp.zeros_like(kprev_ref)
        vprev_ref[...] = jnp.zeros_like(vprev_ref)

    a = lax.broadcasted_iota(jnp.int32, (blk, blk), 0)
    c = lax.broadcasted_iota(jnp.int32, (blk, blk), 1)
    dist_prev = (dilation * (blk + a - c)).astype(f32)
    dist_cur = (dilation * (a - c)).astype(f32)
    valid_cur = c <= a
    valid_prev = c >= a
    valid_prev_first = (c - a + jnp.where(n > 0, 0, -2 * blk)) >= 0
    bias_cur, bias_prev, bias_prev_first = [], [], []
    for h in range(HEADS_PER_PATTERN):
        slope = slope_ref[h:h + 1, :]
        bias_cur.append(jnp.where(valid_cur, -slope * dist_cur, NEG_INF))
        bias_prev.append(jnp.where(valid_prev, -slope * dist_prev, NEG_INF))
        bias_prev_first.append(jnp.where(valid_prev_first, -slope * dist_prev, NEG_INF))

    qs, kcs, vcs, kps, vps, bcs, bps = [], [], [], [], [], [], []
    for i in range(nb):
        rs = slice(i * blk, (i + 1) * blk)
        ps = slice((i - 1) * blk, i * blk)
        for h in range(HEADS_PER_PATTERN):
            hs = slice(h * ATTN_HEAD_DIM, (h + 1) * ATTN_HEAD_DIM)
            qs.append(q_ref[rs, hs])
            kcs.append(k_ref[rs, hs])
            vcs.append(v_ref[rs, hs])
            kps.append(k_ref[ps, hs] if i > 0 else kprev_ref[:, hs])
            vps.append(v_ref[ps, hs] if i > 0 else vprev_ref[:, hs])
            bcs.append(bias_cur[h])
            bps.append(bias_prev[h] if i > 0 else bias_prev_first[h])
    q3 = jnp.stack(qs)
    scale = ATTN_HEAD_DIM ** -0.5
    qk = lambda x, y: jnp.einsum("bqd,bkd->bqk", x, y, preferred_element_type=f32)
    pv = lambda x, y: jnp.einsum("bqk,bkd->bqd", x, y, preferred_element_type=f32)
    s_p = qk(q3, jnp.stack(kps)) * scale + jnp.stack(bps)
    s_c = qk(q3, jnp.stack(kcs)) * scale + jnp.stack(bcs)
    m = jnp.maximum(jnp.max(s_p, axis=-1, keepdims=True), jnp.max(s_c, axis=-1, keepdims=True))
    p_p = jnp.exp(s_p - m)
    p_c = jnp.exp(s_c - m)
    l = jnp.sum(p_p, axis=-1, keepdims=True) + jnp.sum(p_c, axis=-1, keepdims=True)
    o = (pv(p_p.astype(bf16), jnp.stack(vps)) + pv(p_c.astype(bf16), jnp.stack(vcs))) / l
    lse = m + jnp.log(l)
    lane = lax.broadcasted_iota(jnp.int32, (blk, LANES), 1)
    for i in range(nb):
        rs = slice(i * blk, (i + 1) * blk)
        stat = jnp.zeros((blk, LANES), f32)
        for h in range(HEADS_PER_PATTERN):
            idx = i * HEADS_PER_PATTERN + h
            o_ref[rs, h * ATTN_HEAD_DIM:(h + 1) * ATTN_HEAD_DIM] = o[idx].astype(o_ref.dtype)
            stat = jnp.where(lane == h, lse[idx], stat)
        st_ref[rs, :] = stat
    kprev_ref[...] = k_ref[rows - blk:rows, :]
    vprev_ref[...] = v_ref[rows - blk:rows, :]


def _attn_prompt(qkv, slopes, g):
    window, dilation = ATTN_PATTERNS[g]
    assert window // dilation == ATTN_BLOCK
    batch, _, length, _ = qkv.shape
    rows = min(length, 512)
    spec = lambda width, off: pl.BlockSpec((None, None, rows, width), lambda b, r, n: (b, r, n, off))
    return pl.pallas_call(
        functools.partial(_attn_prompt_kernel, dilation=dilation, rows=rows),
        grid=(batch, dilation, length // rows),
        in_specs=[_resident(slopes.shape), spec(ATTN_WIDTH, 0), spec(ATTN_WIDTH, 1), spec(ATTN_WIDTH, 2)],
        out_specs=[spec(ATTN_WIDTH, 0), spec(LANES, 0)],
        out_shape=[jax.ShapeDtypeStruct((batch, dilation, length, ATTN_WIDTH), bf16),
                   jax.ShapeDtypeStruct((batch, dilation, length, LANES), f32)],
        scratch_shapes=[pltpu.VMEM((ATTN_BLOCK, ATTN_WIDTH), bf16), pltpu.VMEM((ATTN_BLOCK, ATTN_WIDTH), bf16)],
        compiler_params=_cparams(3),
        name=f"attn_prompt_d{dilation}",
    )(slopes, qkv, qkv, qkv)


def _attn_decode_kernel(bias_ref, qkv_ref, c0_ref, c1_ref, c2_ref, o_ref):
    scale = ATTN_HEAD_DIM ** -0.5
    outs, ms, ls = [], [], []
    for g, c_ref in enumerate((c0_ref, c1_ref, c2_ref)):
        q = qkv_ref[:, g, 0]
        k_new = qkv_ref[:, g, 1]
        v_new = qkv_ref[:, g, 2]
        s = jnp.sum(c_ref[:, :, 0] * q[:, None], axis=-1, keepdims=True) * scale + bias_ref[g]
        s_new = jnp.sum(k_new * q, axis=-1, keepdims=True) * scale
        m = jnp.maximum(jnp.max(s, axis=1), s_new)
        p = jnp.exp(s - m[:, None])
        p_new = jnp.exp(s_new - m)
        l = jnp.sum(p, axis=1) + p_new
        o = (jnp.sum(p * c_ref[:, :, 1], axis=1) + p_new * v_new) / l
        outs.append(o)
        ms.append(m)
        ls.append(l)
    m_all = jnp.maximum(jnp.maximum(ms[0], ms[1]), ms[2])
    wgts = [l * jnp.exp(m - m_all) for m, l in zip(ms, ls)]
    num = wgts[0] * outs[0] + wgts[1] * outs[1] + wgts[2] * outs[2]
    o_ref[...] = num / (wgts[0] + wgts[1] + wgts[2])


def _attn_decode(qkv, caches, bias, *, bb):
    n = qkv.shape[0]
    in_specs = [_resident(bias.shape),
                pl.BlockSpec((bb, N_PATTERNS, 3, HEADS_PER_PATTERN, ATTN_HEAD_DIM), lambda i: (i, 0, 0, 0, 0))]
    views = []
    for g, (window, dilation) in enumerate(ATTN_PATTERNS):
        assert caches[g].shape[1] == window and window // dilation == ATTN_BLOCK
        views.append(caches[g].reshape(n, ATTN_BLOCK, dilation, 2, HEADS_PER_PATTERN, ATTN_HEAD_DIM))
        in_specs.append(pl.BlockSpec((bb, ATTN_BLOCK, None, 2, HEADS_PER_PATTERN, ATTN_HEAD_DIM),
                                     lambda i: (i, 0, 0, 0, 0, 0)))
    return pl.pallas_call(
        _attn_decode_kernel,
        grid=(n // bb,),
        in_specs=in_specs,
        out_specs=pl.BlockSpec((bb, HEADS_PER_PATTERN, ATTN_HEAD_DIM), lambda i: (i, 0, 0)),
        out_shape=jax.ShapeDtypeStruct((n, HEADS_PER_PATTERN, ATTN_HEAD_DIM), f32),
        compiler_params=_cparams(1),
        name="attn_decode",
    )(bias, qkv, *views)


def _post_mix_kernel(*refs, dilations):
    n_pat = len(dilations)
    merged = n_pat > 1
    h1_ref, y_ref = refs[0], refs[1]
    o_refs = refs[2:2 + n_pat]
    st_refs = refs[2 + n_pat:2 + 2 * n_pat] if merged else ()
    k = 2 + n_pat + len(st_refs)
    wg_ref, wso_ref, wao_ref, wmo_ref, lnw_ref, lnb_ref, out_ref = refs[k:k + 7]
    scratch = refs[k + 7:]
    tm = h1_ref.shape[0]
    h1 = h1_ref[...]
    gates = jax.nn.sigmoid(_dot(h1.astype(bf16), wg_ref[...]))
    if merged:
        o_heads, lses = [], []
        for g, d in enumerate(dilations):
            if d == 1:
                o_heads.append([o_refs[g][:, h * ATTN_HEAD_DIM:(h + 1) * ATTN_HEAD_DIM].astype(f32)
                                for h in range(HEADS_PER_PATTERN)])
                lses.append(st_refs[g])
                continue
            o_scr, st_scr = scratch[0], scratch[1]
            scratch = scratch[2:]
            for r in range(d):
                st_scr[pl.ds(r, tm // d, stride=d), :] = st_refs[g][r]
                for h in range(HEADS_PER_PATTERN):
                    o_scr[h, pl.ds(r, tm // d, stride=d), :] = (
                        o_refs[g][r, :, h * ATTN_HEAD_DIM:(h + 1) * ATTN_HEAD_DIM].astype(f32))
            o_heads.append([o_scr[h] for h in range(HEADS_PER_PATTERN)])
            lses.append(st_scr)
        heads = []
        for h in range(HEADS_PER_PATTERN):
            lse = [st[:, h:h + 1] for st in lses]
            top = functools.reduce(jnp.maximum, lse)
            wgt = [jnp.exp(v - top) for v in lse]
            num = sum(w * o[h] for w, o in zip(wgt, o_heads))
            heads.append(num / sum(wgt))
        attn = jnp.concatenate(heads, axis=1)
    else:
        attn = o_refs[0][...]
    t_ssd = _dot(y_ref[...].astype(bf16), wso_ref[...])
    t_attn = _dot(attn.astype(bf16), wao_ref[...])
    u = gates[:, :D_MODEL] * t_ssd + gates[:, D_MODEL:] * t_attn
    mix = _dot(u.astype(bf16), wmo_ref[...])
    out_ref[...] = _layer_norm(DEEPNORM_ALPHA * h1 + mix, lnw_ref[...], lnb_ref[...])


def _post_mix(h1, y_ssd, outs, stats, dilations, wg, wso, wao, wmo, lnw, lnb, *, tm, seq, carry=None):
    m = h1.shape[0]
    steps = seq // tm
    row = lambda width: pl.BlockSpec((tm, width), lambda i: (i, 0))

    def pattern_spec(width, d):
        if d == 1:
            return row(width)
        return pl.BlockSpec((None, d, tm // d, width), lambda i: (i // steps, 0, i % steps, 0))

    in_specs = [row(D_MODEL), row(D_INNER)] + [pattern_spec(ATTN_WIDTH, d) for d in dilations]
    in_specs += [pattern_spec(LANES, d) for d in dilations[:len(stats)]]
    in_specs += [_resident(w.shape) for w in (wg, wso, wao, wmo, lnw, lnb)]
    scratch = []
    for d in dilations:
        if d > 1:
            scratch += [pltpu.VMEM((HEADS_PER_PATTERN, tm, ATTN_HEAD_DIM), f32), pltpu.VMEM((tm, LANES), f32)]
    return _call(functools.partial(_post_mix_kernel, dilations=tuple(dilations)), grid=(m // tm,), in_specs=in_specs,
                 out_specs=[row(D_MODEL)], out_shape=[jax.ShapeDtypeStruct((m, D_MODEL), f32)],
                 scratch_shapes=scratch, name="post_mix",
                 args=(h1, y_ssd, *outs, *stats, wg, wso, wao, wmo, lnw, lnb), carry=carry)


SHIFT_CHUNK = 256
SHIFT_BLOCK_BYTES = 8 * 1024 * 1024


def _kv_shift_kernel(c_ref, new_ref, o_ref, *, key_rows):
    bb, rows, _ = c_ref.shape
    keep = rows - key_rows
    n_chunks = pl.cdiv(keep, SHIFT_CHUNK)
    for b in range(bb):
        def body(k, carry):
            r0 = pl.multiple_of(jnp.minimum(k * SHIFT_CHUNK, keep - SHIFT_CHUNK), 8)
            o_ref[b, pl.ds(r0, SHIFT_CHUNK), :] = c_ref[b, pl.ds(r0 + key_rows, SHIFT_CHUNK), :]
            return carry
        lax.fori_loop(0, n_chunks, body, 0)
        o_ref[b, keep:rows, :] = new_ref[b]


def _kv_shift(cache, new):
    n, w = cache.shape[:2]
    key_rows = 2 * HEADS_PER_PATTERN * ATTN_HEAD_DIM // LANES
    rows = w * key_rows
    assert key_rows % 8 == 0 and rows - key_rows >= SHIFT_CHUNK
    bb = max(1, min(n, SHIFT_BLOCK_BYTES // (rows * LANES * 4)))
    assert n % bb == 0
    out = pl.pallas_call(
        functools.partial(_kv_shift_kernel, key_rows=key_rows),
        grid=(n // bb,),
        in_specs=[pl.BlockSpec((bb, rows, LANES), lambda i: (i, 0, 0)),
                  pl.BlockSpec((bb, key_rows, LANES), lambda i: (i, 0, 0))],
        out_specs=pl.BlockSpec((bb, rows, LANES), lambda i: (i, 0, 0)),
        out_shape=jax.ShapeDtypeStruct((n, rows, LANES), cache.dtype),
        compiler_params=_cparams(1),
        name=f"kv_shift_w{w}",
    )(cache.reshape(n, rows, LANES), new.reshape(n, key_rows, LANES))
    return out.reshape(cache.shape)


def _alibi_slopes():
    n = N_PATTERNS * HEADS_PER_PATTERN
    e = jnp.arange(1, n + 1, dtype=f32)
    return (2.0 ** (-ALIBI_MAX_EXP * e / n)).reshape(N_PATTERNS, HEADS_PER_PATTERN)


def _prepare_weights(ln_w, ln_b, ffn_w13, ffn_w2, w_in, conv_w, conv_b, dt_bias, a_log, d_skip, ssd_norm_w,
                     w_ssd_out, w_attn_out, w_mix_out, w_pe_gate, w_pe_proj):
    cuts = [0]
    for s in IN_SPLITS:
        cuts.append(cuts[-1] + s)
    wz, wxbc, wdt, wq, wk, wv = (w_in[:, cuts[i]:cuts[i + 1]] for i in range(6))
    pad_heads = LANES - N_SSD_HEADS
    head_of_col = jnp.arange(D_INNER, dtype=jnp.int32) // SSD_HEAD_DIM
    return dict(
        ln_w=[ln_w[i:i + 1] for i in range(3)], ln_b=[ln_b[i:i + 1] for i in range(3)],
        w13=[ffn_w13[i].astype(bf16) for i in range(2)], w2=[ffn_w2[i].astype(bf16) for i in range(2)],
        wz=wz.astype(bf16), wxbc=wxbc.astype(bf16),
        wdt=jnp.pad(wdt, ((0, 0), (0, pad_heads))).astype(bf16),
        wqkv=jnp.concatenate([m[:, g * ATTN_WIDTH:(g + 1) * ATTN_WIDTH] for g in range(N_PATTERNS)
                              for m in (wq, wk, wv)], axis=1).astype(bf16),
        wg=w_in[:, cuts[6]:].astype(bf16),
        cw=conv_w, cb=conv_b[None, :],
        dtb=jnp.pad(dt_bias, (0, pad_heads))[None, :], alog=jnp.pad(a_log, (0, pad_heads))[None, :],
        dsk=jnp.repeat(d_skip, SSD_HEAD_DIM)[None, :], nw=ssd_norm_w[None, :],
        expand=(jnp.arange(LANES, dtype=jnp.int32)[:, None] == head_of_col[None, :]).astype(f32),
        wso=w_ssd_out.astype(bf16), wao=w_attn_out.astype(bf16), wmo=w_mix_out.astype(bf16),
        wpg=w_pe_gate.astype(bf16), wpp=w_pe_proj.astype(bf16),
    )


def _ssd_args(w):
    return (w["wdt"], w["cw"], w["cb"], w["dtb"], w["alog"], w["dsk"], w["nw"], w["expand"])


class _CacheCarrier:
    def __init__(self, cache, new):
        n, w = cache.shape[:2]
        self.shape = cache.shape
        self.key_rows = 2 * HEADS_PER_PATTERN * ATTN_HEAD_DIM // LANES
        self.cache = cache.reshape(n, w * self.key_rows, LANES)
        self.new = new.reshape(n, self.key_rows, LANES)
        self.out, self.done = None, 0

    def take(self, steps):
        count = min(steps, self.cache.shape[0] - self.done)
        return (self.cache, self.new, self.out, self.done, count) if count > 0 else None

    def gave(self, carry, out):
        if carry is not None:
            self.out, self.done = out, self.done + carry[4]

    def result(self):
        assert self.done == self.cache.shape[0]
        return self.out.reshape(self.shape)


def _prompt_layer(x, p, w, carrier):
    batch, seq, _ = x.shape
    assert seq % SSD_CHUNK == 0 and all(seq % (ATTN_BLOCK * d) == 0 and seq >= win for win, d in ATTN_PATTERNS)
    tm = 512
    steps = batch * seq // tm
    x2 = x.reshape(batch * seq, D_MODEL)

    def carried(fn, *args, **kwargs):
        carry = carrier.take(steps) if carrier is not None else None
        outs, cache_out = fn(*args, carry=carry, **kwargs)
        if carrier is not None:
            carrier.gave(carry, cache_out)
        return outs

    h1, = carried(_ffn_ln, x2, w["w13"][0], w["w2"][0], w["ln_w"][0], w["ln_b"][0], tm=tm)
    z, xc, xbc_tail = _proj_zx(h1, w["wz"], w["wxbc"], w["cw"], w["cb"], batch=batch, seq=seq, tm=tm)
    qkv = carried(_qkv_proj, h1, w["wqkv"], batch=batch, seq=seq, tm=tm)
    y_ssd, new_ssm = _ssd_prompt(z, xc, h1, w["wdt"], w["dtb"], w["alog"], w["dsk"], w["nw"], batch=batch)
    slopes = _alibi_slopes()
    dilations = [d for _, d in ATTN_PATTERNS]
    outs, stats, new_kv = [], [], []
    for g, (win, d) in enumerate(ATTN_PATTERNS):
        o, st = _attn_prompt(qkv[g], jnp.broadcast_to(slopes[g][:, None], (HEADS_PER_PATTERN, LANES)), g)
        if d == 1:
            o, st = o.reshape(batch * seq, ATTN_WIDTH), st.reshape(batch * seq, LANES)
        outs.append(o)
        stats.append(st)
        tail = qkv[g][:, :, seq // d - win // d:, ATTN_WIDTH:]
        new_kv.append(jnp.swapaxes(tail, 1, 2).reshape(batch, win, 2, HEADS_PER_PATTERN, ATTN_HEAD_DIM).astype(f32))
    h2, = carried(_post_mix, h1, y_ssd, outs, stats, dilations, w["wg"], w["wso"], w["wao"], w["wmo"], w["ln_w"][1],
                  w["ln_b"][1], tm=tm, seq=seq)
    y, = carried(_ffn_ln, h2, w["w13"][1], w["w2"][1], w["ln_w"][2], w["ln_b"][2],
                 pe=(p.reshape(batch * seq, PLE_DIM), w["wpg"], w["wpp"]), tm=tm)
    new_conv = xbc_tail[:, 8 - (CONV_WIDTH - 1):]
    return (y.reshape(batch, seq, D_MODEL), new_ssm.reshape(batch, N_SSD_HEADS, SSD_HEAD_DIM, D_STATE),
            new_conv, new_kv)


def _decode_layer(x, p, ssm_state, conv_state, caches, w):
    n, t, _ = x.shape
    assert t == 1
    tm = n
    bb = 8
    (h1,), _ = _ffn_ln(x.reshape(n, D_MODEL), w["w13"][0], w["w2"][0], w["ln_w"][0], w["ln_b"][0], tm=tm)
    z = _matmul(h1, w["wz"], f32, tm=tm)
    xbc = _matmul(h1, w["wxbc"], f32, tm=tm)
    qkv = _matmul(h1, w["wqkv"], f32, tm=tm)
    y_ssd, new_ssm = _ssd_decode(z, xbc, h1, conv_state, ssm_state.reshape(n, D_INNER, D_STATE), *_ssd_args(w), bb=bb)
    slopes = _alibi_slopes()
    steps_back = (ATTN_BLOCK - jnp.arange(ATTN_BLOCK, dtype=jnp.int32))
    bias = jnp.stack([-slopes[g][None, :] * (d * steps_back).astype(f32)[:, None]
                      for g, (_, d) in enumerate(ATTN_PATTERNS)])
    bias = jnp.broadcast_to(bias[..., None], bias.shape + (1,))
    qkv5 = qkv.reshape(n, N_PATTERNS, 3, HEADS_PER_PATTERN, ATTN_HEAD_DIM)
    attn = _attn_decode(qkv5, caches, bias, bb=bb).reshape(n, ATTN_WIDTH)
    (h2,), _ = _post_mix(h1, y_ssd, [attn], [], [1], w["wg"], w["wso"], w["wao"], w["wmo"], w["ln_w"][1],
                         w["ln_b"][1], tm=tm, seq=n)
    (y,), _ = _ffn_ln(h2, w["w13"][1], w["w2"][1], w["ln_w"][2], w["ln_b"][2],
                      pe=(p.reshape(n, PLE_DIM), w["wpg"], w["wpp"]), tm=tm)
    new_conv = jnp.concatenate([conv_state[:, 1:], xbc[:, None, :]], axis=1)
    new_rows = [qkv5[:, g, 1:3] for g in range(N_PATTERNS)]
    return y.reshape(n, 1, D_MODEL), new_ssm.reshape(n, N_SSD_HEADS, SSD_HEAD_DIM, D_STATE), new_conv, new_rows


CARRIER_CALLS = 4


def kernel(x_prompt, x_sample, state_ssm, state_conv, cache_kv_w128, cache_kv_w512, cache_kv_w2048, p_prompt, p_sample, ln_w, ln_b, ffn_w13, ffn_w2, w_in, conv_w, conv_b, dt_bias, a_log, d_skip, ssd_norm_w, w_ssd_out, w_attn_out, w_mix_out, w_pe_gate, w_pe_proj):
    assert ln_w.shape[0] == DEPTH
    w = _prepare_weights(ln_w[0], ln_b[0], ffn_w13[0], ffn_w2[0], w_in[0], conv_w[0], conv_b[0], dt_bias[0], a_log[0],
                         d_skip[0], ssd_norm_w[0], w_ssd_out[0], w_attn_out[0], w_mix_out[0], w_pe_gate[0],
                         w_pe_proj[0])
    caches = (cache_kv_w128[0], cache_kv_w512[0], cache_kv_w2048[0])
    y_s, ssm_s, conv_s, new_rows = _decode_layer(x_sample, p_sample[0], state_ssm[0], state_conv[0], caches, w)
    big = max(range(N_PATTERNS), key=lambda g: caches[g].shape[1])
    prompt_steps = x_prompt.shape[0] * x_prompt.shape[1] // 512
    carrier = None
    if caches[big].shape[0] <= CARRIER_CALLS * prompt_steps:
        carrier = _CacheCarrier(caches[big], new_rows[big])
    y_p, ssm_p, conv_p, kv_p = _prompt_layer(x_prompt, p_prompt[0], w, carrier)
    kv_s = [carrier.result() if carrier is not None and g == big else _kv_shift(caches[g], new_rows[g])
            for g in range(N_PATTERNS)]
    lead = lambda a: a[None]
    return (y_p, y_s, lead(ssm_p), lead(conv_p), lead(kv_p[0]), lead(kv_p[1]), lead(kv_p[2]),
            lead(ssm_s), lead(conv_s), lead(kv_s[0]), lead(kv_s[1]), lead(kv_s[2]))
```

```python
import functools

import jax
import jax.numpy as jnp
from jax import lax
from jax.experimental import pallas as pl
from jax.experimental.pallas import tpu as pltpu

f32 = jnp.float32
bf16 = jnp.bfloat16

D_MODEL = 1024
D_INNER = 2 * D_MODEL
SSD_HEAD_DIM = 64
N_SSD_HEADS = D_INNER // SSD_HEAD_DIM
N_SSD_GROUPS = 4
HEADS_PER_GROUP = N_SSD_HEADS // N_SSD_GROUPS
GROUP_WIDTH = HEADS_PER_GROUP * SSD_HEAD_DIM
D_STATE = 128
CONV_WIDTH = 4
CONV_DIM = D_INNER + 2 * N_SSD_GROUPS * D_STATE
SSD_CHUNK = 128
ATTN_PATTERNS = ((128, 1), (512, 4), (2048, 16))
N_PATTERNS = 3
HEADS_PER_PATTERN = 4
ATTN_HEAD_DIM = 128
ATTN_WIDTH = HEADS_PER_PATTERN * ATTN_HEAD_DIM
ATTN_QKV_WIDTH = N_PATTERNS * ATTN_WIDTH
PATTERN_QKV_WIDTH = 3 * ATTN_WIDTH
ATTN_BLOCK = 128
ALIBI_MAX_EXP = 8.0
D_FF = 2816
PLE_DIM = 256
LN_EPS = 1e-5
RMS_EPS = 1e-5
DEPTH = 1
DEEPNORM_ALPHA = (2.0 * DEPTH) ** 0.25
IN_SPLITS = (D_INNER, CONV_DIM, N_SSD_HEADS, ATTN_QKV_WIDTH, ATTN_QKV_WIDTH, ATTN_QKV_WIDTH)

LANES = 128
VMEM_LIMIT = 56 * 1024 * 1024
NEG_INF = float("-inf")


def _cparams(n_axes):
    return pltpu.CompilerParams(dimension_semantics=("arbitrary",) * n_axes, vmem_limit_bytes=VMEM_LIMIT)


def _resident(shape):
    nd = len(shape)
    return pl.BlockSpec(shape, lambda *_: (0,) * nd, pipeline_mode=pl.Buffered(1))


def _layer_norm(y, w, b):
    mu = jnp.mean(y, axis=-1, keepdims=True)
    yc = y - mu
    var = jnp.mean(yc * yc, axis=-1, keepdims=True)
    return yc * lax.rsqrt(var + LN_EPS) * w + b


def _silu(x):
    return x * jax.nn.sigmoid(x)


def _softplus(x):
    return jnp.maximum(x, 0.0) + jnp.log1p(jnp.exp(-jnp.abs(x)))


def _dot(a, b):
    return jnp.dot(a, b, preferred_element_type=f32)


def _dot_nt(a, b):
    return lax.dot_general(a, b, (((1,), (1,)), ((), ())), preferred_element_type=f32)


def _dot_tn(a, b):
    return lax.dot_general(a, b, (((0,), (0,)), ((), ())), preferred_element_type=f32)


def _dot_exact(a, b):
    return jnp.dot(a, b, precision=lax.Precision.HIGHEST, preferred_element_type=f32)


def _carry_where(chunk, base, parts, cache_ref, new_ref, out_ref, buf_ref, sems):
    c = base + chunk
    part = lax.rem(c, parts)
    return lax.rem(chunk, 2), c // parts, part, pl.multiple_of(part * buf_ref.shape[1], 8)


def _carry_in(chunk, op, *refs):
    _, parts, cache_ref, new_ref, _, buf_ref, sems = refs
    key_rows, part_rows = new_ref.shape[1], buf_ref.shape[1]
    slot, b, part, r0 = _carry_where(chunk, *refs)

    @pl.when(part < parts - 1)
    def _():
        op(pltpu.make_async_copy(cache_ref.at[b, pl.ds(r0 + key_rows, part_rows)], buf_ref.at[slot], sems.at[0, slot]))

    @pl.when(part == parts - 1)
    def _():
        op(pltpu.make_async_copy(cache_ref.at[b, pl.ds(r0 + key_rows, part_rows - key_rows)],
                                 buf_ref.at[slot, pl.ds(0, part_rows - key_rows)], sems.at[0, slot]))
        op(pltpu.make_async_copy(new_ref.at[b], buf_ref.at[slot, pl.ds(part_rows - key_rows, key_rows)],
                                 sems.at[1, slot]))


def _carry_out(chunk, op, *refs):
    _, _, _, _, out_ref, buf_ref, sems = refs
    slot, b, _, r0 = _carry_where(chunk, *refs)
    op(pltpu.make_async_copy(buf_ref.at[slot], out_ref.at[b, pl.ds(r0, buf_ref.shape[1])], sems.at[2, slot]))


def _start(copy):
    copy.start()


def _wait(copy):
    copy.wait()


def _carry_begin(step, n_chunks, *refs):
    @pl.when(step == 0)
    def _():
        _carry_in(step, _start, *refs)

    @pl.when(step < n_chunks)
    def _():
        _carry_in(step, _wait, *refs)
        _carry_out(step, _start, *refs)

        @pl.when(step > 0)
        def _():
            _carry_out(step - 1, _wait, *refs)

        @pl.when(step + 1 < n_chunks)
        def _():
            _carry_in(step + 1, _start, *refs)


def _carry_end(step, n_chunks, *refs):
    @pl.when(step == n_chunks - 1)
    def _():
        _carry_out(step, _wait, *refs)


def _call(body, *, grid, in_specs, out_specs, out_shape, scratch_shapes=(), name, args, carry=None):
    out_specs, out_shape = list(out_specs), list(out_shape)
    if carry is None:
        res = pl.pallas_call(body, grid=grid, in_specs=list(in_specs), out_specs=out_specs, out_shape=out_shape,
                             scratch_shapes=list(scratch_shapes), compiler_params=_cparams(len(grid)), name=name)(*args)
        return list(res), None
    cache, new, prev, base, count, parts = carry
    part_rows = cache.shape[1] // parts
    assert cache.shape[1] % parts == 0 and part_rows % 8 == 0 and part_rows > new.shape[1]
    assert len(grid) == 1 and 0 < count <= grid[0]
    n_in, n_out, n_scr = len(in_specs), len(out_specs), len(scratch_shapes)
    n_cin = 2 if prev is None else 3

    def kernel(*refs):
        ins = refs[:n_in]
        cache_ref, new_ref = refs[n_in:n_in + 2]
        outs = refs[n_in + n_cin:n_in + n_cin + n_out]
        out_cache_ref = refs[n_in + n_cin + n_out]
        scratch = refs[n_in + n_cin + n_out + 1:n_in + n_cin + n_out + 1 + n_scr]
        buf_ref, sems = refs[-2:]
        step = pl.program_id(0)
        carry_refs = (base, parts, cache_ref, new_ref, out_cache_ref, buf_ref, sems)
        _carry_begin(step, count, *carry_refs)
        body(*ins, *outs, *scratch)
        _carry_end(step, count, *carry_refs)

    any_spec = pl.BlockSpec(memory_space=pl.ANY)
    cargs = (cache, new) if prev is None else (cache, new, prev)
    res = pl.pallas_call(
        kernel, grid=grid,
        in_specs=list(in_specs) + [any_spec] * n_cin,
        out_specs=out_specs + [any_spec],
        out_shape=out_shape + [jax.ShapeDtypeStruct(cache.shape, cache.dtype)],
        scratch_shapes=list(scratch_shapes) + [pltpu.VMEM((2, part_rows, cache.shape[2]), cache.dtype),
                                               pltpu.SemaphoreType.DMA((3, 2))],
        input_output_aliases={} if prev is None else {n_in + 2: n_out},
        compiler_params=_cparams(1), name=name + "_carry")(*args, *cargs)
    return list(res[:n_out]), res[n_out]


FFN_CHUNKS = (512, 512, 512, 512, 512, 256)


def _ffn_ln_kernel(*refs, with_pe):
    if with_pe:
        x_ref, w13_ref, w2_ref, lnw_ref, lnb_ref, p_ref, wpg_ref, wpp_ref, o_ref = refs
    else:
        x_ref, w13_ref, w2_ref, lnw_ref, lnb_ref, o_ref = refs
    x = x_ref[...]
    xb = x.astype(bf16)
    acc = None
    c0 = 0
    for width in FFN_CHUNKS:
        gate = _dot(xb, w13_ref[:, c0:c0 + width])
        up = _dot(xb, w13_ref[:, D_FF + c0:D_FF + c0 + width])
        part = _dot((_silu(gate) * up).astype(bf16), w2_ref[c0:c0 + width, :])
        acc = part if acc is None else acc + part
        c0 += width
    h = _layer_norm(DEEPNORM_ALPHA * x + 0.5 * acc, lnw_ref[...], lnb_ref[...])
    if with_pe:
        pe_gate = jax.nn.sigmoid(_dot(h.astype(bf16), wpg_ref[...]))
        h = h + pe_gate * _dot(p_ref[...].astype(bf16), wpp_ref[...])
    o_ref[...] = h


def _ffn_ln(x, w13, w2, lnw, lnb, pe=None, *, tm, carry=None):
    m = x.shape[0]
    assert sum(FFN_CHUNKS) == D_FF
    row = lambda width: pl.BlockSpec((tm, width), lambda i: (i, 0))
    in_specs = [row(D_MODEL), _resident(w13.shape), _resident(w2.shape), _resident(lnw.shape), _resident(lnb.shape)]
    args = [x, w13, w2, lnw, lnb]
    if pe is not None:
        p, wpg, wpp = pe
        in_specs += [row(PLE_DIM), _resident(wpg.shape), _resident(wpp.shape)]
        args += [p, wpg, wpp]
    return _call(functools.partial(_ffn_ln_kernel, with_pe=pe is not None), grid=(m // tm,), in_specs=in_specs,
                 out_specs=[row(D_MODEL)], out_shape=[jax.ShapeDtypeStruct((m, D_MODEL), f32)],
                 name="ffn_ln_pe" if pe is not None else "ffn_ln", args=args, carry=carry)


def _matmul_kernel(x_ref, w_ref, o_ref):
    o_ref[...] = _dot(x_ref[...].astype(bf16), w_ref[...]).astype(o_ref.dtype)


def _matmul(x, w, out_dtype, *, tm):
    m, k = x.shape
    n = w.shape[1]
    return pl.pallas_call(
        _matmul_kernel,
        grid=(m // tm,),
        in_specs=[pl.BlockSpec((tm, k), lambda i: (i, 0)), _resident((k, n))],
        out_specs=pl.BlockSpec((tm, n), lambda i: (i, 0)),
        out_shape=jax.ShapeDtypeStruct((m, n), out_dtype),
        compiler_params=_cparams(1),
        name="proj_matmul",
    )(x, w)


def _qkv_proj_kernel(x_ref, w_ref, *rest):
    outs, res_ref = rest[:N_PATTERNS], rest[N_PATTERNS]
    tm = x_ref.shape[0]
    xb = x_ref[...].astype(bf16)
    wide = 2 * LANES
    for c2 in range(N_PATTERNS * PATTERN_QKV_WIDTH // wide):
        res = _dot(xb, w_ref[:, c2 * wide:(c2 + 1) * wide])
        res_ref[2 * c2] = res[:, :LANES]
        res_ref[2 * c2 + 1] = res[:, LANES:]
    blocks = PATTERN_QKV_WIDTH // LANES
    for g, (_, d) in enumerate(ATTN_PATTERNS):
        for cb in range(blocks):
            cs = slice(cb * LANES, (cb + 1) * LANES)
            for r in range(d):
                outs[g][r, :, cs] = res_ref[g * blocks + cb, pl.ds(r, tm // d, stride=d), :].astype(bf16)


def _qkv_proj(x, w, *, batch, seq, tm, carry=None):
    m, k = x.shape
    steps = seq // tm
    return _call(
        _qkv_proj_kernel,
        grid=(m // tm,),
        in_specs=[pl.BlockSpec((tm, k), lambda i: (i, 0)), _resident(w.shape)],
        out_specs=[pl.BlockSpec((None, d, tm // d, PATTERN_QKV_WIDTH), lambda i: (i // steps, 0, i % steps, 0))
                   for _, d in ATTN_PATTERNS],
        out_shape=[jax.ShapeDtypeStruct((batch, d, seq // d, PATTERN_QKV_WIDTH), bf16) for _, d in ATTN_PATTERNS],
        scratch_shapes=[pltpu.VMEM((N_PATTERNS * PATTERN_QKV_WIDTH // LANES, tm, LANES), f32)],
        name="qkv_proj", args=(x, w), carry=carry)


CONV_ROWS = 64


def _proj_zx_kernel(x_ref, wz_ref, wx_ref, cw_ref, cb_ref, z_ref, xc_ref, tail_ref, xpad_ref, *, steps):
    i = pl.program_id(0)
    tm = x_ref.shape[0]
    halo = 8
    xb = x_ref[...].astype(bf16)
    z_ref[...] = _dot(xb, wz_ref[...]).astype(z_ref.dtype)

    @pl.when(i % steps == 0)
    def _():
        xpad_ref[0:halo, :] = jnp.zeros((halo, CONV_DIM), f32)

    for j in range(CONV_DIM // 512):
        xpad_ref[halo:halo + tm, j * 512:(j + 1) * 512] = _dot(xb, wx_ref[:, j * 512:(j + 1) * 512])
        for cb in range(2 * j, 2 * j + 2):
            cs = slice(cb * 256, (cb + 1) * 256)
            taps = [cw_ref[w:w + 1, cs] for w in range(CONV_WIDTH)]
            bias = cb_ref[:, cs]
            for r0 in range(0, tm, CONV_ROWS):
                blk = xpad_ref[r0:r0 + halo + CONV_ROWS, cs]
                acc = bias + blk[halo:] * taps[CONV_WIDTH - 1]
                for k in range(1, CONV_WIDTH):
                    acc = acc + pltpu.roll(blk, k, 0)[halo:] * taps[CONV_WIDTH - 1 - k]
                xc_ref[r0:r0 + CONV_ROWS, cs] = _silu(acc).astype(xc_ref.dtype)
    tail_ref[...] = xpad_ref[tm:tm + halo, :]
    xpad_ref[0:halo, :] = xpad_ref[tm:tm + halo, :]


def _proj_zx(x, wz, wx, cw, cb, *, batch, seq, tm, carry=None):
    m, k = x.shape
    steps = seq // tm
    return _call(
        functools.partial(_proj_zx_kernel, steps=steps),
        grid=(m // tm,),
        in_specs=[pl.BlockSpec((tm, k), lambda i: (i, 0)), _resident(wz.shape), _resident(wx.shape),
                  _resident(cw.shape), _resident(cb.shape)],
        out_specs=[pl.BlockSpec((tm, D_INNER), lambda i: (i, 0)), pl.BlockSpec((tm, CONV_DIM), lambda i: (i, 0)),
                   pl.BlockSpec((None, 8, CONV_DIM), lambda i: (i // steps, 0, 0))],
        out_shape=[jax.ShapeDtypeStruct((m, D_INNER), bf16), jax.ShapeDtypeStruct((m, CONV_DIM), bf16),
                   jax.ShapeDtypeStruct((batch, 8, CONV_DIM), f32)],
        scratch_shapes=[pltpu.VMEM((tm + 8, CONV_DIM), f32)],
        name="proj_zx", args=(x, wz, wx, cw, cb), carry=carry)


def _split3(x):
    hi = x.astype(bf16)
    r = x - hi.astype(f32)
    mid = r.astype(bf16)
    return hi, mid, (r - mid.astype(f32)).astype(bf16)


def _ssd_prompt_kernel(z_ref, xc_ref, h1_ref, wdt_ref, dtb_ref, alog_ref, dsk_ref, nw_ref, y_ref, so_ref,
                       st_ref, dt_ref, cum_ref, cumt_ref, xw_ref, yacc_ref, *, nc):
    c = pl.program_id(0) % nc
    q = SSD_CHUNK

    @pl.when(c == 0)
    def _():
        st_ref[...] = jnp.zeros_like(st_ref)

    lane = lax.broadcasted_iota(jnp.int32, (q, LANES), 1)
    row = lax.broadcasted_iota(jnp.int32, (q, LANES), 0)
    dt = _softplus(_dot(h1_ref[...].astype(bf16), wdt_ref[...]) + dtb_ref[...])
    dt = jnp.where(lane < N_SSD_HEADS, dt, 0.0)
    d_a = dt * (-jnp.exp(alog_ref[...]))
    causal = row >= lane
    c3 = _dot(causal.astype(bf16), jnp.concatenate(_split3(d_a), axis=1))
    cum = c3[:, :LANES] + c3[:, LANES:2 * LANES] + c3[:, 2 * LANES:]
    dt_ref[...] = dt
    cum_ref[...] = cum
    cumt_ref[...] = cum.T

    lo_half = lane < SSD_HEAD_DIM
    lo_half_row = lo_half[0:1, :]
    for g in range(N_SSD_GROUPS):
        b_g = xc_ref[:, D_INNER + g * D_STATE:D_INNER + (g + 1) * D_STATE]
        c_g = xc_ref[:, D_INNER + (N_SSD_GROUPS + g) * D_STATE:D_INNER + (N_SSD_GROUPS + g + 1) * D_STATE]
        cb = _dot_nt(c_g, b_g)
        y_off = _dot(c_g, st_ref[g].astype(bf16))
        chunk_decay = []
        for jj in range(HEADS_PER_GROUP // 2):
            j = g * (HEADS_PER_GROUP // 2) + jj
            cs = slice(j * LANES, (j + 1) * LANES)
            ha, hb = 2 * j, 2 * j + 1
            x_p = xc_ref[:, cs].astype(f32)
            cum_a, cum_b = cum_ref[:, ha:ha + 1], cum_ref[:, hb:hb + 1]
            cume = jnp.where(lo_half, cum_a, cum_b)
            cume_last = jnp.where(lo_half_row, cum_a[q - 1:q, :], cum_b[q - 1:q, :])
            xdt = x_p * jnp.where(lo_half, dt_ref[:, ha:ha + 1], dt_ref[:, hb:hb + 1])
            ls = []
            for h, cum_h in ((ha, cum_a), (hb, cum_b)):
                seg = cum_h - cumt_ref[h:h + 1, :]
                ls.append(jnp.exp(jnp.where(causal, seg, NEG_INF)) * cb)
            l2 = jnp.concatenate(ls, axis=1).astype(bf16)
            x2 = jnp.concatenate([jnp.where(lo_half, xdt, 0.0), jnp.where(lo_half, 0.0, xdt)], axis=0).astype(bf16)
            y = _dot(l2, x2) + y_off[:, jj * LANES:(jj + 1) * LANES] * jnp.exp(cume) + dsk_ref[:, cs] * x_p
            yacc_ref[:, cs] = y * _silu(z_ref[:, cs].astype(f32))
            xw_ref[:, cs] = (xdt * jnp.exp(cume_last - cume)).astype(bf16)
            chunk_decay.append(jnp.exp(cume_last))
        gs = slice(g * GROUP_WIDTH, (g + 1) * GROUP_WIDTH)
        st_ref[g] = st_ref[g] * jnp.concatenate(chunk_decay, axis=1) + _dot_tn(b_g, xw_ref[:, gs])

    y = yacc_ref[...]
    y = y * lax.rsqrt(jnp.mean(y * y, axis=-1, keepdims=True) + RMS_EPS) * nw_ref[...]
    y_ref[...] = y.astype(y_ref.dtype)

    @pl.when(c == nc - 1)
    def _():
        for g in range(N_SSD_GROUPS):
            so_ref[g * GROUP_WIDTH:(g + 1) * GROUP_WIDTH, :] = st_ref[g].T


def _ssd_prompt(z, xc, h1, wdt, dtb, alog, dsk, nw, *, batch, carry=None):
    m = z.shape[0]
    nc = m // batch // SSD_CHUNK
    q = SSD_CHUNK
    row_map = lambda i: (i, 0)
    return _call(
        functools.partial(_ssd_prompt_kernel, nc=nc),
        grid=(batch * nc,),
        in_specs=[pl.BlockSpec((q, D_INNER), row_map), pl.BlockSpec((q, CONV_DIM), row_map),
                  pl.BlockSpec((q, D_MODEL), row_map),
                  _resident(wdt.shape), _resident(dtb.shape), _resident(alog.shape), _resident(dsk.shape),
                  _resident(nw.shape)],
        out_specs=[pl.BlockSpec((q, D_INNER), row_map),
                   pl.BlockSpec((None, D_INNER, D_STATE), lambda i: (i // nc, 0, 0))],
        out_shape=[jax.ShapeDtypeStruct((m, D_INNER), bf16),
                   jax.ShapeDtypeStruct((batch, D_INNER, D_STATE), f32)],
        scratch_shapes=[pltpu.VMEM((N_SSD_GROUPS, D_STATE, GROUP_WIDTH), f32),
                        pltpu.VMEM((q, LANES), f32), pltpu.VMEM((q, LANES), f32), pltpu.VMEM((LANES, q), f32),
                        pltpu.VMEM((q, D_INNER), bf16), pltpu.VMEM((q, D_INNER), f32)],
        name="ssd_prompt", args=(z, xc, h1, wdt, dtb, alog, dsk, nw), carry=carry)


def _ssd_decode_kernel(z_ref, xbc_ref, h1_ref, cst_ref, s_ref, wdt_ref, cw_ref, cb_ref, dtb_ref, alog_ref, dsk_ref,
                       nw_ref, e_ref, y_ref, so_ref, xdtt_ref, yt_ref, *, bb):
    acc = cb_ref[...] + xbc_ref[...] * cw_ref[CONV_WIDTH - 1:CONV_WIDTH, :]
    for w in range(CONV_WIDTH - 1):
        acc = acc + cst_ref[w] * cw_ref[w:w + 1, :]
    xc = _silu(acc)
    xs = xc[:, :D_INNER]
    lane_h = lax.broadcasted_iota(jnp.int32, (bb, LANES), 1)
    dt = _softplus(_dot(h1_ref[...].astype(bf16), wdt_ref[...]) + dtb_ref[...])
    dt = jnp.where(lane_h < N_SSD_HEADS, dt, 0.0)
    decay = jnp.exp(dt * (-jnp.exp(alog_ref[...])))
    xdt = xs * _dot_exact(dt, e_ref[...])
    pad = lambda a: jnp.concatenate([a, jnp.zeros((LANES - bb, a.shape[1]), f32)], axis=0)
    xdtt_ref[...] = pad(xdt).T.astype(bf16)
    row = lax.broadcasted_iota(jnp.int32, (LANES, LANES), 0)
    lane = lax.broadcasted_iota(jnp.int32, (LANES, LANES), 1)
    for g in range(N_SSD_GROUPS):
        gs = slice(g * GROUP_WIDTH, (g + 1) * GROUP_WIDTH)
        b_pad = pad(xc[:, D_INNER + g * D_STATE:D_INNER + (g + 1) * D_STATE])
        c_pad_t = pad(xc[:, D_INNER + (N_SSD_GROUPS + g) * D_STATE:D_INNER + (N_SSD_GROUPS + g + 1) * D_STATE]).T
        y_t = jnp.zeros((GROUP_WIDTH, LANES), f32)
        for j in range(bb):
            outer = _dot(xdtt_ref[gs, :], jnp.where(row == j, b_pad, 0.0).astype(bf16))
            pieces = []
            for hh in range(HEADS_PER_GROUP):
                h = g * HEADS_PER_GROUP + hh
                rs = slice(hh * SSD_HEAD_DIM, (hh + 1) * SSD_HEAD_DIM)
                new = s_ref[j, g * GROUP_WIDTH + hh * SSD_HEAD_DIM:g * GROUP_WIDTH + (hh + 1) * SSD_HEAD_DIM, :] \
                    * decay[j:j + 1, h:h + 1] + outer[rs, :]
                so_ref[j, g * GROUP_WIDTH + hh * SSD_HEAD_DIM:g * GROUP_WIDTH + (hh + 1) * SSD_HEAD_DIM, :] = new
                pieces.append(new.astype(bf16))
            y_t = y_t + _dot(jnp.concatenate(pieces, axis=0), jnp.where(lane == j, c_pad_t, 0.0).astype(bf16))
        yt_ref[gs, :] = y_t
    y = yt_ref[...].T[:bb, :] + dsk_ref[...] * xs
    y = y * _silu(z_ref[...])
    y = y * lax.rsqrt(jnp.mean(y * y, axis=-1, keepdims=True) + RMS_EPS) * nw_ref[...]
    y_ref[...] = y


def _ssd_decode(z, xbc, h1, conv_state, ssm_state, wdt, cw, cb, dtb, alog, dsk, nw, e, *, bb):
    n = z.shape[0]
    return pl.pallas_call(
        functools.partial(_ssd_decode_kernel, bb=bb),
        grid=(n // bb,),
        in_specs=[pl.BlockSpec((bb, D_INNER), lambda i: (i, 0)), pl.BlockSpec((bb, CONV_DIM), lambda i: (i, 0)),
                  pl.BlockSpec((bb, D_MODEL), lambda i: (i, 0)),
                  pl.BlockSpec((CONV_WIDTH - 1, bb, CONV_DIM), lambda i: (0, i, 0)),
                  pl.BlockSpec((bb, D_INNER, D_STATE), lambda i: (i, 0, 0)),
                  _resident(wdt.shape), _resident(cw.shape), _resident(cb.shape), _resident(dtb.shape),
                  _resident(alog.shape), _resident(dsk.shape), _resident(nw.shape), _resident(e.shape)],
        out_specs=[pl.BlockSpec((bb, D_INNER), lambda i: (i, 0)),
                   pl.BlockSpec((bb, D_INNER, D_STATE), lambda i: (i, 0, 0))],
        out_shape=[jax.ShapeDtypeStruct((n, D_INNER), f32),
                   jax.ShapeDtypeStruct((n, D_INNER, D_STATE), f32)],
        scratch_shapes=[pltpu.VMEM((D_INNER, LANES), bf16), pltpu.VMEM((D_INNER, LANES), f32)],
        compiler_params=_cparams(1),
        name="ssd_decode",
    )(z, xbc, h1, jnp.swapaxes(conv_state, 0, 1), ssm_state, wdt, cw, cb, dtb, alog, dsk, nw, e)


def _attn_prompt_kernel(slope_ref, q_ref, k_ref, v_ref, o_ref, st_ref, kprev_ref, vprev_ref, *, dilation, rows):
    n = pl.program_id(2)
    blk = ATTN_BLOCK
    nb = rows // blk

    @pl.when(n == 0)
    def _():
        kprev_ref[...] = jnp.zeros_like(kprev_ref)
        vprev_ref[...] = jnp.zeros_like(vprev_ref)

    a = lax.broadcasted_iota(jnp.int32, (blk, blk), 0)
    c = lax.broadcasted_iota(jnp.int32, (blk, blk), 1)
    dist_prev = (dilation * (blk + a - c)).astype(f32)
    dist_cur = (dilation * (a - c)).astype(f32)
    valid_cur = c <= a
    valid_prev = c >= a
    valid_prev_first = (c - a + jnp.where(n > 0, 0, -2 * blk)) >= 0
    bias_cur, bias_prev, bias_prev_first = [], [], []
    for h in range(HEADS_PER_PATTERN):
        slope = slope_ref[h:h + 1, :]
        bias_cur.append(jnp.where(valid_cur, -slope * dist_cur, NEG_INF))
        bias_prev.append(jnp.where(valid_prev, -slope * dist_prev, NEG_INF))
        bias_prev_first.append(jnp.where(valid_prev_first, -slope * dist_prev, NEG_INF))

    qs, kcs, vcs, kps, vps, bcs, bps = [], [], [], [], [], [], []
    for i in range(nb):
        rs = slice(i * blk, (i + 1) * blk)
        ps = slice((i - 1) * blk, i * blk)
        for h in range(HEADS_PER_PATTERN):
            hs = slice(h * ATTN_HEAD_DIM, (h + 1) * ATTN_HEAD_DIM)
            qs.append(q_ref[rs, hs])
            kcs.append(k_ref[rs, hs])
            vcs.append(v_ref[rs, hs])
            kps.append(k_ref[ps, hs] if i > 0 else kprev_ref[:, hs])
            vps.append(v_ref[ps, hs] if i > 0 else vprev_ref[:, hs])
            bcs.append(bias_cur[h])
            bps.append(bias_prev[h] if i > 0 else bias_prev_first[h])
    q3 = jnp.stack(qs)
    scale = ATTN_HEAD_DIM ** -0.5
    qk = lambda x, y: jnp.einsum("bqd,bkd->bqk", x, y, preferred_element_type=f32)
    pv = lambda x, y: jnp.einsum("bqk,bkd->bqd", x, y, preferred_element_type=f32)
    s_p = qk(q3, jnp.stack(kps)) * scale + jnp.stack(bps)
    s_c = qk(q3, jnp.stack(kcs)) * scale + jnp.stack(bcs)
    m = jnp.maximum(jnp.max(s_p, axis=-1, keepdims=True), jnp.max(s_c, axis=-1, keepdims=True))
    p_p = jnp.exp(s_p - m)
    p_c = jnp.exp(s_c - m)
    l = jnp.sum(p_p, axis=-1, keepdims=True) + jnp.sum(p_c, axis=-1, keepdims=True)
    o = (pv(p_p.astype(bf16), jnp.stack(vps)) + pv(p_c.astype(bf16), jnp.stack(vcs))) / l
    lse = m + jnp.log(l)
    lane = lax.broadcasted_iota(jnp.int32, (blk, LANES), 1)
    for i in range(nb):
        rs = slice(i * blk, (i + 1) * blk)
        stat = jnp.zeros((blk, LANES), f32)
        for h in range(HEADS_PER_PATTERN):
            idx = i * HEADS_PER_PATTERN + h
            o_ref[rs, h * ATTN_HEAD_DIM:(h + 1) * ATTN_HEAD_DIM] = o[idx].astype(o_ref.dtype)
            stat = jnp.where(lane == h, lse[idx], stat)
        st_ref[rs, :] = stat
    kprev_ref[...] = k_ref[rows - blk:rows, :]
    vprev_ref[...] = v_ref[rows - blk:rows, :]


def _attn_prompt(qkv, slopes, g):
    window, dilation = ATTN_PATTERNS[g]
    assert window // dilation == ATTN_BLOCK
    batch, _, length, _ = qkv.shape
    rows = min(length, 512)
    spec = lambda width, off: pl.BlockSpec((None, None, rows, width), lambda b, r, n: (b, r, n, off))
    return pl.pallas_call(
        functools.partial(_attn_prompt_kernel, dilation=dilation, rows=rows),
        grid=(batch, dilation, length // rows),
        in_specs=[_resident(slopes.shape), spec(ATTN_WIDTH, 0), spec(ATTN_WIDTH, 1), spec(ATTN_WIDTH, 2)],
        out_specs=[spec(ATTN_WIDTH, 0), spec(LANES, 0)],
        out_shape=[jax.ShapeDtypeStruct((batch, dilation, length, ATTN_WIDTH), bf16),
                   jax.ShapeDtypeStruct((batch, dilation, length, LANES), f32)],
        scratch_shapes=[pltpu.VMEM((ATTN_BLOCK, ATTN_WIDTH), bf16), pltpu.VMEM((ATTN_BLOCK, ATTN_WIDTH), bf16)],
        compiler_params=_cparams(3),
        name=f"attn_prompt_d{dilation}",
    )(slopes, qkv, qkv, qkv)


def _attn_decode_kernel(bias_ref, qkv_ref, c0_ref, c1_ref, c2_ref, o_ref):
    scale = ATTN_HEAD_DIM ** -0.5
    outs, ms, ls = [], [], []
    for g, c_ref in enumerate((c0_ref, c1_ref, c2_ref)):
        q = qkv_ref[:, g, 0]
        k_new = qkv_ref[:, g, 1]
        v_new = qkv_ref[:, g, 2]
        s = jnp.sum(c_ref[:, :, 0] * q[:, None], axis=-1, keepdims=True) * scale + bias_ref[g]
        s_new = jnp.sum(k_new * q, axis=-1, keepdims=True) * scale
        m = jnp.maximum(jnp.max(s, axis=1), s_new)
        p = jnp.exp(s - m[:, None])
        p_new = jnp.exp(s_new - m)
        l = jnp.sum(p, axis=1) + p_new
        o = (jnp.sum(p * c_ref[:, :, 1], axis=1) + p_new * v_new) / l
        outs.append(o)
        ms.append(m)
        ls.append(l)
    m_all = jnp.maximum(jnp.maximum(ms[0], ms[1]), ms[2])
    wgts = [l * jnp.exp(m - m_all) for m, l in zip(ms, ls)]
    num = wgts[0] * outs[0] + wgts[1] * outs[1] + wgts[2] * outs[2]
    o_ref[...] = num / (wgts[0] + wgts[1] + wgts[2])


def _attn_decode(qkv, caches, bias, *, bb):
    n = qkv.shape[0]
    in_specs = [_resident(bias.shape),
                pl.BlockSpec((bb, N_PATTERNS, 3, HEADS_PER_PATTERN, ATTN_HEAD_DIM), lambda i: (i, 0, 0, 0, 0))]
    views = []
    for g, (window, dilation) in enumerate(ATTN_PATTERNS):
        assert caches[g].shape[1] == window and window // dilation == ATTN_BLOCK
        views.append(caches[g].reshape(n, ATTN_BLOCK, dilation, 2, HEADS_PER_PATTERN, ATTN_HEAD_DIM))
        in_specs.append(pl.BlockSpec((bb, ATTN_BLOCK, None, 2, HEADS_PER_PATTERN, ATTN_HEAD_DIM),
                                     lambda i: (i, 0, 0, 0, 0, 0)))
    return pl.pallas_call(
        _attn_decode_kernel,
        grid=(n // bb,),
        in_specs=in_specs,
        out_specs=pl.BlockSpec((bb, HEADS_PER_PATTERN, ATTN_HEAD_DIM), lambda i: (i, 0, 0)),
        out_shape=jax.ShapeDtypeStruct((n, HEADS_PER_PATTERN, ATTN_HEAD_DIM), f32),
        compiler_params=_cparams(1),
        name="attn_decode",
    )(bias, qkv, *views)


def _post_mix_kernel(*refs, dilations):
    n_pat = len(dilations)
    merged = n_pat > 1
    h1_ref, y_ref = refs[0], refs[1]
    o_refs = refs[2:2 + n_pat]
    st_refs = refs[2 + n_pat:2 + 2 * n_pat] if merged else ()
    k = 2 + n_pat + len(st_refs)
    wg_ref, wso_ref, wao_ref, wmo_ref, lnw_ref, lnb_ref, out_ref = refs[k:k + 7]
    scratch = refs[k + 7:]
    tm = h1_ref.shape[0]
    h1 = h1_ref[...]
    gates = jax.nn.sigmoid(_dot(h1.astype(bf16), wg_ref[...]))
    if merged:
        o_heads, lses = [], []
        for g, d in enumerate(dilations):
            if d == 1:
                o_heads.append([o_refs[g][:, h * ATTN_HEAD_DIM:(h + 1) * ATTN_HEAD_DIM].astype(f32)
                                for h in range(HEADS_PER_PATTERN)])
                lses.append(st_refs[g])
                continue
            o_scr, st_scr = scratch[0], scratch[1]
            scratch = scratch[2:]
            for r in range(d):
                st_scr[pl.ds(r, tm // d, stride=d), :] = st_refs[g][r]
                for h in range(HEADS_PER_PATTERN):
                    o_scr[h, pl.ds(r, tm // d, stride=d), :] = (
                        o_refs[g][r, :, h * ATTN_HEAD_DIM:(h + 1) * ATTN_HEAD_DIM].astype(f32))
            o_heads.append([o_scr[h] for h in range(HEADS_PER_PATTERN)])
            lses.append(st_scr)
        heads = []
        for h in range(HEADS_PER_PATTERN):
            lse = [st[:, h:h + 1] for st in lses]
            top = functools.reduce(jnp.maximum, lse)
            wgt = [jnp.exp(v - top) for v in lse]
            num = sum(w * o[h] for w, o in zip(wgt, o_heads))
            heads.append(num / sum(wgt))
        attn = jnp.concatenate(heads, axis=1)
    else:
        attn = o_refs[0][...]
    t_ssd = _dot(y_ref[...].astype(bf16), wso_ref[...])
    t_attn = _dot(attn.astype(bf16), wao_ref[...])
    u = gates[:, :D_MODEL] * t_ssd + gates[:, D_MODEL:] * t_attn
    mix = _dot(u.astype(bf16), wmo_ref[...])
    out_ref[...] = _layer_norm(DEEPNORM_ALPHA * h1 + mix, lnw_ref[...], lnb_ref[...])


def _post_mix(h1, y_ssd, outs, stats, dilations, wg, wso, wao, wmo, lnw, lnb, *, tm, seq, carry=None):
    m = h1.shape[0]
    steps = seq // tm
    row = lambda width: pl.BlockSpec((tm, width), lambda i: (i, 0))

    def pattern_spec(width, d):
        if d == 1:
            return row(width)
        return pl.BlockSpec((None, d, tm // d, width), lambda i: (i // steps, 0, i % steps, 0))

    in_specs = [row(D_MODEL), row(D_INNER)] + [pattern_spec(ATTN_WIDTH, d) for d in dilations]
    in_specs += [pattern_spec(LANES, d) for d in dilations[:len(stats)]]
    in_specs += [_resident(w.shape) for w in (wg, wso, wao, wmo, lnw, lnb)]
    scratch = []
    for d in dilations:
        if d > 1:
            scratch += [pltpu.VMEM((HEADS_PER_PATTERN, tm, ATTN_HEAD_DIM), f32), pltpu.VMEM((tm, LANES), f32)]
    return _call(functools.partial(_post_mix_kernel, dilations=tuple(dilations)), grid=(m // tm,), in_specs=in_specs,
                 out_specs=[row(D_MODEL)], out_shape=[jax.ShapeDtypeStruct((m, D_MODEL), f32)],
                 scratch_shapes=scratch, name="post_mix",
                 args=(h1, y_ssd, *outs, *stats, wg, wso, wao, wmo, lnw, lnb), carry=carry)


SHIFT_CHUNK = 256
SHIFT_BLOCK_BYTES = 8 * 1024 * 1024


def _kv_shift_kernel(c_ref, new_ref, o_ref, *, key_rows):
    bb, rows, _ = c_ref.shape
    keep = rows - key_rows
    n_chunks = pl.cdiv(keep, SHIFT_CHUNK)
    for b in range(bb):
        def body(k, carry):
            r0 = pl.multiple_of(jnp.minimum(k * SHIFT_CHUNK, keep - SHIFT_CHUNK), 8)
            o_ref[b, pl.ds(r0, SHIFT_CHUNK), :] = c_ref[b, pl.ds(r0 + key_rows, SHIFT_CHUNK), :]
            return carry
        lax.fori_loop(0, n_chunks, body, 0)
        o_ref[b, keep:rows, :] = new_ref[b]


def _kv_shift(cache, new):
    n, w = cache.shape[:2]
    key_rows = 2 * HEADS_PER_PATTERN * ATTN_HEAD_DIM // LANES
    rows = w * key_rows
    assert key_rows % 8 == 0 and rows - key_rows >= SHIFT_CHUNK
    bb = max(1, min(n, SHIFT_BLOCK_BYTES // (rows * LANES * 4)))
    assert n % bb == 0
    out = pl.pallas_call(
        functools.partial(_kv_shift_kernel, key_rows=key_rows),
        grid=(n // bb,),
        in_specs=[pl.BlockSpec((bb, rows, LANES), lambda i: (i, 0, 0)),
                  pl.BlockSpec((bb, key_rows, LANES), lambda i: (i, 0, 0))],
        out_specs=pl.BlockSpec((bb, rows, LANES), lambda i: (i, 0, 0)),
        out_shape=jax.ShapeDtypeStruct((n, rows, LANES), cache.dtype),
        compiler_params=_cparams(1),
        name=f"kv_shift_w{w}",
    )(cache.reshape(n, rows, LANES), new.reshape(n, key_rows, LANES))
    return out.reshape(cache.shape)


def _alibi_slopes():
    n = N_PATTERNS * HEADS_PER_PATTERN
    e = jnp.arange(1, n + 1, dtype=f32)
    return (2.0 ** (-ALIBI_MAX_EXP * e / n)).reshape(N_PATTERNS, HEADS_PER_PATTERN)


def _prepare_weights(ln_w, ln_b, ffn_w13, ffn_w2, w_in, conv_w, conv_b, dt_bias, a_log, d_skip, ssd_norm_w,
                     w_ssd_out, w_attn_out, w_mix_out, w_pe_gate, w_pe_proj):
    cuts = [0]
    for s in IN_SPLITS:
        cuts.append(cuts[-1] + s)
    wz, wxbc, wdt, wq, wk, wv = (w_in[:, cuts[i]:cuts[i + 1]] for i in range(6))
    pad_heads = LANES - N_SSD_HEADS
    head_of_col = jnp.arange(D_INNER, dtype=jnp.int32) // SSD_HEAD_DIM
    return dict(
        ln_w=[ln_w[i:i + 1] for i in range(3)], ln_b=[ln_b[i:i + 1] for i in range(3)],
        w13=[ffn_w13[i].astype(bf16) for i in range(2)], w2=[ffn_w2[i].astype(bf16) for i in range(2)],
        wz=wz.astype(bf16), wxbc=wxbc.astype(bf16),
        wdt=jnp.pad(wdt, ((0, 0), (0, pad_heads))).astype(bf16),
        wqkv=jnp.concatenate([m[:, g * ATTN_WIDTH:(g + 1) * ATTN_WIDTH] for g in range(N_PATTERNS)
                              for m in (wq, wk, wv)], axis=1).astype(bf16),
        wg=w_in[:, cuts[6]:].astype(bf16),
        cw=conv_w, cb=conv_b[None, :],
        dtb=jnp.pad(dt_bias, (0, pad_heads))[None, :], alog=jnp.pad(a_log, (0, pad_heads))[None, :],
        dsk=jnp.repeat(d_skip, SSD_HEAD_DIM)[None, :], nw=ssd_norm_w[None, :],
        expand=(jnp.arange(LANES, dtype=jnp.int32)[:, None] == head_of_col[None, :]).astype(f32),
        wso=w_ssd_out.astype(bf16), wao=w_attn_out.astype(bf16), wmo=w_mix_out.astype(bf16),
        wpg=w_pe_gate.astype(bf16), wpp=w_pe_proj.astype(bf16),
    )


def _ssd_args(w):
    return (w["wdt"], w["cw"], w["cb"], w["dtb"], w["alog"], w["dsk"], w["nw"], w["expand"])


class _CacheCarrier:
    def __init__(self, cache, new):
        n, w = cache.shape[:2]
        self.shape = cache.shape
        self.key_rows = 2 * HEADS_PER_PATTERN * ATTN_HEAD_DIM // LANES
        self.cache = cache.reshape(n, w * self.key_rows, LANES)
        self.new = new.reshape(n, self.key_rows, LANES)
        self.out, self.done = None, 0

    def take(self, steps, parts):
        samples = min(steps // parts, self.cache.shape[0] - self.done)
        if samples <= 0:
            return None
        return (self.cache, self.new, self.out, self.done * parts, samples * parts, parts)

    def gave(self, carry, out):
        if carry is not None:
            self.out, self.done = out, self.done + carry[4] // carry[5]

    def result(self):
        assert self.done == self.cache.shape[0]
        return self.out.reshape(self.shape)


def _prompt_layer(x, p, w, carrier):
    batch, seq, _ = x.shape
    assert seq % SSD_CHUNK == 0 and all(seq % (ATTN_BLOCK * d) == 0 and seq >= win for win, d in ATTN_PATTERNS)
    tm = 512
    steps = batch * seq // tm
    x2 = x.reshape(batch * seq, D_MODEL)

    def carried(parts, fn, *args, n_steps=steps, **kwargs):
        carry = carrier.take(n_steps, parts) if carrier is not None else None
        outs, cache_out = fn(*args, carry=carry, **kwargs)
        if carrier is not None:
            carrier.gave(carry, cache_out)
        return outs

    (h1,), _ = _ffn_ln(x2, w["w13"][0], w["w2"][0], w["ln_w"][0], w["ln_b"][0], tm=tm)
    z, xc, xbc_tail = carried(1, _proj_zx, h1, w["wz"], w["wxbc"], w["cw"], w["cb"], batch=batch, seq=seq, tm=tm)
    qkv = carried(1, _qkv_proj, h1, w["wqkv"], batch=batch, seq=seq, tm=tm)
    y_ssd, new_ssm = carried(SSD_CARRY_PARTS, _ssd_prompt, z, xc, h1, w["wdt"], w["dtb"], w["alog"], w["dsk"], w["nw"],
                             batch=batch, n_steps=batch * seq // SSD_CHUNK)
    slopes = _alibi_slopes()
    dilations = [d for _, d in ATTN_PATTERNS]
    outs, stats, new_kv = [], [], []
    for g, (win, d) in enumerate(ATTN_PATTERNS):
        o, st = _attn_prompt(qkv[g], jnp.broadcast_to(slopes[g][:, None], (HEADS_PER_PATTERN, LANES)), g)
        if d == 1:
            o, st = o.reshape(batch * seq, ATTN_WIDTH), st.reshape(batch * seq, LANES)
        outs.append(o)
        stats.append(st)
        tail = qkv[g][:, :, seq // d - win // d:, ATTN_WIDTH:]
        new_kv.append(jnp.swapaxes(tail, 1, 2).reshape(batch, win, 2, HEADS_PER_PATTERN, ATTN_HEAD_DIM).astype(f32))
    (h2,), _ = _post_mix(h1, y_ssd, outs, stats, dilations, w["wg"], w["wso"], w["wao"], w["wmo"], w["ln_w"][1],
                         w["ln_b"][1], tm=tm, seq=seq)
    y, = carried(1, _ffn_ln, h2, w["w13"][1], w["w2"][1], w["ln_w"][2], w["ln_b"][2],
                 pe=(p.reshape(batch * seq, PLE_DIM), w["wpg"], w["wpp"]), tm=tm)
    new_conv = xbc_tail[:, 8 - (CONV_WIDTH - 1):]
    return (y.reshape(batch, seq, D_MODEL), new_ssm.reshape(batch, N_SSD_HEADS, SSD_HEAD_DIM, D_STATE),
            new_conv, new_kv)


def _decode_layer(x, p, ssm_state, conv_state, caches, w):
    n, t, _ = x.shape
    assert t == 1
    tm = n
    bb = 8
    (h1,), _ = _ffn_ln(x.reshape(n, D_MODEL), w["w13"][0], w["w2"][0], w["ln_w"][0], w["ln_b"][0], tm=tm)
    z = _matmul(h1, w["wz"], f32, tm=tm)
    xbc = _matmul(h1, w["wxbc"], f32, tm=tm)
    qkv = _matmul(h1, w["wqkv"], f32, tm=tm)
    y_ssd, new_ssm = _ssd_decode(z, xbc, h1, conv_state, ssm_state.reshape(n, D_INNER, D_STATE), *_ssd_args(w), bb=bb)
    slopes = _alibi_slopes()
    steps_back = (ATTN_BLOCK - jnp.arange(ATTN_BLOCK, dtype=jnp.int32))
    bias = jnp.stack([-slopes[g][None, :] * (d * steps_back).astype(f32)[:, None]
                      for g, (_, d) in enumerate(ATTN_PATTERNS)])
    bias = jnp.broadcast_to(bias[..., None], bias.shape + (1,))
    qkv5 = qkv.reshape(n, N_PATTERNS, 3, HEADS_PER_PATTERN, ATTN_HEAD_DIM)
    attn = _attn_decode(qkv5, caches, bias, bb=bb).reshape(n, ATTN_WIDTH)
    (h2,), _ = _post_mix(h1, y_ssd, [attn], [], [1], w["wg"], w["wso"], w["wao"], w["wmo"], w["ln_w"][1],
                         w["ln_b"][1], tm=tm, seq=n)
    (y,), _ = _ffn_ln(h2, w["w13"][1], w["w2"][1], w["ln_w"][2], w["ln_b"][2],
                      pe=(p.reshape(n, PLE_DIM), w["wpg"], w["wpp"]), tm=tm)
    new_conv = jnp.concatenate([conv_state[:, 1:], xbc[:, None, :]], axis=1)
    new_rows = [qkv5[:, g, 1:3] for g in range(N_PATTERNS)]
    return y.reshape(n, 1, D_MODEL), new_ssm.reshape(n, N_SSD_HEADS, SSD_HEAD_DIM, D_STATE), new_conv, new_rows


SSD_CARRY_PARTS = 4


def _carrier_capacity(batch, seq):
    return 3 * (batch * seq // 512) + batch * seq // SSD_CHUNK // SSD_CARRY_PARTS


def kernel(x_prompt, x_sample, state_ssm, state_conv, cache_kv_w128, cache_kv_w512, cache_kv_w2048, p_prompt, p_sample, ln_w, ln_b, ffn_w13, ffn_w2, w_in, conv_w, conv_b, dt_bias, a_log, d_skip, ssd_norm_w, w_ssd_out, w_attn_out, w_mix_out, w_pe_gate, w_pe_proj):
    assert ln_w.shape[0] == DEPTH
    w = _prepare_weights(ln_w[0], ln_b[0], ffn_w13[0], ffn_w2[0], w_in[0], conv_w[0], conv_b[0], dt_bias[0], a_log[0],
                         d_skip[0], ssd_norm_w[0], w_ssd_out[0], w_attn_out[0], w_mix_out[0], w_pe_gate[0],
                         w_pe_proj[0])
    caches = (cache_kv_w128[0], cache_kv_w512[0], cache_kv_w2048[0])
    y_s, ssm_s, conv_s, new_rows = _decode_layer(x_sample, p_sample[0], state_ssm[0], state_conv[0], caches, w)
    big = max(range(N_PATTERNS), key=lambda g: caches[g].shape[1])
    carrier = None
    if caches[big].shape[0] <= _carrier_capacity(x_prompt.shape[0], x_prompt.shape[1]):
        carrier = _CacheCarrier(caches[big], new_rows[big])
    y_p, ssm_p, conv_p, kv_p = _prompt_layer(x_prompt, p_prompt[0], w, carrier)
    kv_s = [carrier.result() if carrier is not None and g == big else _kv_shift(caches[g], new_rows[g])
            for g in range(N_PATTERNS)]
    lead = lambda a: a[None]
    return (y_p, y_s, lead(ssm_p), lead(conv_p), lead(kv_p[0]), lead(kv_p[1]), lead(kv_p[2]),
            lead(ssm_s), lead(conv_s), lead(kv_s[0]), lead(kv_s[1]), lead(kv_s[2]))
```

```python
import functools

import jax
import jax.numpy as jnp
from jax import lax
from jax.experimental import pallas as pl
from jax.experimental.pallas import tpu as pltpu

f32 = jnp.float32
bf16 = jnp.bfloat16

D_MODEL = 1024
D_INNER = 2 * D_MODEL
SSD_HEAD_DIM = 64
N_SSD_HEADS = D_INNER // SSD_HEAD_DIM
N_SSD_GROUPS = 4
HEADS_PER_GROUP = N_SSD_HEADS // N_SSD_GROUPS
GROUP_WIDTH = HEADS_PER_GROUP * SSD_HEAD_DIM
D_STATE = 128
CONV_WIDTH = 4
CONV_DIM = D_INNER + 2 * N_SSD_GROUPS * D_STATE
SSD_CHUNK = 128
ATTN_PATTERNS = ((128, 1), (512, 4), (2048, 16))
N_PATTERNS = 3
HEADS_PER_PATTERN = 4
ATTN_HEAD_DIM = 128
ATTN_WIDTH = HEADS_PER_PATTERN * ATTN_HEAD_DIM
ATTN_QKV_WIDTH = N_PATTERNS * ATTN_WIDTH
PATTERN_QKV_WIDTH = 3 * ATTN_WIDTH
ATTN_BLOCK = 128
ALIBI_MAX_EXP = 8.0
D_FF = 2816
PLE_DIM = 256
LN_EPS = 1e-5
RMS_EPS = 1e-5
DEPTH = 1
DEEPNORM_ALPHA = (2.0 * DEPTH) ** 0.25
IN_SPLITS = (D_INNER, CONV_DIM, N_SSD_HEADS, ATTN_QKV_WIDTH, ATTN_QKV_WIDTH, ATTN_QKV_WIDTH)

LANES = 128
VMEM_LIMIT = 56 * 1024 * 1024
NEG_INF = float("-inf")


def _cparams(n_axes):
    return pltpu.CompilerParams(dimension_semantics=("arbitrary",) * n_axes, vmem_limit_bytes=VMEM_LIMIT)


def _resident(shape):
    nd = len(shape)
    return pl.BlockSpec(shape, lambda *_: (0,) * nd, pipeline_mode=pl.Buffered(1))


def _layer_norm(y, w, b):
    mu = jnp.mean(y, axis=-1, keepdims=True)
    yc = y - mu
    var = jnp.mean(yc * yc, axis=-1, keepdims=True)
    return yc * lax.rsqrt(var + LN_EPS) * w + b


def _silu(x):
    return x * jax.nn.sigmoid(x)


def _softplus(x):
    return jnp.maximum(x, 0.0) + jnp.log1p(jnp.exp(-jnp.abs(x)))


def _dot(a, b):
    return jnp.dot(a, b, preferred_element_type=f32)


def _dot_nt(a, b):
    return lax.dot_general(a, b, (((1,), (1,)), ((), ())), preferred_element_type=f32)


def _dot_tn(a, b):
    return lax.dot_general(a, b, (((0,), (0,)), ((), ())), preferred_element_type=f32)


def _dot_exact(a, b):
    return jnp.dot(a, b, precision=lax.Precision.HIGHEST, preferred_element_type=f32)


def _carry_where(chunk, base, parts, cache_ref, new_ref, out_ref, buf_ref, sems):
    c = base + chunk
    part = lax.rem(c, parts)
    return lax.rem(chunk, 2), c // parts, part, pl.multiple_of(part * buf_ref.shape[1], 8)


def _carry_in(chunk, op, *refs):
    _, parts, cache_ref, new_ref, _, buf_ref, sems = refs
    key_rows, part_rows = new_ref.shape[1], buf_ref.shape[1]
    slot, b, part, r0 = _carry_where(chunk, *refs)

    @pl.when(part < parts - 1)
    def _():
        op(pltpu.make_async_copy(cache_ref.at[b, pl.ds(r0 + key_rows, part_rows)], buf_ref.at[slot], sems.at[0, slot]))

    @pl.when(part == parts - 1)
    def _():
        op(pltpu.make_async_copy(cache_ref.at[b, pl.ds(r0 + key_rows, part_rows - key_rows)],
                                 buf_ref.at[slot, pl.ds(0, part_rows - key_rows)], sems.at[0, slot]))
        op(pltpu.make_async_copy(new_ref.at[b], buf_ref.at[slot, pl.ds(part_rows - key_rows, key_rows)],
                                 sems.at[1, slot]))


def _carry_out(chunk, op, *refs):
    _, _, _, _, out_ref, buf_ref, sems = refs
    slot, b, _, r0 = _carry_where(chunk, *refs)
    op(pltpu.make_async_copy(buf_ref.at[slot], out_ref.at[b, pl.ds(r0, buf_ref.shape[1])], sems.at[2, slot]))


def _start(copy):
    copy.start()


def _wait(copy):
    copy.wait()


def _carry_begin(step, n_chunks, *refs):
    @pl.when(step == 0)
    def _():
        _carry_in(step, _start, *refs)

    @pl.when(step < n_chunks)
    def _():
        _carry_in(step, _wait, *refs)
        _carry_out(step, _start, *refs)

        @pl.when(step > 0)
        def _():
            _carry_out(step - 1, _wait, *refs)

        @pl.when(step + 1 < n_chunks)
        def _():
            _carry_in(step + 1, _start, *refs)


def _carry_end(step, n_chunks, *refs):
    @pl.when(step == n_chunks - 1)
    def _():
        _carry_out(step, _wait, *refs)


def _call(body, *, grid, in_specs, out_specs, out_shape, scratch_shapes=(), name, args, carry=None):
    out_specs, out_shape = list(out_specs), list(out_shape)
    if carry is None:
        res = pl.pallas_call(body, grid=grid, in_specs=list(in_specs), out_specs=out_specs, out_shape=out_shape,
                             scratch_shapes=list(scratch_shapes), compiler_params=_cparams(len(grid)), name=name)(*args)
        return list(res), None
    cache, new, prev, base, count, parts = carry
    part_rows = cache.shape[1] // parts
    assert cache.shape[1] % parts == 0 and part_rows % 8 == 0 and part_rows > new.shape[1]
    assert len(grid) == 1 and 0 < count <= grid[0]
    n_in, n_out, n_scr = len(in_specs), len(out_specs), len(scratch_shapes)
    n_cin = 2 if prev is None else 3

    def kernel(*refs):
        ins = refs[:n_in]
        cache_ref, new_ref = refs[n_in:n_in + 2]
        outs = refs[n_in + n_cin:n_in + n_cin + n_out]
        out_cache_ref = refs[n_in + n_cin + n_out]
        scratch = refs[n_in + n_cin + n_out + 1:n_in + n_cin + n_out + 1 + n_scr]
        buf_ref, sems = refs[-2:]
        step = pl.program_id(0)
        carry_refs = (base, parts, cache_ref, new_ref, out_cache_ref, buf_ref, sems)
        _carry_begin(step, count, *carry_refs)
        body(*ins, *outs, *scratch)
        _carry_end(step, count, *carry_refs)

    any_spec = pl.BlockSpec(memory_space=pl.ANY)
    cargs = (cache, new) if prev is None else (cache, new, prev)
    res = pl.pallas_call(
        kernel, grid=grid,
        in_specs=list(in_specs) + [any_spec] * n_cin,
        out_specs=out_specs + [any_spec],
        out_shape=out_shape + [jax.ShapeDtypeStruct(cache.shape, cache.dtype)],
        scratch_shapes=list(scratch_shapes) + [pltpu.VMEM((2, part_rows, cache.shape[2]), cache.dtype),
                                               pltpu.SemaphoreType.DMA((3, 2))],
        input_output_aliases={} if prev is None else {n_in + 2: n_out},
        compiler_params=_cparams(1), name=name + "_carry")(*args, *cargs)
    return list(res[:n_out]), res[n_out]


FFN_CHUNKS = (512, 512, 512, 512, 512, 256)


def _ffn_ln_kernel(*refs, with_pe):
    if with_pe:
        x_ref, w13_ref, w2_ref, lnw_ref, lnb_ref, p_ref, wpg_ref, wpp_ref, o_ref = refs
    else:
        x_ref, w13_ref, w2_ref, lnw_ref, lnb_ref, o_ref = refs
    x = x_ref[...]
    xb = x.astype(bf16)
    acc = None
    c0 = 0
    for width in FFN_CHUNKS:
        gate = _dot(xb, w13_ref[:, c0:c0 + width])
        up = _dot(xb, w13_ref[:, D_FF + c0:D_FF + c0 + width])
        part = _dot((_silu(gate) * up).astype(bf16), w2_ref[c0:c0 + width, :])
        acc = part if acc is None else acc + part
        c0 += width
    h = _layer_norm(DEEPNORM_ALPHA * x + 0.5 * acc, lnw_ref[...], lnb_ref[...])
    if with_pe:
        pe_gate = jax.nn.sigmoid(_dot(h.astype(bf16), wpg_ref[...]))
        h = h + pe_gate * _dot(p_ref[...].astype(bf16), wpp_ref[...])
    o_ref[...] = h


def _ffn_ln(x, w13, w2, lnw, lnb, pe=None, *, tm, carry=None):
    m = x.shape[0]
    assert sum(FFN_CHUNKS) == D_FF
    row = lambda width: pl.BlockSpec((tm, width), lambda i: (i, 0))
    in_specs = [row(D_MODEL), _resident(w13.shape), _resident(w2.shape), _resident(lnw.shape), _resident(lnb.shape)]
    args = [x, w13, w2, lnw, lnb]
    if pe is not None:
        p, wpg, wpp = pe
        in_specs += [row(PLE_DIM), _resident(wpg.shape), _resident(wpp.shape)]
        args += [p, wpg, wpp]
    return _call(functools.partial(_ffn_ln_kernel, with_pe=pe is not None), grid=(m // tm,), in_specs=in_specs,
                 out_specs=[row(D_MODEL)], out_shape=[jax.ShapeDtypeStruct((m, D_MODEL), f32)],
                 name="ffn_ln_pe" if pe is not None else "ffn_ln", args=args, carry=carry)


def _matmul_kernel(x_ref, w_ref, o_ref):
    o_ref[...] = _dot(x_ref[...].astype(bf16), w_ref[...]).astype(o_ref.dtype)


def _matmul(x, w, out_dtype, *, tm):
    m, k = x.shape
    n = w.shape[1]
    return pl.pallas_call(
        _matmul_kernel,
        grid=(m // tm,),
        in_specs=[pl.BlockSpec((tm, k), lambda i: (i, 0)), _resident((k, n))],
        out_specs=pl.BlockSpec((tm, n), lambda i: (i, 0)),
        out_shape=jax.ShapeDtypeStruct((m, n), out_dtype),
        compiler_params=_cparams(1),
        name="proj_matmul",
    )(x, w)


def _qkv_proj_kernel(x_ref, w_ref, *rest):
    outs, res_ref = rest[:N_PATTERNS], rest[N_PATTERNS]
    tm = x_ref.shape[0]
    xb = x_ref[...].astype(bf16)
    wide = 2 * LANES
    for c2 in range(N_PATTERNS * PATTERN_QKV_WIDTH // wide):
        res = _dot(xb, w_ref[:, c2 * wide:(c2 + 1) * wide])
        res_ref[2 * c2] = res[:, :LANES]
        res_ref[2 * c2 + 1] = res[:, LANES:]
    blocks = PATTERN_QKV_WIDTH // LANES
    for g, (_, d) in enumerate(ATTN_PATTERNS):
        for cb in range(blocks):
            cs = slice(cb * LANES, (cb + 1) * LANES)
            for r in range(d):
                outs[g][r, :, cs] = res_ref[g * blocks + cb, pl.ds(r, tm // d, stride=d), :].astype(bf16)


def _qkv_proj(x, w, *, batch, seq, tm, carry=None):
    m, k = x.shape
    steps = seq // tm
    return _call(
        _qkv_proj_kernel,
        grid=(m // tm,),
        in_specs=[pl.BlockSpec((tm, k), lambda i: (i, 0)), _resident(w.shape)],
        out_specs=[pl.BlockSpec((None, d, tm // d, PATTERN_QKV_WIDTH), lambda i: (i // steps, 0, i % steps, 0))
                   for _, d in ATTN_PATTERNS],
        out_shape=[jax.ShapeDtypeStruct((batch, d, seq // d, PATTERN_QKV_WIDTH), bf16) for _, d in ATTN_PATTERNS],
        scratch_shapes=[pltpu.VMEM((N_PATTERNS * PATTERN_QKV_WIDTH // LANES, tm, LANES), f32)],
        name="qkv_proj", args=(x, w), carry=carry)


CONV_ROWS = 64


def _proj_zx_kernel(x_ref, wz_ref, wx_ref, cw_ref, cb_ref, z_ref, xc_ref, tail_ref, xpad_ref, *, steps):
    i = pl.program_id(0)
    tm = x_ref.shape[0]
    halo = 8
    xb = x_ref[...].astype(bf16)
    z_ref[...] = _dot(xb, wz_ref[...]).astype(z_ref.dtype)

    @pl.when(i % steps == 0)
    def _():
        xpad_ref[0:halo, :] = jnp.zeros((halo, CONV_DIM), f32)

    for j in range(CONV_DIM // 512):
        xpad_ref[halo:halo + tm, j * 512:(j + 1) * 512] = _dot(xb, wx_ref[:, j * 512:(j + 1) * 512])
        for cb in range(2 * j, 2 * j + 2):
            cs = slice(cb * 256, (cb + 1) * 256)
            taps = [cw_ref[w:w + 1, cs] for w in range(CONV_WIDTH)]
            bias = cb_ref[:, cs]
            for r0 in range(0, tm, CONV_ROWS):
                blk = xpad_ref[r0:r0 + halo + CONV_ROWS, cs]
                acc = bias + blk[halo:] * taps[CONV_WIDTH - 1]
                for k in range(1, CONV_WIDTH):
                    acc = acc + pltpu.roll(blk, k, 0)[halo:] * taps[CONV_WIDTH - 1 - k]
                xc_ref[r0:r0 + CONV_ROWS, cs] = _silu(acc).astype(xc_ref.dtype)
    tail_ref[...] = xpad_ref[tm:tm + halo, :]
    xpad_ref[0:halo, :] = xpad_ref[tm:tm + halo, :]


def _proj_zx(x, wz, wx, cw, cb, *, batch, seq, tm, carry=None):
    m, k = x.shape
    steps = seq // tm
    return _call(
        functools.partial(_proj_zx_kernel, steps=steps),
        grid=(m // tm,),
        in_specs=[pl.BlockSpec((tm, k), lambda i: (i, 0)), _resident(wz.shape), _resident(wx.shape),
                  _resident(cw.shape), _resident(cb.shape)],
        out_specs=[pl.BlockSpec((tm, D_INNER), lambda i: (i, 0)), pl.BlockSpec((tm, CONV_DIM), lambda i: (i, 0)),
                   pl.BlockSpec((None, 8, CONV_DIM), lambda i: (i // steps, 0, 0))],
        out_shape=[jax.ShapeDtypeStruct((m, D_INNER), bf16), jax.ShapeDtypeStruct((m, CONV_DIM), bf16),
                   jax.ShapeDtypeStruct((batch, 8, CONV_DIM), f32)],
        scratch_shapes=[pltpu.VMEM((tm + 8, CONV_DIM), f32)],
        name="proj_zx", args=(x, wz, wx, cw, cb), carry=carry)


def _split3(x):
    hi = x.astype(bf16)
    r = x - hi.astype(f32)
    mid = r.astype(bf16)
    return hi, mid, (r - mid.astype(f32)).astype(bf16)


def _ssd_prompt_kernel(z_ref, xc_ref, h1_ref, wdt_ref, dtb_ref, alog_ref, dsk_ref, nw_ref, y_ref, so_ref,
                       st_ref, dt_ref, cum_ref, cumt_ref, xw_ref, yacc_ref, *, nc):
    c = pl.program_id(0) % nc
    q = SSD_CHUNK

    @pl.when(c == 0)
    def _():
        st_ref[...] = jnp.zeros_like(st_ref)

    lane = lax.broadcasted_iota(jnp.int32, (q, LANES), 1)
    row = lax.broadcasted_iota(jnp.int32, (q, LANES), 0)
    dt = _softplus(_dot(h1_ref[...].astype(bf16), wdt_ref[...]) + dtb_ref[...])
    dt = jnp.where(lane < N_SSD_HEADS, dt, 0.0)
    d_a = dt * (-jnp.exp(alog_ref[...]))
    causal = row >= lane
    c3 = _dot(causal.astype(bf16), jnp.concatenate(_split3(d_a), axis=1))
    cum = c3[:, :LANES] + c3[:, LANES:2 * LANES] + c3[:, 2 * LANES:]
    dt_ref[...] = dt
    cum_ref[...] = cum
    cumt_ref[...] = cum.T

    lo_half = lane < SSD_HEAD_DIM
    lo_half_row = lo_half[0:1, :]
    for g in range(N_SSD_GROUPS):
        b_g = xc_ref[:, D_INNER + g * D_STATE:D_INNER + (g + 1) * D_STATE]
        c_g = xc_ref[:, D_INNER + (N_SSD_GROUPS + g) * D_STATE:D_INNER + (N_SSD_GROUPS + g + 1) * D_STATE]
        cb = _dot_nt(c_g, b_g)
        y_off = _dot(c_g, st_ref[g].astype(bf16))
        chunk_decay = []
        for jj in range(HEADS_PER_GROUP // 2):
            j = g * (HEADS_PER_GROUP // 2) + jj
            cs = slice(j * LANES, (j + 1) * LANES)
            ha, hb = 2 * j, 2 * j + 1
            x_p = xc_ref[:, cs].astype(f32)
            cum_a, cum_b = cum_ref[:, ha:ha + 1], cum_ref[:, hb:hb + 1]
            cume = jnp.where(lo_half, cum_a, cum_b)
            cume_last = jnp.where(lo_half_row, cum_a[q - 1:q, :], cum_b[q - 1:q, :])
            xdt = x_p * jnp.where(lo_half, dt_ref[:, ha:ha + 1], dt_ref[:, hb:hb + 1])
            ls = []
            for h, cum_h in ((ha, cum_a), (hb, cum_b)):
                seg = cum_h - cumt_ref[h:h + 1, :]
                ls.append(jnp.exp(jnp.where(causal, seg, NEG_INF)) * cb)
            l2 = jnp.concatenate(ls, axis=1).astype(bf16)
            x2 = jnp.concatenate([jnp.where(lo_half, xdt, 0.0), jnp.where(lo_half, 0.0, xdt)], axis=0).astype(bf16)
            y = _dot(l2, x2) + y_off[:, jj * LANES:(jj + 1) * LANES] * jnp.exp(cume) + dsk_ref[:, cs] * x_p
            yacc_ref[:, cs] = y * _silu(z_ref[:, cs].astype(f32))
            xw_ref[:, cs] = (xdt * jnp.exp(cume_last - cume)).astype(bf16)
            chunk_decay.append(jnp.exp(cume_last))
        gs = slice(g * GROUP_WIDTH, (g + 1) * GROUP_WIDTH)
        st_ref[g] = st_ref[g] * jnp.concatenate(chunk_decay, axis=1) + _dot_tn(b_g, xw_ref[:, gs])

    y = yacc_ref[...]
    y = y * lax.rsqrt(jnp.mean(y * y, axis=-1, keepdims=True) + RMS_EPS) * nw_ref[...]
    y_ref[...] = y.astype(y_ref.dtype)

    @pl.when(c == nc - 1)
    def _():
        for g in range(N_SSD_GROUPS):
            so_ref[g * GROUP_WIDTH:(g + 1) * GROUP_WIDTH, :] = st_ref[g].T


def _ssd_prompt(z, xc, h1, wdt, dtb, alog, dsk, nw, *, batch, carry=None):
    m = z.shape[0]
    nc = m // batch // SSD_CHUNK
    q = SSD_CHUNK
    row_map = lambda i: (i, 0)
    return _call(
        functools.partial(_ssd_prompt_kernel, nc=nc),
        grid=(batch * nc,),
        in_specs=[pl.BlockSpec((q, D_INNER), row_map), pl.BlockSpec((q, CONV_DIM), row_map),
                  pl.BlockSpec((q, D_MODEL), row_map),
                  _resident(wdt.shape), _resident(dtb.shape), _resident(alog.shape), _resident(dsk.shape),
                  _resident(nw.shape)],
        out_specs=[pl.BlockSpec((q, D_INNER), row_map),
                   pl.BlockSpec((None, D_INNER, D_STATE), lambda i: (i // nc, 0, 0))],
        out_shape=[jax.ShapeDtypeStruct((m, D_INNER), bf16),
                   jax.ShapeDtypeStruct((batch, D_INNER, D_STATE), f32)],
        scratch_shapes=[pltpu.VMEM((N_SSD_GROUPS, D_STATE, GROUP_WIDTH), f32),
                        pltpu.VMEM((q, LANES), f32), pltpu.VMEM((q, LANES), f32), pltpu.VMEM((LANES, q), f32),
                        pltpu.VMEM((q, D_INNER), bf16), pltpu.VMEM((q, D_INNER), f32)],
        name="ssd_prompt", args=(z, xc, h1, wdt, dtb, alog, dsk, nw), carry=carry)


def _ssd_decode_kernel(z_ref, xbc_ref, h1_ref, cst_ref, s_ref, wdt_ref, cw_ref, cb_ref, dtb_ref, alog_ref, dsk_ref,
                       nw_ref, e_ref, y_ref, so_ref, xdtt_ref, yt_ref, *, bb):
    acc = cb_ref[...] + xbc_ref[...] * cw_ref[CONV_WIDTH - 1:CONV_WIDTH, :]
    for w in range(CONV_WIDTH - 1):
        acc = acc + cst_ref[w] * cw_ref[w:w + 1, :]
    xc = _silu(acc)
    xs = xc[:, :D_INNER]
    lane_h = lax.broadcasted_iota(jnp.int32, (bb, LANES), 1)
    dt = _softplus(_dot(h1_ref[...].astype(bf16), wdt_ref[...]) + dtb_ref[...])
    dt = jnp.where(lane_h < N_SSD_HEADS, dt, 0.0)
    decay = jnp.exp(dt * (-jnp.exp(alog_ref[...])))
    xdt = xs * _dot_exact(dt, e_ref[...])
    pad = lambda a: jnp.concatenate([a, jnp.zeros((LANES - bb, a.shape[1]), f32)], axis=0)
    xdtt_ref[...] = pad(xdt).T.astype(bf16)
    row = lax.broadcasted_iota(jnp.int32, (LANES, LANES), 0)
    lane = lax.broadcasted_iota(jnp.int32, (LANES, LANES), 1)
    for g in range(N_SSD_GROUPS):
        gs = slice(g * GROUP_WIDTH, (g + 1) * GROUP_WIDTH)
        b_pad = pad(xc[:, D_INNER + g * D_STATE:D_INNER + (g + 1) * D_STATE])
        c_pad_t = pad(xc[:, D_INNER + (N_SSD_GROUPS + g) * D_STATE:D_INNER + (N_SSD_GROUPS + g + 1) * D_STATE]).T
        y_t = jnp.zeros((GROUP_WIDTH, LANES), f32)
        for j in range(bb):
            outer = _dot(xdtt_ref[gs, :], jnp.where(row == j, b_pad, 0.0).astype(bf16))
            pieces = []
            for hh in range(HEADS_PER_GROUP):
                h = g * HEADS_PER_GROUP + hh
                rs = slice(hh * SSD_HEAD_DIM, (hh + 1) * SSD_HEAD_DIM)
                new = s_ref[j, g * GROUP_WIDTH + hh * SSD_HEAD_DIM:g * GROUP_WIDTH + (hh + 1) * SSD_HEAD_DIM, :] \
                    * decay[j:j + 1, h:h + 1] + outer[rs, :]
                so_ref[j, g * GROUP_WIDTH + hh * SSD_HEAD_DIM:g * GROUP_WIDTH + (hh + 1) * SSD_HEAD_DIM, :] = new
                pieces.append(new.astype(bf16))
            y_t = y_t + _dot(jnp.concatenate(pieces, axis=0), jnp.where(lane == j, c_pad_t, 0.0).astype(bf16))
        yt_ref[gs, :] = y_t
    y = yt_ref[...].T[:bb, :] + dsk_ref[...] * xs
    y = y * _silu(z_ref[...])
    y = y * lax.rsqrt(jnp.mean(y * y, axis=-1, keepdims=True) + RMS_EPS) * nw_ref[...]
    y_ref[...] = y


def _ssd_decode(z, xbc, h1, conv_state, ssm_state, wdt, cw, cb, dtb, alog, dsk, nw, e, *, bb):
    n = z.shape[0]
    return pl.pallas_call(
        functools.partial(_ssd_decode_kernel, bb=bb),
        grid=(n // bb,),
        in_specs=[pl.BlockSpec((bb, D_INNER), lambda i: (i, 0)), pl.BlockSpec((bb, CONV_DIM), lambda i: (i, 0)),
                  pl.BlockSpec((bb, D_MODEL), lambda i: (i, 0)),
                  pl.BlockSpec((CONV_WIDTH - 1, bb, CONV_DIM), lambda i: (0, i, 0)),
                  pl.BlockSpec((bb, D_INNER, D_STATE), lambda i: (i, 0, 0)),
                  _resident(wdt.shape), _resident(cw.shape), _resident(cb.shape), _resident(dtb.shape),
                  _resident(alog.shape), _resident(dsk.shape), _resident(nw.shape), _resident(e.shape)],
        out_specs=[pl.BlockSpec((bb, D_INNER), lambda i: (i, 0)),
                   pl.BlockSpec((bb, D_INNER, D_STATE), lambda i: (i, 0, 0))],
        out_shape=[jax.ShapeDtypeStruct((n, D_INNER), f32),
                   jax.ShapeDtypeStruct((n, D_INNER, D_STATE), f32)],
        scratch_shapes=[pltpu.VMEM((D_INNER, LANES), bf16), pltpu.VMEM((D_INNER, LANES), f32)],
        compiler_params=_cparams(1),
        name="ssd_decode",
    )(z, xbc, h1, jnp.swapaxes(conv_state, 0, 1), ssm_state, wdt, cw, cb, dtb, alog, dsk, nw, e)


def _attn_prompt_kernel(slope_ref, q_ref, k_ref, v_ref, o_ref, st_ref, kprev_ref, vprev_ref, *, dilation, rows):
    n = pl.program_id(2)
    blk = ATTN_BLOCK
    nb = rows // blk

    @pl.when(n == 0)
    def _():
        kprev_ref[...] = jnp.zeros_like(kprev_ref)
        vprev_ref[...] = jnp.zeros_like(vprev_ref)

    a = lax.broadcasted_iota(jnp.int32, (blk, blk), 0)
    c = lax.broadcasted_iota(jnp.int32, (blk, blk), 1)
    dist_prev = (dilation * (blk + a - c)).astype(f32)
    dist_cur = (dilation * (a - c)).astype(f32)
    valid_cur = c <= a
    valid_prev = c >= a
    valid_prev_first = (c - a + jnp.where(n > 0, 0, -2 * blk)) >= 0
    bias_cur, bias_prev, bias_prev_first = [], [], []
    for h in range(HEADS_PER_PATTERN):
        slope = slope_ref[h:h + 1, :]
        bias_cur.append(jnp.where(valid_cur, -slope * dist_cur, NEG_INF))
        bias_prev.append(jnp.where(valid_prev, -slope * dist_prev, NEG_INF))
        bias_prev_first.append(jnp.where(valid_prev_first, -slope * dist_prev, NEG_INF))

    qs, kcs, vcs, kps, vps, bcs, bps = [], [], [], [], [], [], []
    for i in range(nb):
        rs = slice(i * blk, (i + 1) * blk)
        ps = slice((i - 1) * blk, i * blk)
        for h in range(HEADS_PER_PATTERN):
            hs = slice(h * ATTN_HEAD_DIM, (h + 1) * ATTN_HEAD_DIM)
            qs.append(q_ref[rs, hs])
            kcs.append(k_ref[rs, hs])
            vcs.append(v_ref[rs, hs])
            kps.append(k_ref[ps, hs] if i > 0 else kprev_ref[:, hs])
            vps.append(v_ref[ps, hs] if i > 0 else vprev_ref[:, hs])
            bcs.append(bias_cur[h])
            bps.append(bias_prev[h] if i > 0 else bias_prev_first[h])
    q3 = jnp.stack(qs)
    scale = ATTN_HEAD_DIM ** -0.5
    qk = lambda x, y: jnp.einsum("bqd,bkd->bqk", x, y, preferred_element_type=f32)
    pv = lambda x, y: jnp.einsum("bqk,bkd->bqd", x, y, preferred_element_type=f32)
    s_p = qk(q3, jnp.stack(kps)) * scale + jnp.stack(bps)
    s_c = qk(q3, jnp.stack(kcs)) * scale + jnp.stack(bcs)
    m = jnp.maximum(jnp.max(s_p, axis=-1, keepdims=True), jnp.max(s_c, axis=-1, keepdims=True))
    p_p = jnp.exp(s_p - m)
    p_c = jnp.exp(s_c - m)
    l = jnp.sum(p_p, axis=-1, keepdims=True) + jnp.sum(p_c, axis=-1, keepdims=True)
    o = (pv(p_p.astype(bf16), jnp.stack(vps)) + pv(p_c.astype(bf16), jnp.stack(vcs))) / l
    lse = m + jnp.log(l)
    lane = lax.broadcasted_iota(jnp.int32, (blk, LANES), 1)
    for i in range(nb):
        rs = slice(i * blk, (i + 1) * blk)
        stat = jnp.zeros((blk, LANES), f32)
        for h in range(HEADS_PER_PATTERN):
            idx = i * HEADS_PER_PATTERN + h
            o_ref[rs, h * ATTN_HEAD_DIM:(h + 1) * ATTN_HEAD_DIM] = o[idx].astype(o_ref.dtype)
            stat = jnp.where(lane == h, lse[idx], stat)
        st_ref[rs, :] = stat
    kprev_ref[...] = k_ref[rows - blk:rows, :]
    vprev_ref[...] = v_ref[rows - blk:rows, :]


def _attn_prompt(qkv, slopes, g):
    window, dilation = ATTN_PATTERNS[g]
    assert window // dilation == ATTN_BLOCK
    batch, _, length, _ = qkv.shape
    rows = min(length, 512)
    spec = lambda width, off: pl.BlockSpec((None, None, rows, width), lambda b, r, n: (b, r, n, off))
    return pl.pallas_call(
        functools.partial(_attn_prompt_kernel, dilation=dilation, rows=rows),
        grid=(batch, dilation, length // rows),
        in_specs=[_resident(slopes.shape), spec(ATTN_WIDTH, 0), spec(ATTN_WIDTH, 1), spec(ATTN_WIDTH, 2)],
        out_specs=[spec(ATTN_WIDTH, 0), spec(LANES, 0)],
        out_shape=[jax.ShapeDtypeStruct((batch, dilation, length, ATTN_WIDTH), bf16),
                   jax.ShapeDtypeStruct((batch, dilation, length, LANES), f32)],
        scratch_shapes=[pltpu.VMEM((ATTN_BLOCK, ATTN_WIDTH), bf16), pltpu.VMEM((ATTN_BLOCK, ATTN_WIDTH), bf16)],
        compiler_params=_cparams(3),
        name=f"attn_prompt_d{dilation}",
    )(slopes, qkv, qkv, qkv)


def _attn_decode_kernel(bias_ref, qkv_ref, c0_ref, c1_ref, c2_ref, o_ref):
    scale = ATTN_HEAD_DIM ** -0.5
    outs, ms, ls = [], [], []
    for g, c_ref in enumerate((c0_ref, c1_ref, c2_ref)):
        q = qkv_ref[:, g, 0]
        k_new = qkv_ref[:, g, 1]
        v_new = qkv_ref[:, g, 2]
        s = jnp.sum(c_ref[:, :, 0] * q[:, None], axis=-1, keepdims=True) * scale + bias_ref[g]
        s_new = jnp.sum(k_new * q, axis=-1, keepdims=True) * scale
        m = jnp.maximum(jnp.max(s, axis=1), s_new)
        p = jnp.exp(s - m[:, None])
        p_new = jnp.exp(s_new - m)
        l = jnp.sum(p, axis=1) + p_new
        o = (jnp.sum(p * c_ref[:, :, 1], axis=1) + p_new * v_new) / l
        outs.append(o)
        ms.append(m)
        ls.append(l)
    m_all = jnp.maximum(jnp.maximum(ms[0], ms[1]), ms[2])
    wgts = [l * jnp.exp(m - m_all) for m, l in zip(ms, ls)]
    num = wgts[0] * outs[0] + wgts[1] * outs[1] + wgts[2] * outs[2]
    o_ref[...] = num / (wgts[0] + wgts[1] + wgts[2])


def _attn_decode(qkv, caches, bias, *, bb):
    n = qkv.shape[0]
    in_specs = [_resident(bias.shape),
                pl.BlockSpec((bb, N_PATTERNS, 3, HEADS_PER_PATTERN, ATTN_HEAD_DIM), lambda i: (i, 0, 0, 0, 0))]
    views = []
    for g, (window, dilation) in enumerate(ATTN_PATTERNS):
        assert caches[g].shape[1] == window and window // dilation == ATTN_BLOCK
        views.append(caches[g].reshape(n, ATTN_BLOCK, dilation, 2, HEADS_PER_PATTERN, ATTN_HEAD_DIM))
        in_specs.append(pl.BlockSpec((bb, ATTN_BLOCK, None, 2, HEADS_PER_PATTERN, ATTN_HEAD_DIM),
                                     lambda i: (i, 0, 0, 0, 0, 0)))
    return pl.pallas_call(
        _attn_decode_kernel,
        grid=(n // bb,),
        in_specs=in_specs,
        out_specs=pl.BlockSpec((bb, HEADS_PER_PATTERN, ATTN_HEAD_DIM), lambda i: (i, 0, 0)),
        out_shape=jax.ShapeDtypeStruct((n, HEADS_PER_PATTERN, ATTN_HEAD_DIM), f32),
        compiler_params=_cparams(1),
        name="attn_decode",
    )(bias, qkv, *views)


def _post_mix_kernel(*refs, dilations):
    n_pat = len(dilations)
    merged = n_pat > 1
    h1_ref, y_ref = refs[0], refs[1]
    o_refs = refs[2:2 + n_pat]
    st_refs = refs[2 + n_pat:2 + 2 * n_pat] if merged else ()
    k = 2 + n_pat + len(st_refs)
    wg_ref, wso_ref, wao_ref, wmo_ref, lnw_ref, lnb_ref, out_ref = refs[k:k + 7]
    scratch = refs[k + 7:]
    tm = h1_ref.shape[0]
    h1 = h1_ref[...]
    gates = jax.nn.sigmoid(_dot(h1.astype(bf16), wg_ref[...]))
    if merged:
        o_heads, lses = [], []
        for g, d in enumerate(dilations):
            if d == 1:
                o_heads.append([o_refs[g][:, h * ATTN_HEAD_DIM:(h + 1) * ATTN_HEAD_DIM].astype(f32)
                                for h in range(HEADS_PER_PATTERN)])
                lses.append(st_refs[g])
                continue
            o_scr, st_scr = scratch[0], scratch[1]
            scratch = scratch[2:]
            for r in range(d):
                st_scr[pl.ds(r, tm // d, stride=d), :] = st_refs[g][r]
                for h in range(HEADS_PER_PATTERN):
                    o_scr[h, pl.ds(r, tm // d, stride=d), :] = (
                        o_refs[g][r, :, h * ATTN_HEAD_DIM:(h + 1) * ATTN_HEAD_DIM].astype(f32))
            o_heads.append([o_scr[h] for h in range(HEADS_PER_PATTERN)])
            lses.append(st_scr)
        heads = []
        for h in range(HEADS_PER_PATTERN):
            lse = [st[:, h:h + 1] for st in lses]
            top = functools.reduce(jnp.maximum, lse)
            wgt = [jnp.exp(v - top) for v in lse]
            num = sum(w * o[h] for w, o in zip(wgt, o_heads))
            heads.append(num / sum(wgt))
        attn = jnp.concatenate(heads, axis=1)
    else:
        attn = o_refs[0][...]
    t_ssd = _dot(y_ref[...].astype(bf16), wso_ref[...])
    t_attn = _dot(attn.astype(bf16), wao_ref[...])
    u = gates[:, :D_MODEL] * t_ssd + gates[:, D_MODEL:] * t_attn
    mix = _dot(u.astype(bf16), wmo_ref[...])
    out_ref[...] = _layer_norm(DEEPNORM_ALPHA * h1 + mix, lnw_ref[...], lnb_ref[...])


def _post_mix(h1, y_ssd, outs, stats, dilations, wg, wso, wao, wmo, lnw, lnb, *, tm, seq, carry=None):
    m = h1.shape[0]
    steps = seq // tm
    row = lambda width: pl.BlockSpec((tm, width), lambda i: (i, 0))

    def pattern_spec(width, d):
        if d == 1:
            return row(width)
        return pl.BlockSpec((None, d, tm // d, width), lambda i: (i // steps, 0, i % steps, 0))

    in_specs = [row(D_MODEL), row(D_INNER)] + [pattern_spec(ATTN_WIDTH, d) for d in dilations]
    in_specs += [pattern_spec(LANES, d) for d in dilations[:len(stats)]]
    in_specs += [_resident(w.shape) for w in (wg, wso, wao, wmo, lnw, lnb)]
    scratch = []
    for d in dilations:
        if d > 1:
            scratch += [pltpu.VMEM((HEADS_PER_PATTERN, tm, ATTN_HEAD_DIM), f32), pltpu.VMEM((tm, LANES), f32)]
    return _call(functools.partial(_post_mix_kernel, dilations=tuple(dilations)), grid=(m // tm,), in_specs=in_specs,
                 out_specs=[row(D_MODEL)], out_shape=[jax.ShapeDtypeStruct((m, D_MODEL), f32)],
                 scratch_shapes=scratch, name="post_mix",
                 args=(h1, y_ssd, *outs, *stats, wg, wso, wao, wmo, lnw, lnb), carry=carry)


SHIFT_CHUNK = 256
SHIFT_BLOCK_BYTES = 8 * 1024 * 1024


def _kv_shift_kernel(c_ref, new_ref, o_ref, *, key_rows):
    bb, rows, _ = c_ref.shape
    keep = rows - key_rows
    n_chunks = pl.cdiv(keep, SHIFT_CHUNK)
    for b in range(bb):
        def body(k, carry):
            r0 = pl.multiple_of(jnp.minimum(k * SHIFT_CHUNK, keep - SHIFT_CHUNK), 8)
            o_ref[b, pl.ds(r0, SHIFT_CHUNK), :] = c_ref[b, pl.ds(r0 + key_rows, SHIFT_CHUNK), :]
            return carry
        lax.fori_loop(0, n_chunks, body, 0)
        o_ref[b, keep:rows, :] = new_ref[b]


def _kv_shift(cache, new):
    n, w = cache.shape[:2]
    key_rows = 2 * HEADS_PER_PATTERN * ATTN_HEAD_DIM // LANES
    rows = w * key_rows
    assert key_rows % 8 == 0 and rows - key_rows >= SHIFT_CHUNK
    bb = max(1, min(n, SHIFT_BLOCK_BYTES // (rows * LANES * 4)))
    assert n % bb == 0
    out = pl.pallas_call(
        functools.partial(_kv_shift_kernel, key_rows=key_rows),
        grid=(n // bb,),
        in_specs=[pl.BlockSpec((bb, rows, LANES), lambda i: (i, 0, 0)),
                  pl.BlockSpec((bb, key_rows, LANES), lambda i: (i, 0, 0))],
        out_specs=pl.BlockSpec((bb, rows, LANES), lambda i: (i, 0, 0)),
        out_shape=jax.ShapeDtypeStruct((n, rows, LANES), cache.dtype),
        compiler_params=_cparams(1),
        name=f"kv_shift_w{w}",
    )(cache.reshape(n, rows, LANES), new.reshape(n, key_rows, LANES))
    return out.reshape(cache.shape)


def _alibi_slopes():
    n = N_PATTERNS * HEADS_PER_PATTERN
    e = jnp.arange(1, n + 1, dtype=f32)
    return (2.0 ** (-ALIBI_MAX_EXP * e / n)).reshape(N_PATTERNS, HEADS_PER_PATTERN)


def _prepare_weights(ln_w, ln_b, ffn_w13, ffn_w2, w_in, conv_w, conv_b, dt_bias, a_log, d_skip, ssd_norm_w,
                     w_ssd_out, w_attn_out, w_mix_out, w_pe_gate, w_pe_proj):
    cuts = [0]
    for s in IN_SPLITS:
        cuts.append(cuts[-1] + s)
    wz, wxbc, wdt, wq, wk, wv = (w_in[:, cuts[i]:cuts[i + 1]] for i in range(6))
    pad_heads = LANES - N_SSD_HEADS
    head_of_col = jnp.arange(D_INNER, dtype=jnp.int32) // SSD_HEAD_DIM
    return dict(
        ln_w=[ln_w[i:i + 1] for i in range(3)], ln_b=[ln_b[i:i + 1] for i in range(3)],
        w13=[ffn_w13[i].astype(bf16) for i in range(2)], w2=[ffn_w2[i].astype(bf16) for i in range(2)],
        wz=wz.astype(bf16), wxbc=wxbc.astype(bf16),
        wdt=jnp.pad(wdt, ((0, 0), (0, pad_heads))).astype(bf16),
        wqkv=jnp.concatenate([m[:, g * ATTN_WIDTH:(g + 1) * ATTN_WIDTH] for g in range(N_PATTERNS)
                              for m in (wq, wk, wv)], axis=1).astype(bf16),
        wg=w_in[:, cuts[6]:].astype(bf16),
        cw=conv_w, cb=conv_b[None, :],
        dtb=jnp.pad(dt_bias, (0, pad_heads))[None, :], alog=jnp.pad(a_log, (0, pad_heads))[None, :],
        dsk=jnp.repeat(d_skip, SSD_HEAD_DIM)[None, :], nw=ssd_norm_w[None, :],
        expand=(jnp.arange(LANES, dtype=jnp.int32)[:, None] == head_of_col[None, :]).astype(f32),
        wso=w_ssd_out.astype(bf16), wao=w_attn_out.astype(bf16), wmo=w_mix_out.astype(bf16),
        wpg=w_pe_gate.astype(bf16), wpp=w_pe_proj.astype(bf16),
    )


def _ssd_args(w):
    return (w["wdt"], w["cw"], w["cb"], w["dtb"], w["alog"], w["dsk"], w["nw"], w["expand"])


class _CacheCarrier:
    def __init__(self, cache, new):
        n, w = cache.shape[:2]
        self.shape = cache.shape
        self.key_rows = 2 * HEADS_PER_PATTERN * ATTN_HEAD_DIM // LANES
        self.cache = cache.reshape(n, w * self.key_rows, LANES)
        self.new = new.reshape(n, self.key_rows, LANES)
        self.out, self.done = None, 0

    def take(self, steps, parts):
        samples = min(steps // parts, self.cache.shape[0] - self.done)
        if samples <= 0:
            return None
        return (self.cache, self.new, self.out, self.done * parts, samples * parts, parts)

    def gave(self, carry, out):
        if carry is not None:
            self.out, self.done = out, self.done + carry[4] // carry[5]

    def result(self):
        assert self.done == self.cache.shape[0]
        return self.out.reshape(self.shape)


def _prompt_layer(x, p, w, big_carrier, mid_carrier):
    batch, seq, _ = x.shape
    assert seq % SSD_CHUNK == 0 and all(seq % (ATTN_BLOCK * d) == 0 and seq >= win for win, d in ATTN_PATTERNS)
    tm = 512
    x2 = x.reshape(batch * seq, D_MODEL)

    def carried(carrier, n_steps, parts, fn, *args, **kwargs):
        carry = carrier.take(n_steps, parts) if carrier is not None else None
        outs, cache_out = fn(*args, carry=carry, **kwargs)
        if carrier is not None:
            carrier.gave(carry, cache_out)
        return outs

    (h1,), _ = _ffn_ln(x2, w["w13"][0], w["w2"][0], w["ln_w"][0], w["ln_b"][0], tm=tm)
    z, xc, xbc_tail = carried(big_carrier, batch * seq // PROJ_ZX_ROWS, 1, _proj_zx, h1, w["wz"], w["wxbc"], w["cw"],
                              w["cb"], batch=batch, seq=seq, tm=PROJ_ZX_ROWS)
    qkv = carried(big_carrier, batch * seq // tm, 1, _qkv_proj, h1, w["wqkv"], batch=batch, seq=seq, tm=tm)
    y_ssd, new_ssm = carried(mid_carrier, batch * seq // SSD_CHUNK, 1, _ssd_prompt, z, xc, h1, w["wdt"], w["dtb"],
                             w["alog"], w["dsk"], w["nw"], batch=batch)
    slopes = _alibi_slopes()
    dilations = [d for _, d in ATTN_PATTERNS]
    outs, stats, new_kv = [], [], []
    for g, (win, d) in enumerate(ATTN_PATTERNS):
        o, st = _attn_prompt(qkv[g], jnp.broadcast_to(slopes[g][:, None], (HEADS_PER_PATTERN, LANES)), g)
        if d == 1:
            o, st = o.reshape(batch * seq, ATTN_WIDTH), st.reshape(batch * seq, LANES)
        outs.append(o)
        stats.append(st)
        tail = qkv[g][:, :, seq // d - win // d:, ATTN_WIDTH:]
        new_kv.append(jnp.swapaxes(tail, 1, 2).reshape(batch, win, 2, HEADS_PER_PATTERN, ATTN_HEAD_DIM).astype(f32))
    (h2,), _ = _post_mix(h1, y_ssd, outs, stats, dilations, w["wg"], w["wso"], w["wao"], w["wmo"], w["ln_w"][1],
                         w["ln_b"][1], tm=tm, seq=seq)
    y, = carried(big_carrier, batch * seq // tm, 1, _ffn_ln, h2, w["w13"][1], w["w2"][1], w["ln_w"][2], w["ln_b"][2],
                 pe=(p.reshape(batch * seq, PLE_DIM), w["wpg"], w["wpp"]), tm=tm)
    new_conv = xbc_tail[:, 8 - (CONV_WIDTH - 1):]
    return (y.reshape(batch, seq, D_MODEL), new_ssm.reshape(batch, N_SSD_HEADS, SSD_HEAD_DIM, D_STATE),
            new_conv, new_kv)


def _decode_layer(x, p, ssm_state, conv_state, caches, w):
    n, t, _ = x.shape
    assert t == 1
    tm = n
    bb = 8
    (h1,), _ = _ffn_ln(x.reshape(n, D_MODEL), w["w13"][0], w["w2"][0], w["ln_w"][0], w["ln_b"][0], tm=tm)
    z = _matmul(h1, w["wz"], f32, tm=tm)
    xbc = _matmul(h1, w["wxbc"], f32, tm=tm)
    qkv = _matmul(h1, w["wqkv"], f32, tm=tm)
    y_ssd, new_ssm = _ssd_decode(z, xbc, h1, conv_state, ssm_state.reshape(n, D_INNER, D_STATE), *_ssd_args(w), bb=bb)
    slopes = _alibi_slopes()
    steps_back = (ATTN_BLOCK - jnp.arange(ATTN_BLOCK, dtype=jnp.int32))
    bias = jnp.stack([-slopes[g][None, :] * (d * steps_back).astype(f32)[:, None]
                      for g, (_, d) in enumerate(ATTN_PATTERNS)])
    bias = jnp.broadcast_to(bias[..., None], bias.shape + (1,))
    qkv5 = qkv.reshape(n, N_PATTERNS, 3, HEADS_PER_PATTERN, ATTN_HEAD_DIM)
    attn = _attn_decode(qkv5, caches, bias, bb=bb).reshape(n, ATTN_WIDTH)
    (h2,), _ = _post_mix(h1, y_ssd, [attn], [], [1], w["wg"], w["wso"], w["wao"], w["wmo"], w["ln_w"][1],
                         w["ln_b"][1], tm=tm, seq=n)
    (y,), _ = _ffn_ln(h2, w["w13"][1], w["w2"][1], w["ln_w"][2], w["ln_b"][2],
                      pe=(p.reshape(n, PLE_DIM), w["wpg"], w["wpp"]), tm=tm)
    new_conv = jnp.concatenate([conv_state[:, 1:], xbc[:, None, :]], axis=1)
    new_rows = [qkv5[:, g, 1:3] for g in range(N_PATTERNS)]
    return y.reshape(n, 1, D_MODEL), new_ssm.reshape(n, N_SSD_HEADS, SSD_HEAD_DIM, D_STATE), new_conv, new_rows


PROJ_ZX_ROWS = 256


def _carrier_capacity(batch, seq):
    return batch * seq // PROJ_ZX_ROWS + 2 * (batch * seq // 512), batch * seq // SSD_CHUNK


def kernel(x_prompt, x_sample, state_ssm, state_conv, cache_kv_w128, cache_kv_w512, cache_kv_w2048, p_prompt, p_sample, ln_w, ln_b, ffn_w13, ffn_w2, w_in, conv_w, conv_b, dt_bias, a_log, d_skip, ssd_norm_w, w_ssd_out, w_attn_out, w_mix_out, w_pe_gate, w_pe_proj):
    assert ln_w.shape[0] == DEPTH
    w = _prepare_weights(ln_w[0], ln_b[0], ffn_w13[0], ffn_w2[0], w_in[0], conv_w[0], conv_b[0], dt_bias[0], a_log[0],
                         d_skip[0], ssd_norm_w[0], w_ssd_out[0], w_attn_out[0], w_mix_out[0], w_pe_gate[0],
                         w_pe_proj[0])
    caches = (cache_kv_w128[0], cache_kv_w512[0], cache_kv_w2048[0])
    y_s, ssm_s, conv_s, new_rows = _decode_layer(x_sample, p_sample[0], state_ssm[0], state_conv[0], caches, w)
    order = sorted(range(N_PATTERNS), key=lambda g: -caches[g].shape[1])
    big, mid = order[0], order[1]
    cap_big, cap_mid = _carrier_capacity(x_prompt.shape[0], x_prompt.shape[1])
    big_carrier = _CacheCarrier(caches[big], new_rows[big]) if caches[big].shape[0] <= cap_big else None
    mid_carrier = _CacheCarrier(caches[mid], new_rows[mid]) if caches[mid].shape[0] <= cap_mid else None
    y_p, ssm_p, conv_p, kv_p = _prompt_layer(x_prompt, p_prompt[0], w, big_carrier, mid_carrier)
    carriers = {big: big_carrier, mid: mid_carrier}
    kv_s = [carriers[g].result() if carriers.get(g) is not None else _kv_shift(caches[g], new_rows[g])
            for g in range(N_PATTERNS)]
    lead = lambda a: a[None]
    return (y_p, y_s, lead(ssm_p), lead(conv_p), lead(kv_p[0]), lead(kv_p[1]), lead(kv_p[2]),
            lead(ssm_s), lead(conv_s), lead(kv_s[0]), lead(kv_s[1]), lead(kv_s[2]))
```

```python
import functools

import jax
import jax.numpy as jnp
from jax import lax
from jax.experimental import pallas as pl
from jax.experimental.pallas import tpu as pltpu

f32 = jnp.float32
bf16 = jnp.bfloat16

D_MODEL = 1024
D_INNER = 2 * D_MODEL
SSD_HEAD_DIM = 64
N_SSD_HEADS = D_INNER // SSD_HEAD_DIM
N_SSD_GROUPS = 4
HEADS_PER_GROUP = N_SSD_HEADS // N_SSD_GROUPS
GROUP_WIDTH = HEADS_PER_GROUP * SSD_HEAD_DIM
D_STATE = 128
CONV_WIDTH = 4
CONV_DIM = D_INNER + 2 * N_SSD_GROUPS * D_STATE
SSD_CHUNK = 128
ATTN_PATTERNS = ((128, 1), (512, 4), (2048, 16))
N_PATTERNS = 3
HEADS_PER_PATTERN = 4
ATTN_HEAD_DIM = 128
ATTN_WIDTH = HEADS_PER_PATTERN * ATTN_HEAD_DIM
ATTN_QKV_WIDTH = N_PATTERNS * ATTN_WIDTH
PATTERN_QKV_WIDTH = 3 * ATTN_WIDTH
ATTN_BLOCK = 128
ALIBI_MAX_EXP = 8.0
D_FF = 2816
PLE_DIM = 256
LN_EPS = 1e-5
RMS_EPS = 1e-5
DEPTH = 1
DEEPNORM_ALPHA = (2.0 * DEPTH) ** 0.25
IN_SPLITS = (D_INNER, CONV_DIM, N_SSD_HEADS, ATTN_QKV_WIDTH, ATTN_QKV_WIDTH, ATTN_QKV_WIDTH)

LANES = 128
VMEM_LIMIT = 56 * 1024 * 1024
NEG_INF = float("-inf")


def _cparams(n_axes):
    return pltpu.CompilerParams(dimension_semantics=("arbitrary",) * n_axes, vmem_limit_bytes=VMEM_LIMIT)


def _resident(shape):
    nd = len(shape)
    return pl.BlockSpec(shape, lambda *_: (0,) * nd, pipeline_mode=pl.Buffered(1))


def _layer_norm(y, w, b):
    mu = jnp.mean(y, axis=-1, keepdims=True)
    yc = y - mu
    var = jnp.mean(yc * yc, axis=-1, keepdims=True)
    return yc * lax.rsqrt(var + LN_EPS) * w + b


def _silu(x):
    return x * jax.nn.sigmoid(x)


def _softplus(x):
    return jnp.maximum(x, 0.0) + jnp.log1p(jnp.exp(-jnp.abs(x)))


def _dot(a, b):
    return jnp.dot(a, b, preferred_element_type=f32)


def _dot_nt(a, b):
    return lax.dot_general(a, b, (((1,), (1,)), ((), ())), preferred_element_type=f32)


def _dot_tn(a, b):
    return lax.dot_general(a, b, (((0,), (0,)), ((), ())), preferred_element_type=f32)


def _dot_exact(a, b):
    return jnp.dot(a, b, precision=lax.Precision.HIGHEST, preferred_element_type=f32)


def _carry_where(chunk, base, parts, cache_ref, new_ref, out_ref, buf_ref, sems):
    c = base + chunk
    part = lax.rem(c, parts)
    return lax.rem(chunk, 2), c // parts, part, pl.multiple_of(part * buf_ref.shape[1], 8)


def _carry_in(chunk, op, *refs):
    _, parts, cache_ref, new_ref, _, buf_ref, sems = refs
    key_rows, part_rows = new_ref.shape[1], buf_ref.shape[1]
    slot, b, part, r0 = _carry_where(chunk, *refs)

    @pl.when(part < parts - 1)
    def _():
        op(pltpu.make_async_copy(cache_ref.at[b, pl.ds(r0 + key_rows, part_rows)], buf_ref.at[slot], sems.at[0, slot]))

    @pl.when(part == parts - 1)
    def _():
        op(pltpu.make_async_copy(cache_ref.at[b, pl.ds(r0 + key_rows, part_rows - key_rows)],
                                 buf_ref.at[slot, pl.ds(0, part_rows - key_rows)], sems.at[0, slot]))
        op(pltpu.make_async_copy(new_ref.at[b], buf_ref.at[slot, pl.ds(part_rows - key_rows, key_rows)],
                                 sems.at[1, slot]))


def _carry_out(chunk, op, *refs):
    _, _, _, _, out_ref, buf_ref, sems = refs
    slot, b, _, r0 = _carry_where(chunk, *refs)
    op(pltpu.make_async_copy(buf_ref.at[slot], out_ref.at[b, pl.ds(r0, buf_ref.shape[1])], sems.at[2, slot]))


def _start(copy):
    copy.start()


def _wait(copy):
    copy.wait()


def _carry_begin(step, n_chunks, *refs):
    @pl.when(step == 0)
    def _():
        _carry_in(step, _start, *refs)

    @pl.when(step < n_chunks)
    def _():
        _carry_in(step, _wait, *refs)
        _carry_out(step, _start, *refs)

        @pl.when(step > 0)
        def _():
            _carry_out(step - 1, _wait, *refs)

        @pl.when(step + 1 < n_chunks)
        def _():
            _carry_in(step + 1, _start, *refs)


def _carry_end(step, n_chunks, *refs):
    @pl.when(step == n_chunks - 1)
    def _():
        _carry_out(step, _wait, *refs)


def _call(body, *, grid, in_specs, out_specs, out_shape, scratch_shapes=(), name, args, carry=None):
    out_specs, out_shape = list(out_specs), list(out_shape)
    if carry is None:
        res = pl.pallas_call(body, grid=grid, in_specs=list(in_specs), out_specs=out_specs, out_shape=out_shape,
                             scratch_shapes=list(scratch_shapes), compiler_params=_cparams(len(grid)), name=name)(*args)
        return list(res), None
    cache, new, prev, base, count, parts = carry
    part_rows = cache.shape[1] // parts
    assert cache.shape[1] % parts == 0 and part_rows % 8 == 0 and part_rows > new.shape[1]
    assert len(grid) == 1 and 0 < count <= grid[0]
    n_in, n_out, n_scr = len(in_specs), len(out_specs), len(scratch_shapes)
    n_cin = 2 if prev is None else 3

    def kernel(*refs):
        ins = refs[:n_in]
        cache_ref, new_ref = refs[n_in:n_in + 2]
        outs = refs[n_in + n_cin:n_in + n_cin + n_out]
        out_cache_ref = refs[n_in + n_cin + n_out]
        scratch = refs[n_in + n_cin + n_out + 1:n_in + n_cin + n_out + 1 + n_scr]
        buf_ref, sems = refs[-2:]
        step = pl.program_id(0)
        carry_refs = (base, parts, cache_ref, new_ref, out_cache_ref, buf_ref, sems)
        _carry_begin(step, count, *carry_refs)
        body(*ins, *outs, *scratch)
        _carry_end(step, count, *carry_refs)

    any_spec = pl.BlockSpec(memory_space=pl.ANY)
    cargs = (cache, new) if prev is None else (cache, new, prev)
    res = pl.pallas_call(
        kernel, grid=grid,
        in_specs=list(in_specs) + [any_spec] * n_cin,
        out_specs=out_specs + [any_spec],
        out_shape=out_shape + [jax.ShapeDtypeStruct(cache.shape, cache.dtype)],
        scratch_shapes=list(scratch_shapes) + [pltpu.VMEM((2, part_rows, cache.shape[2]), cache.dtype),
                                               pltpu.SemaphoreType.DMA((3, 2))],
        input_output_aliases={} if prev is None else {n_in + 2: n_out},
        compiler_params=_cparams(1), name=name + "_carry")(*args, *cargs)
    return list(res[:n_out]), res[n_out]


FFN_CHUNKS = (512, 512, 512, 512, 512, 256)


def _ffn_ln_kernel(*refs, with_pe):
    if with_pe:
        x_ref, w13_ref, w2_ref, lnw_ref, lnb_ref, p_ref, wpg_ref, wpp_ref, o_ref = refs
    else:
        x_ref, w13_ref, w2_ref, lnw_ref, lnb_ref, o_ref = refs
    x = x_ref[...]
    xb = x.astype(bf16)
    acc = None
    c0 = 0
    for width in FFN_CHUNKS:
        gate = _dot(xb, w13_ref[:, c0:c0 + width])
        up = _dot(xb, w13_ref[:, D_FF + c0:D_FF + c0 + width])
        part = _dot((_silu(gate) * up).astype(bf16), w2_ref[c0:c0 + width, :])
        acc = part if acc is None else acc + part
        c0 += width
    h = _layer_norm(DEEPNORM_ALPHA * x + 0.5 * acc, lnw_ref[...], lnb_ref[...])
    if with_pe:
        pe_gate = jax.nn.sigmoid(_dot(h.astype(bf16), wpg_ref[...]))
        h = h + pe_gate * _dot(p_ref[...].astype(bf16), wpp_ref[...])
    o_ref[...] = h


def _ffn_ln(x, w13, w2, lnw, lnb, pe=None, *, tm, carry=None):
    m = x.shape[0]
    assert sum(FFN_CHUNKS) == D_FF
    row = lambda width: pl.BlockSpec((tm, width), lambda i: (i, 0))
    in_specs = [row(D_MODEL), _resident(w13.shape), _resident(w2.shape), _resident(lnw.shape), _resident(lnb.shape)]
    args = [x, w13, w2, lnw, lnb]
    if pe is not None:
        p, wpg, wpp = pe
        in_specs += [row(PLE_DIM), _resident(wpg.shape), _resident(wpp.shape)]
        args += [p, wpg, wpp]
    return _call(functools.partial(_ffn_ln_kernel, with_pe=pe is not None), grid=(m // tm,), in_specs=in_specs,
                 out_specs=[row(D_MODEL)], out_shape=[jax.ShapeDtypeStruct((m, D_MODEL), f32)],
                 name="ffn_ln_pe" if pe is not None else "ffn_ln", args=args, carry=carry)


def _matmul_kernel(x_ref, w_ref, o_ref):
    o_ref[...] = _dot(x_ref[...].astype(bf16), w_ref[...]).astype(o_ref.dtype)


def _matmul(x, w, out_dtype, *, tm):
    m, k = x.shape
    n = w.shape[1]
    return pl.pallas_call(
        _matmul_kernel,
        grid=(m // tm,),
        in_specs=[pl.BlockSpec((tm, k), lambda i: (i, 0)), _resident((k, n))],
        out_specs=pl.BlockSpec((tm, n), lambda i: (i, 0)),
        out_shape=jax.ShapeDtypeStruct((m, n), out_dtype),
        compiler_params=_cparams(1),
        name="proj_matmul",
    )(x, w)


def _qkv_proj_kernel(x_ref, w_ref, *rest):
    outs, res_ref = rest[:N_PATTERNS], rest[N_PATTERNS]
    tm = x_ref.shape[0]
    xb = x_ref[...].astype(bf16)
    wide = 2 * LANES
    for c2 in range(N_PATTERNS * PATTERN_QKV_WIDTH // wide):
        res = _dot(xb, w_ref[:, c2 * wide:(c2 + 1) * wide])
        res_ref[2 * c2] = res[:, :LANES]
        res_ref[2 * c2 + 1] = res[:, LANES:]
    blocks = PATTERN_QKV_WIDTH // LANES
    for g, (_, d) in enumerate(ATTN_PATTERNS):
        for cb in range(blocks):
            cs = slice(cb * LANES, (cb + 1) * LANES)
            for r in range(d):
                outs[g][r, :, cs] = res_ref[g * blocks + cb, pl.ds(r, tm // d, stride=d), :].astype(bf16)


def _qkv_proj(x, w, *, batch, seq, tm, carry=None):
    m, k = x.shape
    steps = seq // tm
    return _call(
        _qkv_proj_kernel,
        grid=(m // tm,),
        in_specs=[pl.BlockSpec((tm, k), lambda i: (i, 0)), _resident(w.shape)],
        out_specs=[pl.BlockSpec((None, d, tm // d, PATTERN_QKV_WIDTH), lambda i: (i // steps, 0, i % steps, 0))
                   for _, d in ATTN_PATTERNS],
        out_shape=[jax.ShapeDtypeStruct((batch, d, seq // d, PATTERN_QKV_WIDTH), bf16) for _, d in ATTN_PATTERNS],
        scratch_shapes=[pltpu.VMEM((N_PATTERNS * PATTERN_QKV_WIDTH // LANES, tm, LANES), f32)],
        name="qkv_proj", args=(x, w), carry=carry)


CONV_ROWS = 64


def _proj_zx_kernel(x_ref, wz_ref, wx_ref, cw_ref, cb_ref, z_ref, xc_ref, tail_ref, xpad_ref, *, steps):
    i = pl.program_id(0)
    tm = x_ref.shape[0]
    halo = 8
    xb = x_ref[...].astype(bf16)
    z_ref[...] = _dot(xb, wz_ref[...]).astype(z_ref.dtype)

    @pl.when(i % steps == 0)
    def _():
        xpad_ref[0:halo, :] = jnp.zeros((halo, CONV_DIM), f32)

    for j in range(CONV_DIM // 512):
        xpad_ref[halo:halo + tm, j * 512:(j + 1) * 512] = _dot(xb, wx_ref[:, j * 512:(j + 1) * 512])
        for cb in range(2 * j, 2 * j + 2):
            cs = slice(cb * 256, (cb + 1) * 256)
            taps = [cw_ref[w:w + 1, cs] for w in range(CONV_WIDTH)]
            bias = cb_ref[:, cs]
            for r0 in range(0, tm, CONV_ROWS):
                blk = xpad_ref[r0:r0 + halo + CONV_ROWS, cs]
                acc = bias + blk[halo:] * taps[CONV_WIDTH - 1]
                for k in range(1, CONV_WIDTH):
                    acc = acc + pltpu.roll(blk, k, 0)[halo:] * taps[CONV_WIDTH - 1 - k]
                xc_ref[r0:r0 + CONV_ROWS, cs] = _silu(acc).astype(xc_ref.dtype)
    tail_ref[...] = xpad_ref[tm:tm + halo, :]
    xpad_ref[0:halo, :] = xpad_ref[tm:tm + halo, :]


def _proj_zx(x, wz, wx, cw, cb, *, batch, seq, tm, carry=None):
    m, k = x.shape
    steps = seq // tm
    return _call(
        functools.partial(_proj_zx_kernel, steps=steps),
        grid=(m // tm,),
        in_specs=[pl.BlockSpec((tm, k), lambda i: (i, 0)), _resident(wz.shape), _resident(wx.shape),
                  _resident(cw.shape), _resident(cb.shape)],
        out_specs=[pl.BlockSpec((tm, D_INNER), lambda i: (i, 0)), pl.BlockSpec((tm, CONV_DIM), lambda i: (i, 0)),
                   pl.BlockSpec((None, 8, CONV_DIM), lambda i: (i // steps, 0, 0))],
        out_shape=[jax.ShapeDtypeStruct((m, D_INNER), bf16), jax.ShapeDtypeStruct((m, CONV_DIM), bf16),
                   jax.ShapeDtypeStruct((batch, 8, CONV_DIM), f32)],
        scratch_shapes=[pltpu.VMEM((tm + 8, CONV_DIM), f32)],
        name="proj_zx", args=(x, wz, wx, cw, cb), carry=carry)


def _split3(x):
    hi = x.astype(bf16)
    r = x - hi.astype(f32)
    mid = r.astype(bf16)
    return hi, mid, (r - mid.astype(f32)).astype(bf16)


def _ssd_prompt_kernel(z_ref, xc_ref, h1_ref, wdt_ref, dtb_ref, alog_ref, dsk_ref, nw_ref, y_ref, so_ref,
                       st_ref, dt_ref, cum_ref, cumt_ref, xw_ref, yacc_ref, *, nc):
    c = pl.program_id(0) % nc
    q = SSD_CHUNK

    @pl.when(c == 0)
    def _():
        st_ref[...] = jnp.zeros_like(st_ref)

    lane = lax.broadcasted_iota(jnp.int32, (q, LANES), 1)
    row = lax.broadcasted_iota(jnp.int32, (q, LANES), 0)
    dt = _softplus(_dot(h1_ref[...].astype(bf16), wdt_ref[...]) + dtb_ref[...])
    dt = jnp.where(lane < N_SSD_HEADS, dt, 0.0)
    d_a = dt * (-jnp.exp(alog_ref[...]))
    causal = row >= lane
    c3 = _dot(causal.astype(bf16), jnp.concatenate(_split3(d_a), axis=1))
    cum = c3[:, :LANES] + c3[:, LANES:2 * LANES] + c3[:, 2 * LANES:]
    dt_ref[...] = dt
    cum_ref[...] = cum
    cumt_ref[...] = cum.T

    lo_half = lane < SSD_HEAD_DIM
    lo_half_row = lo_half[0:1, :]
    for g in range(N_SSD_GROUPS):
        b_g = xc_ref[:, D_INNER + g * D_STATE:D_INNER + (g + 1) * D_STATE]
        c_g = xc_ref[:, D_INNER + (N_SSD_GROUPS + g) * D_STATE:D_INNER + (N_SSD_GROUPS + g + 1) * D_STATE]
        cb = _dot_nt(c_g, b_g)
        y_off = _dot(c_g, st_ref[g].astype(bf16))
        chunk_decay = []
        for jj in range(HEADS_PER_GROUP // 2):
            j = g * (HEADS_PER_GROUP // 2) + jj
            cs = slice(j * LANES, (j + 1) * LANES)
            ha, hb = 2 * j, 2 * j + 1
            x_p = xc_ref[:, cs].astype(f32)
            cum_a, cum_b = cum_ref[:, ha:ha + 1], cum_ref[:, hb:hb + 1]
            cume = jnp.where(lo_half, cum_a, cum_b)
            cume_last = jnp.where(lo_half_row, cum_a[q - 1:q, :], cum_b[q - 1:q, :])
            xdt = x_p * jnp.where(lo_half, dt_ref[:, ha:ha + 1], dt_ref[:, hb:hb + 1])
            ls = []
            for h, cum_h in ((ha, cum_a), (hb, cum_b)):
                seg = cum_h - cumt_ref[h:h + 1, :]
                ls.append(jnp.exp(jnp.where(causal, seg, NEG_INF)) * cb)
            l2 = jnp.concatenate(ls, axis=1).astype(bf16)
            x2 = jnp.concatenate([jnp.where(lo_half, xdt, 0.0), jnp.where(lo_half, 0.0, xdt)], axis=0).astype(bf16)
            y = _dot(l2, x2) + y_off[:, jj * LANES:(jj + 1) * LANES] * jnp.exp(cume) + dsk_ref[:, cs] * x_p
            yacc_ref[:, cs] = y * _silu(z_ref[:, cs].astype(f32))
            xw_ref[:, cs] = (xdt * jnp.exp(cume_last - cume)).astype(bf16)
            chunk_decay.append(jnp.exp(cume_last))
        gs = slice(g * GROUP_WIDTH, (g + 1) * GROUP_WIDTH)
        st_ref[g] = st_ref[g] * jnp.concatenate(chunk_decay, axis=1) + _dot_tn(b_g, xw_ref[:, gs])

    y = yacc_ref[...]
    y = y * lax.rsqrt(jnp.mean(y * y, axis=-1, keepdims=True) + RMS_EPS) * nw_ref[...]
    y_ref[...] = y.astype(y_ref.dtype)

    @pl.when(c == nc - 1)
    def _():
        for g in range(N_SSD_GROUPS):
            so_ref[g * GROUP_WIDTH:(g + 1) * GROUP_WIDTH, :] = st_ref[g].T


def _ssd_prompt(z, xc, h1, wdt, dtb, alog, dsk, nw, *, batch, carry=None):
    m = z.shape[0]
    nc = m // batch // SSD_CHUNK
    q = SSD_CHUNK
    row_map = lambda i: (i, 0)
    return _call(
        functools.partial(_ssd_prompt_kernel, nc=nc),
        grid=(batch * nc,),
        in_specs=[pl.BlockSpec((q, D_INNER), row_map), pl.BlockSpec((q, CONV_DIM), row_map),
                  pl.BlockSpec((q, D_MODEL), row_map),
                  _resident(wdt.shape), _resident(dtb.shape), _resident(alog.shape), _resident(dsk.shape),
                  _resident(nw.shape)],
        out_specs=[pl.BlockSpec((q, D_INNER), row_map),
                   pl.BlockSpec((None, D_INNER, D_STATE), lambda i: (i // nc, 0, 0))],
        out_shape=[jax.ShapeDtypeStruct((m, D_INNER), bf16),
                   jax.ShapeDtypeStruct((batch, D_INNER, D_STATE), f32)],
        scratch_shapes=[pltpu.VMEM((N_SSD_GROUPS, D_STATE, GROUP_WIDTH), f32),
                        pltpu.VMEM((q, LANES), f32), pltpu.VMEM((q, LANES), f32), pltpu.VMEM((LANES, q), f32),
                        pltpu.VMEM((q, D_INNER), bf16), pltpu.VMEM((q, D_INNER), f32)],
        name="ssd_prompt", args=(z, xc, h1, wdt, dtb, alog, dsk, nw), carry=carry)


def _ssd_decode_kernel(z_ref, xbc_ref, h1_ref, cst_ref, s_ref, wdt_ref, cw_ref, cb_ref, dtb_ref, alog_ref, dsk_ref,
                       nw_ref, e_ref, y_ref, so_ref, xdtt_ref, yt_ref, *, bb):
    acc = cb_ref[...] + xbc_ref[...] * cw_ref[CONV_WIDTH - 1:CONV_WIDTH, :]
    for w in range(CONV_WIDTH - 1):
        acc = acc + cst_ref[w] * cw_ref[w:w + 1, :]
    xc = _silu(acc)
    xs = xc[:, :D_INNER]
    lane_h = lax.broadcasted_iota(jnp.int32, (bb, LANES), 1)
    dt = _softplus(_dot(h1_ref[...].astype(bf16), wdt_ref[...]) + dtb_ref[...])
    dt = jnp.where(lane_h < N_SSD_HEADS, dt, 0.0)
    decay = jnp.exp(dt * (-jnp.exp(alog_ref[...])))
    xdt = xs * _dot_exact(dt, e_ref[...])
    pad = lambda a: jnp.concatenate([a, jnp.zeros((LANES - bb, a.shape[1]), f32)], axis=0)
    xdtt_ref[...] = pad(xdt).T.astype(bf16)
    row = lax.broadcasted_iota(jnp.int32, (LANES, LANES), 0)
    lane = lax.broadcasted_iota(jnp.int32, (LANES, LANES), 1)
    for g in range(N_SSD_GROUPS):
        gs = slice(g * GROUP_WIDTH, (g + 1) * GROUP_WIDTH)
        b_pad = pad(xc[:, D_INNER + g * D_STATE:D_INNER + (g + 1) * D_STATE])
        c_pad_t = pad(xc[:, D_INNER + (N_SSD_GROUPS + g) * D_STATE:D_INNER + (N_SSD_GROUPS + g + 1) * D_STATE]).T
        y_t = jnp.zeros((GROUP_WIDTH, LANES), f32)
        for j in range(bb):
            outer = _dot(xdtt_ref[gs, :], jnp.where(row == j, b_pad, 0.0).astype(bf16))
            pieces = []
            for hh in range(HEADS_PER_GROUP):
                h = g * HEADS_PER_GROUP + hh
                rs = slice(hh * SSD_HEAD_DIM, (hh + 1) * SSD_HEAD_DIM)
                new = s_ref[j, g * GROUP_WIDTH + hh * SSD_HEAD_DIM:g * GROUP_WIDTH + (hh + 1) * SSD_HEAD_DIM, :] \
                    * decay[j:j + 1, h:h + 1] + outer[rs, :]
                so_ref[j, g * GROUP_WIDTH + hh * SSD_HEAD_DIM:g * GROUP_WIDTH + (hh + 1) * SSD_HEAD_DIM, :] = new
                pieces.append(new.astype(bf16))
            y_t = y_t + _dot(jnp.concatenate(pieces, axis=0), jnp.where(lane == j, c_pad_t, 0.0).astype(bf16))
        yt_ref[gs, :] = y_t
    y = yt_ref[...].T[:bb, :] + dsk_ref[...] * xs
    y = y * _silu(z_ref[...])
    y = y * lax.rsqrt(jnp.mean(y * y, axis=-1, keepdims=True) + RMS_EPS) * nw_ref[...]
    y_ref[...] = y


def _ssd_decode(z, xbc, h1, conv_state, ssm_state, wdt, cw, cb, dtb, alog, dsk, nw, e, *, bb):
    n = z.shape[0]
    return pl.pallas_call(
        functools.partial(_ssd_decode_kernel, bb=bb),
        grid=(n // bb,),
        in_specs=[pl.BlockSpec((bb, D_INNER), lambda i: (i, 0)), pl.BlockSpec((bb, CONV_DIM), lambda i: (i, 0)),
                  pl.BlockSpec((bb, D_MODEL), lambda i: (i, 0)),
                  pl.BlockSpec((CONV_WIDTH - 1, bb, CONV_DIM), lambda i: (0, i, 0)),
                  pl.BlockSpec((bb, D_INNER, D_STATE), lambda i: (i, 0, 0)),
                  _resident(wdt.shape), _resident(cw.shape), _resident(cb.shape), _resident(dtb.shape),
                  _resident(alog.shape), _resident(dsk.shape), _resident(nw.shape), _resident(e.shape)],
        out_specs=[pl.BlockSpec((bb, D_INNER), lambda i: (i, 0)),
                   pl.BlockSpec((bb, D_INNER, D_STATE), lambda i: (i, 0, 0))],
        out_shape=[jax.ShapeDtypeStruct((n, D_INNER), f32),
                   jax.ShapeDtypeStruct((n, D_INNER, D_STATE), f32)],
        scratch_shapes=[pltpu.VMEM((D_INNER, LANES), bf16), pltpu.VMEM((D_INNER, LANES), f32)],
        compiler_params=_cparams(1),
        name="ssd_decode",
    )(z, xbc, h1, jnp.swapaxes(conv_state, 0, 1), ssm_state, wdt, cw, cb, dtb, alog, dsk, nw, e)


def _attn_prompt_kernel(slope_ref, q_ref, k_ref, v_ref, o_ref, st_ref, kprev_ref, vprev_ref, *, dilation, rows):
    n = pl.program_id(2)
    blk = ATTN_BLOCK
    nb = rows // blk

    @pl.when(n == 0)
    def _():
        kprev_ref[...] = jnp.zeros_like(kprev_ref)
        vprev_ref[...] = jnp.zeros_like(vprev_ref)

    a = lax.broadcasted_iota(jnp.int32, (blk, blk), 0)
    c = lax.broadcasted_iota(jnp.int32, (blk, blk), 1)
    dist_prev = (dilation * (blk + a - c)).astype(f32)
    dist_cur = (dilation * (a - c)).astype(f32)
    valid_cur = c <= a
    valid_prev = c >= a
    valid_prev_first = (c - a + jnp.where(n > 0, 0, -2 * blk)) >= 0
    bias_cur, bias_prev, bias_prev_first = [], [], []
    for h in range(HEADS_PER_PATTERN):
        slope = slope_ref[h:h + 1, :]
        bias_cur.append(jnp.where(valid_cur, -slope * dist_cur, NEG_INF))
        bias_prev.append(jnp.where(valid_prev, -slope * dist_prev, NEG_INF))
        bias_prev_first.append(jnp.where(valid_prev_first, -slope * dist_prev, NEG_INF))

    qs, kcs, vcs, kps, vps, bcs, bps = [], [], [], [], [], [], []
    for i in range(nb):
        rs = slice(i * blk, (i + 1) * blk)
        ps = slice((i - 1) * blk, i * blk)
        for h in range(HEADS_PER_PATTERN):
            hs = slice(h * ATTN_HEAD_DIM, (h + 1) * ATTN_HEAD_DIM)
            qs.append(q_ref[rs, hs])
            kcs.append(k_ref[rs, hs])
            vcs.append(v_ref[rs, hs])
            kps.append(k_ref[ps, hs] if i > 0 else kprev_ref[:, hs])
            vps.append(v_ref[ps, hs] if i > 0 else vprev_ref[:, hs])
            bcs.append(bias_cur[h])
            bps.append(bias_prev[h] if i > 0 else bias_prev_first[h])
    q3 = jnp.stack(qs)
    scale = ATTN_HEAD_DIM ** -0.5
    qk = lambda x, y: jnp.einsum("bqd,bkd->bqk", x, y, preferred_element_type=f32)
    pv = lambda x, y: jnp.einsum("bqk,bkd->bqd", x, y, preferred_element_type=f32)
    s_p = qk(q3, jnp.stack(kps)) * scale + jnp.stack(bps)
    s_c = qk(q3, jnp.stack(kcs)) * scale + jnp.stack(bcs)
    m = jnp.maximum(jnp.max(s_p, axis=-1, keepdims=True), jnp.max(s_c, axis=-1, keepdims=True))
    p_p = jnp.exp(s_p - m)
    p_c = jnp.exp(s_c - m)
    l = jnp.sum(p_p, axis=-1, keepdims=True) + jnp.sum(p_c, axis=-1, keepdims=True)
    o = (pv(p_p.astype(bf16), jnp.stack(vps)) + pv(p_c.astype(bf16), jnp.stack(vcs))) / l
    lse = m + jnp.log(l)
    lane = lax.broadcasted_iota(jnp.int32, (blk, LANES), 1)
    for i in range(nb):
        rs = slice(i * blk, (i + 1) * blk)
        stat = jnp.zeros((blk, LANES), f32)
        for h in range(HEADS_PER_PATTERN):
            idx = i * HEADS_PER_PATTERN + h
            o_ref[rs, h * ATTN_HEAD_DIM:(h + 1) * ATTN_HEAD_DIM] = o[idx].astype(o_ref.dtype)
            stat = jnp.where(lane == h, lse[idx], stat)
        st_ref[rs, :] = stat
    kprev_ref[...] = k_ref[rows - blk:rows, :]
    vprev_ref[...] = v_ref[rows - blk:rows, :]


def _attn_prompt(qkv, slopes, g):
    window, dilation = ATTN_PATTERNS[g]
    assert window // dilation == ATTN_BLOCK
    batch, _, length, _ = qkv.shape
    rows = min(length, 512)
    spec = lambda width, off: pl.BlockSpec((None, None, rows, width), lambda b, r, n: (b, r, n, off))
    return pl.pallas_call(
        functools.partial(_attn_prompt_kernel, dilation=dilation, rows=rows),
        grid=(batch, dilation, length // rows),
        in_specs=[_resident(slopes.shape), spec(ATTN_WIDTH, 0), spec(ATTN_WIDTH, 1), spec(ATTN_WIDTH, 2)],
        out_specs=[spec(ATTN_WIDTH, 0), spec(LANES, 0)],
        out_shape=[jax.ShapeDtypeStruct((batch, dilation, length, ATTN_WIDTH), bf16),
                   jax.ShapeDtypeStruct((batch, dilation, length, LANES), f32)],
        scratch_shapes=[pltpu.VMEM((ATTN_BLOCK, ATTN_WIDTH), bf16), pltpu.VMEM((ATTN_BLOCK, ATTN_WIDTH), bf16)],
        compiler_params=_cparams(3),
        name=f"attn_prompt_d{dilation}",
    )(slopes, qkv, qkv, qkv)


def _attn_decode_kernel(bias_ref, qkv_ref, c0_ref, c1_ref, c2_ref, o_ref):
    scale = ATTN_HEAD_DIM ** -0.5
    outs, ms, ls = [], [], []
    for g, c_ref in enumerate((c0_ref, c1_ref, c2_ref)):
        q = qkv_ref[:, g, 0]
        k_new = qkv_ref[:, g, 1]
        v_new = qkv_ref[:, g, 2]
        s = jnp.sum(c_ref[:, :, 0] * q[:, None], axis=-1, keepdims=True) * scale + bias_ref[g]
        s_new = jnp.sum(k_new * q, axis=-1, keepdims=True) * scale
        m = jnp.maximum(jnp.max(s, axis=1), s_new)
        p = jnp.exp(s - m[:, None])
        p_new = jnp.exp(s_new - m)
        l = jnp.sum(p, axis=1) + p_new
        o = (jnp.sum(p * c_ref[:, :, 1], axis=1) + p_new * v_new) / l
        outs.append(o)
        ms.append(m)
        ls.append(l)
    m_all = jnp.maximum(jnp.maximum(ms[0], ms[1]), ms[2])
    wgts = [l * jnp.exp(m - m_all) for m, l in zip(ms, ls)]
    num = wgts[0] * outs[0] + wgts[1] * outs[1] + wgts[2] * outs[2]
    o_ref[...] = num / (wgts[0] + wgts[1] + wgts[2])


def _attn_decode(qkv, caches, bias, *, bb):
    n = qkv.shape[0]
    in_specs = [_resident(bias.shape),
                pl.BlockSpec((bb, N_PATTERNS, 3, HEADS_PER_PATTERN, ATTN_HEAD_DIM), lambda i: (i, 0, 0, 0, 0))]
    views = []
    for g, (window, dilation) in enumerate(ATTN_PATTERNS):
        assert caches[g].shape[1] == window and window // dilation == ATTN_BLOCK
        views.append(caches[g].reshape(n, ATTN_BLOCK, dilation, 2, HEADS_PER_PATTERN, ATTN_HEAD_DIM))
        in_specs.append(pl.BlockSpec((bb, ATTN_BLOCK, None, 2, HEADS_PER_PATTERN, ATTN_HEAD_DIM),
                                     lambda i: (i, 0, 0, 0, 0, 0)))
    return pl.pallas_call(
        _attn_decode_kernel,
        grid=(n // bb,),
        in_specs=in_specs,
        out_specs=pl.BlockSpec((bb, HEADS_PER_PATTERN, ATTN_HEAD_DIM), lambda i: (i, 0, 0)),
        out_shape=jax.ShapeDtypeStruct((n, HEADS_PER_PATTERN, ATTN_HEAD_DIM), f32),
        compiler_params=_cparams(1),
        name="attn_decode",
    )(bias, qkv, *views)


def _post_mix_kernel(*refs, dilations):
    n_pat = len(dilations)
    merged = n_pat > 1
    h1_ref, y_ref = refs[0], refs[1]
    o_refs = refs[2:2 + n_pat]
    st_refs = refs[2 + n_pat:2 + 2 * n_pat] if merged else ()
    k = 2 + n_pat + len(st_refs)
    wg_ref, wso_ref, wao_ref, wmo_ref, lnw_ref, lnb_ref, out_ref = refs[k:k + 7]
    scratch = refs[k + 7:]
    tm = h1_ref.shape[0]
    h1 = h1_ref[...]
    gates = jax.nn.sigmoid(_dot(h1.astype(bf16), wg_ref[...]))
    if merged:
        o_heads, lses = [], []
        for g, d in enumerate(dilations):
            if d == 1:
                o_heads.append([o_refs[g][:, h * ATTN_HEAD_DIM:(h + 1) * ATTN_HEAD_DIM].astype(f32)
                                for h in range(HEADS_PER_PATTERN)])
                lses.append(st_refs[g])
                continue
            o_scr, st_scr = scratch[0], scratch[1]
            scratch = scratch[2:]
            for r in range(d):
                st_scr[pl.ds(r, tm // d, stride=d), :] = st_refs[g][r]
                for h in range(HEADS_PER_PATTERN):
                    o_scr[h, pl.ds(r, tm // d, stride=d), :] = (
                        o_refs[g][r, :, h * ATTN_HEAD_DIM:(h + 1) * ATTN_HEAD_DIM].astype(f32))
            o_heads.append([o_scr[h] for h in range(HEADS_PER_PATTERN)])
            lses.append(st_scr)
        heads = []
        for h in range(HEADS_PER_PATTERN):
            lse = [st[:, h:h + 1] for st in lses]
            top = functools.reduce(jnp.maximum, lse)
            wgt = [jnp.exp(v - top) for v in lse]
            num = sum(w * o[h] for w, o in zip(wgt, o_heads))
            heads.append(num / sum(wgt))
        attn = jnp.concatenate(heads, axis=1)
    else:
        attn = o_refs[0][...]
    t_ssd = _dot(y_ref[...].astype(bf16), wso_ref[...])
    t_attn = _dot(attn.astype(bf16), wao_ref[...])
    u = gates[:, :D_MODEL] * t_ssd + gates[:, D_MODEL:] * t_attn
    mix = _dot(u.astype(bf16), wmo_ref[...])
    out_ref[...] = _layer_norm(DEEPNORM_ALPHA * h1 + mix, lnw_ref[...], lnb_ref[...])


def _post_mix(h1, y_ssd, outs, stats, dilations, wg, wso, wao, wmo, lnw, lnb, *, tm, seq, carry=None):
    m = h1.shape[0]
    steps = seq // tm
    row = lambda width: pl.BlockSpec((tm, width), lambda i: (i, 0))

    def pattern_spec(width, d):
        if d == 1:
            return row(width)
        return pl.BlockSpec((None, d, tm // d, width), lambda i: (i // steps, 0, i % steps, 0))

    in_specs = [row(D_MODEL), row(D_INNER)] + [pattern_spec(ATTN_WIDTH, d) for d in dilations]
    in_specs += [pattern_spec(LANES, d) for d in dilations[:len(stats)]]
    in_specs += [_resident(w.shape) for w in (wg, wso, wao, wmo, lnw, lnb)]
    scratch = []
    for d in dilations:
        if d > 1:
            scratch += [pltpu.VMEM((HEADS_PER_PATTERN, tm, ATTN_HEAD_DIM), f32), pltpu.VMEM((tm, LANES), f32)]
    return _call(functools.partial(_post_mix_kernel, dilations=tuple(dilations)), grid=(m // tm,), in_specs=in_specs,
                 out_specs=[row(D_MODEL)], out_shape=[jax.ShapeDtypeStruct((m, D_MODEL), f32)],
                 scratch_shapes=scratch, name="post_mix",
                 args=(h1, y_ssd, *outs, *stats, wg, wso, wao, wmo, lnw, lnb), carry=carry)


SHIFT_CHUNK = 256
SHIFT_BLOCK_BYTES = 8 * 1024 * 1024


def _kv_shift_kernel(c_ref, new_ref, o_ref, *, key_rows):
    bb, rows, _ = c_ref.shape
    keep = rows - key_rows
    n_chunks = pl.cdiv(keep, SHIFT_CHUNK)
    for b in range(bb):
        def body(k, carry):
            r0 = pl.multiple_of(jnp.minimum(k * SHIFT_CHUNK, keep - SHIFT_CHUNK), 8)
            o_ref[b, pl.ds(r0, SHIFT_CHUNK), :] = c_ref[b, pl.ds(r0 + key_rows, SHIFT_CHUNK), :]
            return carry
        lax.fori_loop(0, n_chunks, body, 0)
        o_ref[b, keep:rows, :] = new_ref[b]


def _kv_shift(cache, new):
    n, w = cache.shape[:2]
    key_rows = 2 * HEADS_PER_PATTERN * ATTN_HEAD_DIM // LANES
    rows = w * key_rows
    assert key_rows % 8 == 0 and rows - key_rows >= SHIFT_CHUNK
    bb = max(1, min(n, SHIFT_BLOCK_BYTES // (rows * LANES * 4)))
    assert n % bb == 0
    out = pl.pallas_call(
        functools.partial(_kv_shift_kernel, key_rows=key_rows),
        grid=(n // bb,),
        in_specs=[pl.BlockSpec((bb, rows, LANES), lambda i: (i, 0, 0)),
                  pl.BlockSpec((bb, key_rows, LANES), lambda i: (i, 0, 0))],
        out_specs=pl.BlockSpec((bb, rows, LANES), lambda i: (i, 0, 0)),
        out_shape=jax.ShapeDtypeStruct((n, rows, LANES), cache.dtype),
        compiler_params=_cparams(1),
        name=f"kv_shift_w{w}",
    )(cache.reshape(n, rows, LANES), new.reshape(n, key_rows, LANES))
    return out.reshape(cache.shape)


def _alibi_slopes():
    n = N_PATTERNS * HEADS_PER_PATTERN
    e = jnp.arange(1, n + 1, dtype=f32)
    return (2.0 ** (-ALIBI_MAX_EXP * e / n)).reshape(N_PATTERNS, HEADS_PER_PATTERN)


def _prepare_weights(ln_w, ln_b, ffn_w13, ffn_w2, w_in, conv_w, conv_b, dt_bias, a_log, d_skip, ssd_norm_w,
                     w_ssd_out, w_attn_out, w_mix_out, w_pe_gate, w_pe_proj):
    cuts = [0]
    for s in IN_SPLITS:
        cuts.append(cuts[-1] + s)
    wz, wxbc, wdt, wq, wk, wv = (w_in[:, cuts[i]:cuts[i + 1]] for i in range(6))
    pad_heads = LANES - N_SSD_HEADS
    head_of_col = jnp.arange(D_INNER, dtype=jnp.int32) // SSD_HEAD_DIM
    return dict(
        ln_w=[ln_w[i:i + 1] for i in range(3)], ln_b=[ln_b[i:i + 1] for i in range(3)],
        w13=[ffn_w13[i].astype(bf16) for i in range(2)], w2=[ffn_w2[i].astype(bf16) for i in range(2)],
        wz=wz.astype(bf16), wxbc=wxbc.astype(bf16),
        wdt=jnp.pad(wdt, ((0, 0), (0, pad_heads))).astype(bf16),
        wqkv=jnp.concatenate([m[:, g * ATTN_WIDTH:(g + 1) * ATTN_WIDTH] for g in range(N_PATTERNS)
                              for m in (wq, wk, wv)], axis=1).astype(bf16),
        wg=w_in[:, cuts[6]:].astype(bf16),
        cw=conv_w, cb=conv_b[None, :],
        dtb=jnp.pad(dt_bias, (0, pad_heads))[None, :], alog=jnp.pad(a_log, (0, pad_heads))[None, :],
        dsk=jnp.repeat(d_skip, SSD_HEAD_DIM)[None, :], nw=ssd_norm_w[None, :],
        expand=(jnp.arange(LANES, dtype=jnp.int32)[:, None] == head_of_col[None, :]).astype(f32),
        wso=w_ssd_out.astype(bf16), wao=w_attn_out.astype(bf16), wmo=w_mix_out.astype(bf16),
        wpg=w_pe_gate.astype(bf16), wpp=w_pe_proj.astype(bf16),
    )


def _ssd_args(w):
    return (w["wdt"], w["cw"], w["cb"], w["dtb"], w["alog"], w["dsk"], w["nw"], w["expand"])


class _CacheCarrier:
    def __init__(self, cache, new):
        n, w = cache.shape[:2]
        self.shape = cache.shape
        self.key_rows = 2 * HEADS_PER_PATTERN * ATTN_HEAD_DIM // LANES
        self.cache = cache.reshape(n, w * self.key_rows, LANES)
        self.new = new.reshape(n, self.key_rows, LANES)
        self.out, self.done = None, 0

    def take(self, steps, parts):
        samples = min(steps // parts, self.cache.shape[0] - self.done)
        if samples <= 0:
            return None
        return (self.cache, self.new, self.out, self.done * parts, samples * parts, parts)

    def gave(self, carry, out):
        if carry is not None:
            self.out, self.done = out, self.done + carry[4] // carry[5]

    def result(self):
        assert self.done == self.cache.shape[0]
        return self.out.reshape(self.shape)


def _prompt_layer(x, p, w, big_carrier, mid_carrier):
    batch, seq, _ = x.shape
    assert seq % SSD_CHUNK == 0 and all(seq % (ATTN_BLOCK * d) == 0 and seq >= win for win, d in ATTN_PATTERNS)
    tm = 512
    x2 = x.reshape(batch * seq, D_MODEL)

    def carried(carrier, n_steps, parts, fn, *args, **kwargs):
        carry = carrier.take(n_steps, parts) if carrier is not None else None
        outs, cache_out = fn(*args, carry=carry, **kwargs)
        if carrier is not None:
            carrier.gave(carry, cache_out)
        return outs

    (h1,), _ = _ffn_ln(x2, w["w13"][0], w["w2"][0], w["ln_w"][0], w["ln_b"][0], tm=tm)
    z, xc, xbc_tail = carried(big_carrier, batch * seq // PROJ_ZX_ROWS, 1, _proj_zx, h1, w["wz"], w["wxbc"], w["cw"],
                              w["cb"], batch=batch, seq=seq, tm=PROJ_ZX_ROWS)
    qkv = carried(big_carrier, batch * seq // PROJ_QKV_ROWS, 1, _qkv_proj, h1, w["wqkv"], batch=batch, seq=seq,
                  tm=PROJ_QKV_ROWS)
    y_ssd, new_ssm = carried(mid_carrier, batch * seq // SSD_CHUNK, 1, _ssd_prompt, z, xc, h1, w["wdt"], w["dtb"],
                             w["alog"], w["dsk"], w["nw"], batch=batch)
    slopes = _alibi_slopes()
    dilations = [d for _, d in ATTN_PATTERNS]
    outs, stats, new_kv = [], [], []
    for g, (win, d) in enumerate(ATTN_PATTERNS):
        o, st = _attn_prompt(qkv[g], jnp.broadcast_to(slopes[g][:, None], (HEADS_PER_PATTERN, LANES)), g)
        if d == 1:
            o, st = o.reshape(batch * seq, ATTN_WIDTH), st.reshape(batch * seq, LANES)
        outs.append(o)
        stats.append(st)
        tail = qkv[g][:, :, seq // d - win // d:, ATTN_WIDTH:]
        new_kv.append(jnp.swapaxes(tail, 1, 2).reshape(batch, win, 2, HEADS_PER_PATTERN, ATTN_HEAD_DIM).astype(f32))
    (h2,), _ = _post_mix(h1, y_ssd, outs, stats, dilations, w["wg"], w["wso"], w["wao"], w["wmo"], w["ln_w"][1],
                         w["ln_b"][1], tm=tm, seq=seq)
    y, = carried(big_carrier, batch * seq // tm, 1, _ffn_ln, h2, w["w13"][1], w["w2"][1], w["ln_w"][2], w["ln_b"][2],
                 pe=(p.reshape(batch * seq, PLE_DIM), w["wpg"], w["wpp"]), tm=tm)
    new_conv = xbc_tail[:, 8 - (CONV_WIDTH - 1):]
    return (y.reshape(batch, seq, D_MODEL), new_ssm.reshape(batch, N_SSD_HEADS, SSD_HEAD_DIM, D_STATE),
            new_conv, new_kv)


def _decode_layer(x, p, ssm_state, conv_state, caches, w):
    n, t, _ = x.shape
    assert t == 1
    tm = n
    bb = 8
    (h1,), _ = _ffn_ln(x.reshape(n, D_MODEL), w["w13"][0], w["w2"][0], w["ln_w"][0], w["ln_b"][0], tm=tm)
    z = _matmul(h1, w["wz"], f32, tm=tm)
    xbc = _matmul(h1, w["wxbc"], f32, tm=tm)
    qkv = _matmul(h1, w["wqkv"], f32, tm=tm)
    y_ssd, new_ssm = _ssd_decode(z, xbc, h1, conv_state, ssm_state.reshape(n, D_INNER, D_STATE), *_ssd_args(w), bb=bb)
    slopes = _alibi_slopes()
    steps_back = (ATTN_BLOCK - jnp.arange(ATTN_BLOCK, dtype=jnp.int32))
    bias = jnp.stack([-slopes[g][None, :] * (d * steps_back).astype(f32)[:, None]
                      for g, (_, d) in enumerate(ATTN_PATTERNS)])
    bias = jnp.broadcast_to(bias[..., None], bias.shape + (1,))
    qkv5 = qkv.reshape(n, N_PATTERNS, 3, HEADS_PER_PATTERN, ATTN_HEAD_DIM)
    attn = _attn_decode(qkv5, caches, bias, bb=bb).reshape(n, ATTN_WIDTH)
    (h2,), _ = _post_mix(h1, y_ssd, [attn], [], [1], w["wg"], w["wso"], w["wao"], w["wmo"], w["ln_w"][1],
                         w["ln_b"][1], tm=tm, seq=n)
    (y,), _ = _ffn_ln(h2, w["w13"][1], w["w2"][1], w["ln_w"][2], w["ln_b"][2],
                      pe=(p.reshape(n, PLE_DIM), w["wpg"], w["wpp"]), tm=tm)
    new_conv = jnp.concatenate([conv_state[:, 1:], xbc[:, None, :]], axis=1)
    new_rows = [qkv5[:, g, 1:3] for g in range(N_PATTERNS)]
    return y.reshape(n, 1, D_MODEL), new_ssm.reshape(n, N_SSD_HEADS, SSD_HEAD_DIM, D_STATE), new_conv, new_rows


PROJ_ZX_ROWS = 256
PROJ_QKV_ROWS = 256


def _carrier_capacity(batch, seq):
    rows = batch * seq
    return rows // PROJ_ZX_ROWS + rows // PROJ_QKV_ROWS + rows // 512, rows // SSD_CHUNK


def kernel(x_prompt, x_sample, state_ssm, state_conv, cache_kv_w128, cache_kv_w512, cache_kv_w2048, p_prompt, p_sample, ln_w, ln_b, ffn_w13, ffn_w2, w_in, conv_w, conv_b, dt_bias, a_log, d_skip, ssd_norm_w, w_ssd_out, w_attn_out, w_mix_out, w_pe_gate, w_pe_proj):
    assert ln_w.shape[0] == DEPTH
    w = _prepare_weights(ln_w[0], ln_b[0], ffn_w13[0], ffn_w2[0], w_in[0], conv_w[0], conv_b[0], dt_bias[0], a_log[0],
                         d_skip[0], ssd_norm_w[0], w_ssd_out[0], w_attn_out[0], w_mix_out[0], w_pe_gate[0],
                         w_pe_proj[0])
    caches = (cache_kv_w128[0], cache_kv_w512[0], cache_kv_w2048[0])
    y_s, ssm_s, conv_s, new_rows = _decode_layer(x_sample, p_sample[0], state_ssm[0], state_conv[0], caches, w)
    order = sorted(range(N_PATTERNS), key=lambda g: -caches[g].shape[1])
    big, mid = order[0], order[1]
    cap_big, cap_mid = _carrier_capacity(x_prompt.shape[0], x_prompt.shape[1])
    big_carrier = _CacheCarrier(caches[big], new_rows[big]) if caches[big].shape[0] <= cap_big else None
    mid_carrier = _CacheCarrier(caches[mid], new_rows[mid]) if caches[mid].shape[0] <= cap_mid else None
    y_p, ssm_p, conv_p, kv_p = _prompt_layer(x_prompt, p_prompt[0], w, big_carrier, mid_carrier)
    carriers = {big: big_carrier, mid: mid_carrier}
    kv_s = [carriers[g].result() if carriers.get(g) is not None else _kv_shift(caches[g], new_rows[g])
            for g in range(N_PATTERNS)]
    lead = lambda a: a[None]
    return (y_p, y_s, lead(ssm_p), lead(conv_p), lead(kv_p[0]), lead(kv_p[1]), lead(kv_p[2]),
            lead(ssm_s), lead(conv_s), lead(kv_s[0]), lead(kv_s[1]), lead(kv_s[2]))
```

```python
import functools

import jax
import jax.numpy as jnp
from jax import lax
from jax.experimental import pallas as pl
from jax.experimental.pallas import tpu as pltpu

f32 = jnp.float32
bf16 = jnp.bfloat16

D_MODEL = 1024
D_INNER = 2 * D_MODEL
SSD_HEAD_DIM = 64
N_SSD_HEADS = D_INNER // SSD_HEAD_DIM
N_SSD_GROUPS = 4
HEADS_PER_GROUP = N_SSD_HEADS // N_SSD_GROUPS
GROUP_WIDTH = HEADS_PER_GROUP * SSD_HEAD_DIM
D_STATE = 128
CONV_WIDTH = 4
CONV_DIM = D_INNER + 2 * N_SSD_GROUPS * D_STATE
SSD_CHUNK = 128
ATTN_PATTERNS = ((128, 1), (512, 4), (2048, 16))
N_PATTERNS = 3
HEADS_PER_PATTERN = 4
ATTN_HEAD_DIM = 128
ATTN_WIDTH = HEADS_PER_PATTERN * ATTN_HEAD_DIM
ATTN_QKV_WIDTH = N_PATTERNS * ATTN_WIDTH
PATTERN_QKV_WIDTH = 3 * ATTN_WIDTH
ATTN_BLOCK = 128
ALIBI_MAX_EXP = 8.0
D_FF = 2816
PLE_DIM = 256
LN_EPS = 1e-5
RMS_EPS = 1e-5
DEPTH = 1
DEEPNORM_ALPHA = (2.0 * DEPTH) ** 0.25
IN_SPLITS = (D_INNER, CONV_DIM, N_SSD_HEADS, ATTN_QKV_WIDTH, ATTN_QKV_WIDTH, ATTN_QKV_WIDTH)

LANES = 128
VMEM_LIMIT = 56 * 1024 * 1024
NEG_INF = float("-inf")


def _cparams(n_axes):
    return pltpu.CompilerParams(dimension_semantics=("arbitrary",) * n_axes, vmem_limit_bytes=VMEM_LIMIT)


def _resident(shape):
    nd = len(shape)
    return pl.BlockSpec(shape, lambda *_: (0,) * nd, pipeline_mode=pl.Buffered(1))


def _layer_norm(y, w, b):
    mu = jnp.mean(y, axis=-1, keepdims=True)
    yc = y - mu
    var = jnp.mean(yc * yc, axis=-1, keepdims=True)
    return yc * lax.rsqrt(var + LN_EPS) * w + b


def _silu(x):
    return x * jax.nn.sigmoid(x)


def _softplus(x):
    return jnp.maximum(x, 0.0) + jnp.log1p(jnp.exp(-jnp.abs(x)))


def _dot(a, b):
    return jnp.dot(a, b, preferred_element_type=f32)


def _dot_nt(a, b):
    return lax.dot_general(a, b, (((1,), (1,)), ((), ())), preferred_element_type=f32)


def _dot_tn(a, b):
    return lax.dot_general(a, b, (((0,), (0,)), ((), ())), preferred_element_type=f32)


def _dot_exact(a, b):
    return jnp.dot(a, b, precision=lax.Precision.HIGHEST, preferred_element_type=f32)


def _carry_where(chunk, base, parts, cache_ref, new_ref, out_ref, buf_ref, sems):
    c = base + chunk
    part = lax.rem(c, parts)
    return lax.rem(chunk, 2), c // parts, part, pl.multiple_of(part * buf_ref.shape[1], 8)


def _carry_in(chunk, op, *refs):
    _, parts, cache_ref, new_ref, _, buf_ref, sems = refs
    key_rows, part_rows = new_ref.shape[1], buf_ref.shape[1]
    slot, b, part, r0 = _carry_where(chunk, *refs)

    @pl.when(part < parts - 1)
    def _():
        op(pltpu.make_async_copy(cache_ref.at[b, pl.ds(r0 + key_rows, part_rows)], buf_ref.at[slot], sems.at[0, slot]))

    @pl.when(part == parts - 1)
    def _():
        op(pltpu.make_async_copy(cache_ref.at[b, pl.ds(r0 + key_rows, part_rows - key_rows)],
                                 buf_ref.at[slot, pl.ds(0, part_rows - key_rows)], sems.at[0, slot]))
        op(pltpu.make_async_copy(new_ref.at[b], buf_ref.at[slot, pl.ds(part_rows - key_rows, key_rows)],
                                 sems.at[1, slot]))


def _carry_out(chunk, op, *refs):
    _, _, _, _, out_ref, buf_ref, sems = refs
    slot, b, _, r0 = _carry_where(chunk, *refs)
    op(pltpu.make_async_copy(buf_ref.at[slot], out_ref.at[b, pl.ds(r0, buf_ref.shape[1])], sems.at[2, slot]))


def _start(copy):
    copy.start()


def _wait(copy):
    copy.wait()


def _carry_begin(step, n_chunks, *refs):
    @pl.when(step == 0)
    def _():
        _carry_in(step, _start, *refs)

    @pl.when(step < n_chunks)
    def _():
        _carry_in(step, _wait, *refs)
        _carry_out(step, _start, *refs)

        @pl.when(step > 0)
        def _():
            _carry_out(step - 1, _wait, *refs)

        @pl.when(step + 1 < n_chunks)
        def _():
            _carry_in(step + 1, _start, *refs)


def _carry_end(step, n_chunks, *refs):
    @pl.when(step == n_chunks - 1)
    def _():
        _carry_out(step, _wait, *refs)


def _call(body, *, grid, in_specs, out_specs, out_shape, scratch_shapes=(), name, args, carry=None):
    out_specs, out_shape = list(out_specs), list(out_shape)
    if carry is None:
        res = pl.pallas_call(body, grid=grid, in_specs=list(in_specs), out_specs=out_specs, out_shape=out_shape,
                             scratch_shapes=list(scratch_shapes), compiler_params=_cparams(len(grid)), name=name)(*args)
        return list(res), None
    cache, new, prev, base, count, parts = carry
    part_rows = cache.shape[1] // parts
    assert cache.shape[1] % parts == 0 and part_rows % 8 == 0 and part_rows > new.shape[1]
    assert len(grid) == 1 and 0 < count <= grid[0]
    n_in, n_out, n_scr = len(in_specs), len(out_specs), len(scratch_shapes)
    n_cin = 2 if prev is None else 3

    def kernel(*refs):
        ins = refs[:n_in]
        cache_ref, new_ref = refs[n_in:n_in + 2]
        outs = refs[n_in + n_cin:n_in + n_cin + n_out]
        out_cache_ref = refs[n_in + n_cin + n_out]
        scratch = refs[n_in + n_cin + n_out + 1:n_in + n_cin + n_out + 1 + n_scr]
        buf_ref, sems = refs[-2:]
        step = pl.program_id(0)
        carry_refs = (base, parts, cache_ref, new_ref, out_cache_ref, buf_ref, sems)
        _carry_begin(step, count, *carry_refs)
        body(*ins, *outs, *scratch)
        _carry_end(step, count, *carry_refs)

    any_spec = pl.BlockSpec(memory_space=pl.ANY)
    cargs = (cache, new) if prev is None else (cache, new, prev)
    res = pl.pallas_call(
        kernel, grid=grid,
        in_specs=list(in_specs) + [any_spec] * n_cin,
        out_specs=out_specs + [any_spec],
        out_shape=out_shape + [jax.ShapeDtypeStruct(cache.shape, cache.dtype)],
        scratch_shapes=list(scratch_shapes) + [pltpu.VMEM((2, part_rows, cache.shape[2]), cache.dtype),
                                               pltpu.SemaphoreType.DMA((3, 2))],
        input_output_aliases={} if prev is None else {n_in + 2: n_out},
        compiler_params=_cparams(1), name=name + "_carry")(*args, *cargs)
    return list(res[:n_out]), res[n_out]


FFN_CHUNKS = (512, 512, 512, 512, 512, 256)


def _ffn_ln_kernel(*refs, with_pe):
    if with_pe:
        x_ref, w13_ref, w2_ref, lnw_ref, lnb_ref, p_ref, wpg_ref, wpp_ref, o_ref = refs
    else:
        x_ref, w13_ref, w2_ref, lnw_ref, lnb_ref, o_ref = refs
    x = x_ref[...]
    xb = x.astype(bf16)
    acc = None
    c0 = 0
    for width in FFN_CHUNKS:
        gate = _dot(xb, w13_ref[:, c0:c0 + width])
        up = _dot(xb, w13_ref[:, D_FF + c0:D_FF + c0 + width])
        part = _dot((_silu(gate) * up).astype(bf16), w2_ref[c0:c0 + width, :])
        acc = part if acc is None else acc + part
        c0 += width
    h = _layer_norm(DEEPNORM_ALPHA * x + 0.5 * acc, lnw_ref[...], lnb_ref[...])
    if with_pe:
        pe_gate = jax.nn.sigmoid(_dot(h.astype(bf16), wpg_ref[...]))
        h = h + pe_gate * _dot(p_ref[...].astype(bf16), wpp_ref[...])
    o_ref[...] = h


def _ffn_ln(x, w13, w2, lnw, lnb, pe=None, *, tm, carry=None):
    m = x.shape[0]
    assert sum(FFN_CHUNKS) == D_FF
    row = lambda width: pl.BlockSpec((tm, width), lambda i: (i, 0))
    in_specs = [row(D_MODEL), _resident(w13.shape), _resident(w2.shape), _resident(lnw.shape), _resident(lnb.shape)]
    args = [x, w13, w2, lnw, lnb]
    if pe is not None:
        p, wpg, wpp = pe
        in_specs += [row(PLE_DIM), _resident(wpg.shape), _resident(wpp.shape)]
        args += [p, wpg, wpp]
    return _call(functools.partial(_ffn_ln_kernel, with_pe=pe is not None), grid=(m // tm,), in_specs=in_specs,
                 out_specs=[row(D_MODEL)], out_shape=[jax.ShapeDtypeStruct((m, D_MODEL), f32)],
                 name="ffn_ln_pe" if pe is not None else "ffn_ln", args=args, carry=carry)


def _matmul_kernel(x_ref, w_ref, o_ref):
    o_ref[...] = _dot(x_ref[...].astype(bf16), w_ref[...]).astype(o_ref.dtype)


def _matmul(x, w, out_dtype, *, tm):
    m, k = x.shape
    n = w.shape[1]
    return pl.pallas_call(
        _matmul_kernel,
        grid=(m // tm,),
        in_specs=[pl.BlockSpec((tm, k), lambda i: (i, 0)), _resident((k, n))],
        out_specs=pl.BlockSpec((tm, n), lambda i: (i, 0)),
        out_shape=jax.ShapeDtypeStruct((m, n), out_dtype),
        compiler_params=_cparams(1),
        name="proj_matmul",
    )(x, w)


def _qkv_proj_kernel(x_ref, w_ref, *rest):
    outs, res_ref = rest[:N_PATTERNS], rest[N_PATTERNS]
    tm = x_ref.shape[0]
    xb = x_ref[...].astype(bf16)
    wide = 2 * LANES
    for c2 in range(N_PATTERNS * PATTERN_QKV_WIDTH // wide):
        res = _dot(xb, w_ref[:, c2 * wide:(c2 + 1) * wide])
        res_ref[2 * c2] = res[:, :LANES]
        res_ref[2 * c2 + 1] = res[:, LANES:]
    blocks = PATTERN_QKV_WIDTH // LANES
    for g, (_, d) in enumerate(ATTN_PATTERNS):
        for cb in range(blocks):
            cs = slice(cb * LANES, (cb + 1) * LANES)
            for r in range(d):
                outs[g][r, :, cs] = res_ref[g * blocks + cb, pl.ds(r, tm // d, stride=d), :].astype(bf16)


def _qkv_proj(x, w, *, batch, seq, tm, carry=None):
    m, k = x.shape
    steps = seq // tm
    return _call(
        _qkv_proj_kernel,
        grid=(m // tm,),
        in_specs=[pl.BlockSpec((tm, k), lambda i: (i, 0)), _resident(w.shape)],
        out_specs=[pl.BlockSpec((None, d, tm // d, PATTERN_QKV_WIDTH), lambda i: (i // steps, 0, i % steps, 0))
                   for _, d in ATTN_PATTERNS],
        out_shape=[jax.ShapeDtypeStruct((batch, d, seq // d, PATTERN_QKV_WIDTH), bf16) for _, d in ATTN_PATTERNS],
        scratch_shapes=[pltpu.VMEM((N_PATTERNS * PATTERN_QKV_WIDTH // LANES, tm, LANES), f32)],
        name="qkv_proj", args=(x, w), carry=carry)


CONV_ROWS = 64


def _proj_zx_kernel(x_ref, wz_ref, wx_ref, cw_ref, cb_ref, z_ref, xc_ref, tail_ref, xpad_ref, *, steps):
    i = pl.program_id(0)
    tm = x_ref.shape[0]
    halo = 8
    xb = x_ref[...].astype(bf16)
    z_ref[...] = _dot(xb, wz_ref[...]).astype(z_ref.dtype)

    @pl.when(i % steps == 0)
    def _():
        xpad_ref[0:halo, :] = jnp.zeros((halo, CONV_DIM), f32)

    for j in range(CONV_DIM // 512):
        xpad_ref[halo:halo + tm, j * 512:(j + 1) * 512] = _dot(xb, wx_ref[:, j * 512:(j + 1) * 512])
        for cb in range(2 * j, 2 * j + 2):
            cs = slice(cb * 256, (cb + 1) * 256)
            taps = [cw_ref[w:w + 1, cs] for w in range(CONV_WIDTH)]
            bias = cb_ref[:, cs]
            for r0 in range(0, tm, CONV_ROWS):
                blk = xpad_ref[r0:r0 + halo + CONV_ROWS, cs]
                acc = bias + blk[halo:] * taps[CONV_WIDTH - 1]
                for k in range(1, CONV_WIDTH):
                    acc = acc + pltpu.roll(blk, k, 0)[halo:] * taps[CONV_WIDTH - 1 - k]
                xc_ref[r0:r0 + CONV_ROWS, cs] = _silu(acc).astype(xc_ref.dtype)
    tail_ref[...] = xpad_ref[tm:tm + halo, :]
    xpad_ref[0:halo, :] = xpad_ref[tm:tm + halo, :]


def _proj_zx(x, wz, wx, cw, cb, *, batch, seq, tm, carry=None):
    m, k = x.shape
    steps = seq // tm
    return _call(
        functools.partial(_proj_zx_kernel, steps=steps),
        grid=(m // tm,),
        in_specs=[pl.BlockSpec((tm, k), lambda i: (i, 0)), _resident(wz.shape), _resident(wx.shape),
                  _resident(cw.shape), _resident(cb.shape)],
        out_specs=[pl.BlockSpec((tm, D_INNER), lambda i: (i, 0)), pl.BlockSpec((tm, CONV_DIM), lambda i: (i, 0)),
                   pl.BlockSpec((None, 8, CONV_DIM), lambda i: (i // steps, 0, 0))],
        out_shape=[jax.ShapeDtypeStruct((m, D_INNER), bf16), jax.ShapeDtypeStruct((m, CONV_DIM), bf16),
                   jax.ShapeDtypeStruct((batch, 8, CONV_DIM), f32)],
        scratch_shapes=[pltpu.VMEM((tm + 8, CONV_DIM), f32)],
        name="proj_zx", args=(x, wz, wx, cw, cb), carry=carry)


def _split3(x):
    hi = x.astype(bf16)
    r = x - hi.astype(f32)
    mid = r.astype(bf16)
    return hi, mid, (r - mid.astype(f32)).astype(bf16)


def _ssd_prompt_kernel(z_ref, xc_ref, h1_ref, wdt_ref, dtb_ref, alog_ref, dsk_ref, nw_ref, y_ref, so_ref,
                       st_ref, dt_ref, cum_ref, cumt_ref, xw_ref, yacc_ref, *, nc):
    c = pl.program_id(0) % nc
    q = SSD_CHUNK

    @pl.when(c == 0)
    def _():
        st_ref[...] = jnp.zeros_like(st_ref)

    lane = lax.broadcasted_iota(jnp.int32, (q, LANES), 1)
    row = lax.broadcasted_iota(jnp.int32, (q, LANES), 0)
    dt = _softplus(_dot(h1_ref[...].astype(bf16), wdt_ref[...]) + dtb_ref[...])
    dt = jnp.where(lane < N_SSD_HEADS, dt, 0.0)
    d_a = dt * (-jnp.exp(alog_ref[...]))
    causal = row >= lane
    c3 = _dot(causal.astype(bf16), jnp.concatenate(_split3(d_a), axis=1))
    cum = c3[:, :LANES] + c3[:, LANES:2 * LANES] + c3[:, 2 * LANES:]
    dt_ref[...] = dt
    cum_ref[...] = cum
    cumt_ref[...] = cum.T

    lo_half = lane < SSD_HEAD_DIM
    lo_half_row = lo_half[0:1, :]
    for g in range(N_SSD_GROUPS):
        b_g = xc_ref[:, D_INNER + g * D_STATE:D_INNER + (g + 1) * D_STATE]
        c_g = xc_ref[:, D_INNER + (N_SSD_GROUPS + g) * D_STATE:D_INNER + (N_SSD_GROUPS + g + 1) * D_STATE]
        cb = _dot_nt(c_g, b_g)
        y_off = _dot(c_g, st_ref[g].astype(bf16))
        chunk_decay = []
        for jj in range(HEADS_PER_GROUP // 2):
            j = g * (HEADS_PER_GROUP // 2) + jj
            cs = slice(j * LANES, (j + 1) * LANES)
            ha, hb = 2 * j, 2 * j + 1
            x_p = xc_ref[:, cs].astype(f32)
            cum_a, cum_b = cum_ref[:, ha:ha + 1], cum_ref[:, hb:hb + 1]
            cume = jnp.where(lo_half, cum_a, cum_b)
            cume_last = jnp.where(lo_half_row, cum_a[q - 1:q, :], cum_b[q - 1:q, :])
            xdt = x_p * jnp.where(lo_half, dt_ref[:, ha:ha + 1], dt_ref[:, hb:hb + 1])
            ls = []
            for h, cum_h in ((ha, cum_a), (hb, cum_b)):
                seg = cum_h - cumt_ref[h:h + 1, :]
                ls.append(jnp.exp(jnp.where(causal, seg, NEG_INF)) * cb)
            l2 = jnp.concatenate(ls, axis=1).astype(bf16)
            x2 = jnp.concatenate([jnp.where(lo_half, xdt, 0.0), jnp.where(lo_half, 0.0, xdt)], axis=0).astype(bf16)
            y = _dot(l2, x2) + y_off[:, jj * LANES:(jj + 1) * LANES] * jnp.exp(cume) + dsk_ref[:, cs] * x_p
            yacc_ref[:, cs] = y * _silu(z_ref[:, cs].astype(f32))
            xw_ref[:, cs] = (xdt * jnp.exp(cume_last - cume)).astype(bf16)
            chunk_decay.append(jnp.exp(cume_last))
        gs = slice(g * GROUP_WIDTH, (g + 1) * GROUP_WIDTH)
        st_ref[g] = st_ref[g] * jnp.concatenate(chunk_decay, axis=1) + _dot_tn(b_g, xw_ref[:, gs])

    y = yacc_ref[...]
    y = y * lax.rsqrt(jnp.mean(y * y, axis=-1, keepdims=True) + RMS_EPS) * nw_ref[...]
    y_ref[...] = y.astype(y_ref.dtype)

    @pl.when(c == nc - 1)
    def _():
        for g in range(N_SSD_GROUPS):
            so_ref[g * GROUP_WIDTH:(g + 1) * GROUP_WIDTH, :] = st_ref[g].T


def _ssd_prompt(z, xc, h1, wdt, dtb, alog, dsk, nw, *, batch, carry=None):
    m = z.shape[0]
    nc = m // batch // SSD_CHUNK
    q = SSD_CHUNK
    row_map = lambda i: (i, 0)
    return _call(
        functools.partial(_ssd_prompt_kernel, nc=nc),
        grid=(batch * nc,),
        in_specs=[pl.BlockSpec((q, D_INNER), row_map), pl.BlockSpec((q, CONV_DIM), row_map),
                  pl.BlockSpec((q, D_MODEL), row_map),
                  _resident(wdt.shape), _resident(dtb.shape), _resident(alog.shape), _resident(dsk.shape),
                  _resident(nw.shape)],
        out_specs=[pl.BlockSpec((q, D_INNER), row_map),
                   pl.BlockSpec((None, D_INNER, D_STATE), lambda i: (i // nc, 0, 0))],
        out_shape=[jax.ShapeDtypeStruct((m, D_INNER), bf16),
                   jax.ShapeDtypeStruct((batch, D_INNER, D_STATE), f32)],
        scratch_shapes=[pltpu.VMEM((N_SSD_GROUPS, D_STATE, GROUP_WIDTH), f32),
                        pltpu.VMEM((q, LANES), f32), pltpu.VMEM((q, LANES), f32), pltpu.VMEM((LANES, q), f32),
                        pltpu.VMEM((q, D_INNER), bf16), pltpu.VMEM((q, D_INNER), f32)],
        name="ssd_prompt", args=(z, xc, h1, wdt, dtb, alog, dsk, nw), carry=carry)


def _ssd_decode_kernel(z_ref, xbc_ref, h1_ref, cst_ref, s_ref, wdt_ref, cw_ref, cb_ref, dtb_ref, alog_ref, dsk_ref,
                       nw_ref, e_ref, y_ref, so_ref, xdtt_ref, yt_ref, *, bb):
    acc = cb_ref[...] + xbc_ref[...] * cw_ref[CONV_WIDTH - 1:CONV_WIDTH, :]
    for w in range(CONV_WIDTH - 1):
        acc = acc + cst_ref[w] * cw_ref[w:w + 1, :]
    xc = _silu(acc)
    xs = xc[:, :D_INNER]
    lane_h = lax.broadcasted_iota(jnp.int32, (bb, LANES), 1)
    dt = _softplus(_dot(h1_ref[...].astype(bf16), wdt_ref[...]) + dtb_ref[...])
    dt = jnp.where(lane_h < N_SSD_HEADS, dt, 0.0)
    decay = jnp.exp(dt * (-jnp.exp(alog_ref[...])))
    xdt = xs * _dot_exact(dt, e_ref[...])
    pad = lambda a: jnp.concatenate([a, jnp.zeros((LANES - bb, a.shape[1]), f32)], axis=0)
    xdtt_ref[...] = pad(xdt).T.astype(bf16)
    row = lax.broadcasted_iota(jnp.int32, (LANES, LANES), 0)
    lane = lax.broadcasted_iota(jnp.int32, (LANES, LANES), 1)
    for g in range(N_SSD_GROUPS):
        gs = slice(g * GROUP_WIDTH, (g + 1) * GROUP_WIDTH)
        b_pad = pad(xc[:, D_INNER + g * D_STATE:D_INNER + (g + 1) * D_STATE])
        c_pad_t = pad(xc[:, D_INNER + (N_SSD_GROUPS + g) * D_STATE:D_INNER + (N_SSD_GROUPS + g + 1) * D_STATE]).T
        y_t = jnp.zeros((GROUP_WIDTH, LANES), f32)
        for j in range(bb):
            outer = _dot(xdtt_ref[gs, :], jnp.where(row == j, b_pad, 0.0).astype(bf16))
            pieces = []
            for hh in range(HEADS_PER_GROUP):
                h = g * HEADS_PER_GROUP + hh
                rs = slice(hh * SSD_HEAD_DIM, (hh + 1) * SSD_HEAD_DIM)
                new = s_ref[j, g * GROUP_WIDTH + hh * SSD_HEAD_DIM:g * GROUP_WIDTH + (hh + 1) * SSD_HEAD_DIM, :] \
                    * decay[j:j + 1, h:h + 1] + outer[rs, :]
                so_ref[j, g * GROUP_WIDTH + hh * SSD_HEAD_DIM:g * GROUP_WIDTH + (hh + 1) * SSD_HEAD_DIM, :] = new
                pieces.append(new.astype(bf16))
            y_t = y_t + _dot(jnp.concatenate(pieces, axis=0), jnp.where(lane == j, c_pad_t, 0.0).astype(bf16))
        yt_ref[gs, :] = y_t
    y = yt_ref[...].T[:bb, :] + dsk_ref[...] * xs
    y = y * _silu(z_ref[...])
    y = y * lax.rsqrt(jnp.mean(y * y, axis=-1, keepdims=True) + RMS_EPS) * nw_ref[...]
    y_ref[...] = y


def _ssd_decode(z, xbc, h1, conv_state, ssm_state, wdt, cw, cb, dtb, alog, dsk, nw, e, *, bb):
    n = z.shape[0]
    return pl.pallas_call(
        functools.partial(_ssd_decode_kernel, bb=bb),
        grid=(n // bb,),
        in_specs=[pl.BlockSpec((bb, D_INNER), lambda i: (i, 0)), pl.BlockSpec((bb, CONV_DIM), lambda i: (i, 0)),
                  pl.BlockSpec((bb, D_MODEL), lambda i: (i, 0)),
                  pl.BlockSpec((CONV_WIDTH - 1, bb, CONV_DIM), lambda i: (0, i, 0)),
                  pl.BlockSpec((bb, D_INNER, D_STATE), lambda i: (i, 0, 0)),
                  _resident(wdt.shape), _resident(cw.shape), _resident(cb.shape), _resident(dtb.shape),
                  _resident(alog.shape), _resident(dsk.shape), _resident(nw.shape), _resident(e.shape)],
        out_specs=[pl.BlockSpec((bb, D_INNER), lambda i: (i, 0)),
                   pl.BlockSpec((bb, D_INNER, D_STATE), lambda i: (i, 0, 0))],
        out_shape=[jax.ShapeDtypeStruct((n, D_INNER), f32),
                   jax.ShapeDtypeStruct((n, D_INNER, D_STATE), f32)],
        scratch_shapes=[pltpu.VMEM((D_INNER, LANES), bf16), pltpu.VMEM((D_INNER, LANES), f32)],
        compiler_params=_cparams(1),
        name="ssd_decode",
    )(z, xbc, h1, jnp.swapaxes(conv_state, 0, 1), ssm_state, wdt, cw, cb, dtb, alog, dsk, nw, e)


def _attn_prompt_kernel(slope_ref, q_ref, k_ref, v_ref, o_ref, st_ref, kprev_ref, vprev_ref, *, dilation, rows):
    n = pl.program_id(2)
    blk = ATTN_BLOCK
    nb = rows // blk

    @pl.when(n == 0)
    def _():
        kprev_ref[...] = jnp.zeros_like(kprev_ref)
        vprev_ref[...] = jnp.zeros_like(vprev_ref)

    a = lax.broadcasted_iota(jnp.int32, (blk, blk), 0)
    c = lax.broadcasted_iota(jnp.int32, (blk, blk), 1)
    dist_prev = (dilation * (blk + a - c)).astype(f32)
    dist_cur = (dilation * (a - c)).astype(f32)
    valid_cur = c <= a
    valid_prev = c >= a
    valid_prev_first = (c - a + jnp.where(n > 0, 0, -2 * blk)) >= 0
    bias_cur, bias_prev, bias_prev_first = [], [], []
    for h in range(HEADS_PER_PATTERN):
        slope = slope_ref[h:h + 1, :]
        bias_cur.append(jnp.where(valid_cur, -slope * dist_cur, NEG_INF))
        bias_prev.append(jnp.where(valid_prev, -slope * dist_prev, NEG_INF))
        bias_prev_first.append(jnp.where(valid_prev_first, -slope * dist_prev, NEG_INF))

    qs, kcs, vcs, kps, vps, bcs, bps = [], [], [], [], [], [], []
    for i in range(nb):
        rs = slice(i * blk, (i + 1) * blk)
        ps = slice((i - 1) * blk, i * blk)
        for h in range(HEADS_PER_PATTERN):
            hs = slice(h * ATTN_HEAD_DIM, (h + 1) * ATTN_HEAD_DIM)
            qs.append(q_ref[rs, hs])
            kcs.append(k_ref[rs, hs])
            vcs.append(v_ref[rs, hs])
            kps.append(k_ref[ps, hs] if i > 0 else kprev_ref[:, hs])
            vps.append(v_ref[ps, hs] if i > 0 else vprev_ref[:, hs])
            bcs.append(bias_cur[h])
            bps.append(bias_prev[h] if i > 0 else bias_prev_first[h])
    q3 = jnp.stack(qs)
    scale = ATTN_HEAD_DIM ** -0.5
    qk = lambda x, y: jnp.einsum("bqd,bkd->bqk", x, y, preferred_element_type=f32)
    pv = lambda x, y: jnp.einsum("bqk,bkd->bqd", x, y, preferred_element_type=f32)
    s_p = qk(q3, jnp.stack(kps)) * scale + jnp.stack(bps)
    s_c = qk(q3, jnp.stack(kcs)) * scale + jnp.stack(bcs)
    m = jnp.maximum(jnp.max(s_p, axis=-1, keepdims=True), jnp.max(s_c, axis=-1, keepdims=True))
    p_p = jnp.exp(s_p - m)
    p_c = jnp.exp(s_c - m)
    l = jnp.sum(p_p, axis=-1, keepdims=True) + jnp.sum(p_c, axis=-1, keepdims=True)
    o = (pv(p_p.astype(bf16), jnp.stack(vps)) + pv(p_c.astype(bf16), jnp.stack(vcs))) / l
    lse = m + jnp.log(l)
    lane = lax.broadcasted_iota(jnp.int32, (blk, LANES), 1)
    for i in range(nb):
        rs = slice(i * blk, (i + 1) * blk)
        stat = jnp.zeros((blk, LANES), f32)
        for h in range(HEADS_PER_PATTERN):
            idx = i * HEADS_PER_PATTERN + h
            o_ref[rs, h * ATTN_HEAD_DIM:(h + 1) * ATTN_HEAD_DIM] = o[idx].astype(o_ref.dtype)
            stat = jnp.where(lane == h, lse[idx], stat)
        st_ref[rs, :] = stat
    kprev_ref[...] = k_ref[rows - blk:rows, :]
    vprev_ref[...] = v_ref[rows - blk:rows, :]


ATTN_ROWS = 1024


def _attn_prompt(qkv, slopes, g):
    window, dilation = ATTN_PATTERNS[g]
    assert window // dilation == ATTN_BLOCK
    batch, _, length, _ = qkv.shape
    rows = min(length, ATTN_ROWS)
    spec = lambda width, off: pl.BlockSpec((None, None, rows, width), lambda b, r, n: (b, r, n, off))
    return pl.pallas_call(
        functools.partial(_attn_prompt_kernel, dilation=dilation, rows=rows),
        grid=(batch, dilation, length // rows),
        in_specs=[_resident(slopes.shape), spec(ATTN_WIDTH, 0), spec(ATTN_WIDTH, 1), spec(ATTN_WIDTH, 2)],
        out_specs=[spec(ATTN_WIDTH, 0), spec(LANES, 0)],
        out_shape=[jax.ShapeDtypeStruct((batch, dilation, length, ATTN_WIDTH), bf16),
                   jax.ShapeDtypeStruct((batch, dilation, length, LANES), f32)],
        scratch_shapes=[pltpu.VMEM((ATTN_BLOCK, ATTN_WIDTH), bf16), pltpu.VMEM((ATTN_BLOCK, ATTN_WIDTH), bf16)],
        compiler_params=_cparams(3),
        name=f"attn_prompt_d{dilation}",
    )(slopes, qkv, qkv, qkv)


def _attn_decode_kernel(bias_ref, qkv_ref, c0_ref, c1_ref, c2_ref, o_ref):
    scale = ATTN_HEAD_DIM ** -0.5
    outs, ms, ls = [], [], []
    for g, c_ref in enumerate((c0_ref, c1_ref, c2_ref)):
        q = qkv_ref[:, g, 0]
        k_new = qkv_ref[:, g, 1]
        v_new = qkv_ref[:, g, 2]
        s = jnp.sum(c_ref[:, :, 0] * q[:, None], axis=-1, keepdims=True) * scale + bias_ref[g]
        s_new = jnp.sum(k_new * q, axis=-1, keepdims=True) * scale
        m = jnp.maximum(jnp.max(s, axis=1), s_new)
        p = jnp.exp(s - m[:, None])
        p_new = jnp.exp(s_new - m)
        l = jnp.sum(p, axis=1) + p_new
        o = (jnp.sum(p * c_ref[:, :, 1], axis=1) + p_new * v_new) / l
        outs.append(o)
        ms.append(m)
        ls.append(l)
    m_all = jnp.maximum(jnp.maximum(ms[0], ms[1]), ms[2])
    wgts = [l * jnp.exp(m - m_all) for m, l in zip(ms, ls)]
    num = wgts[0] * outs[0] + wgts[1] * outs[1] + wgts[2] * outs[2]
    o_ref[...] = num / (wgts[0] + wgts[1] + wgts[2])


def _attn_decode(qkv, caches, bias, *, bb):
    n = qkv.shape[0]
    in_specs = [_resident(bias.shape),
                pl.BlockSpec((bb, N_PATTERNS, 3, HEADS_PER_PATTERN, ATTN_HEAD_DIM), lambda i: (i, 0, 0, 0, 0))]
    views = []
    for g, (window, dilation) in enumerate(ATTN_PATTERNS):
        assert caches[g].shape[1] == window and window // dilation == ATTN_BLOCK
        views.append(caches[g].reshape(n, ATTN_BLOCK, dilation, 2, HEADS_PER_PATTERN, ATTN_HEAD_DIM))
        in_specs.append(pl.BlockSpec((bb, ATTN_BLOCK, None, 2, HEADS_PER_PATTERN, ATTN_HEAD_DIM),
                                     lambda i: (i, 0, 0, 0, 0, 0)))
    return pl.pallas_call(
        _attn_decode_kernel,
        grid=(n // bb,),
        in_specs=in_specs,
        out_specs=pl.BlockSpec((bb, HEADS_PER_PATTERN, ATTN_HEAD_DIM), lambda i: (i, 0, 0)),
        out_shape=jax.ShapeDtypeStruct((n, HEADS_PER_PATTERN, ATTN_HEAD_DIM), f32),
        compiler_params=_cparams(1),
        name="attn_decode",
    )(bias, qkv, *views)


def _post_mix_kernel(*refs, dilations):
    n_pat = len(dilations)
    merged = n_pat > 1
    h1_ref, y_ref = refs[0], refs[1]
    o_refs = refs[2:2 + n_pat]
    st_refs = refs[2 + n_pat:2 + 2 * n_pat] if merged else ()
    k = 2 + n_pat + len(st_refs)
    wg_ref, wso_ref, wao_ref, wmo_ref, lnw_ref, lnb_ref, out_ref = refs[k:k + 7]
    scratch = refs[k + 7:]
    tm = h1_ref.shape[0]
    h1 = h1_ref[...]
    gates = jax.nn.sigmoid(_dot(h1.astype(bf16), wg_ref[...]))
    if merged:
        o_heads, lses = [], []
        for g, d in enumerate(dilations):
            if d == 1:
                o_heads.append([o_refs[g][:, h * ATTN_HEAD_DIM:(h + 1) * ATTN_HEAD_DIM].astype(f32)
                                for h in range(HEADS_PER_PATTERN)])
                lses.append(st_refs[g])
                continue
            o_scr, st_scr = scratch[0], scratch[1]
            scratch = scratch[2:]
            for r in range(d):
                st_scr[pl.ds(r, tm // d, stride=d), :] = st_refs[g][r]
                for h in range(HEADS_PER_PATTERN):
                    o_scr[h, pl.ds(r, tm // d, stride=d), :] = (
                        o_refs[g][r, :, h * ATTN_HEAD_DIM:(h + 1) * ATTN_HEAD_DIM].astype(f32))
            o_heads.append([o_scr[h] for h in range(HEADS_PER_PATTERN)])
            lses.append(st_scr)
        heads = []
        for h in range(HEADS_PER_PATTERN):
            lse = [st[:, h:h + 1] for st in lses]
            top = functools.reduce(jnp.maximum, lse)
            wgt = [jnp.exp(v - top) for v in lse]
            num = sum(w * o[h] for w, o in zip(wgt, o_heads))
            heads.append(num / sum(wgt))
        attn = jnp.concatenate(heads, axis=1)
    else:
        attn = o_refs[0][...]
    t_ssd = _dot(y_ref[...].astype(bf16), wso_ref[...])
    t_attn = _dot(attn.astype(bf16), wao_ref[...])
    u = gates[:, :D_MODEL] * t_ssd + gates[:, D_MODEL:] * t_attn
    mix = _dot(u.astype(bf16), wmo_ref[...])
    out_ref[...] = _layer_norm(DEEPNORM_ALPHA * h1 + mix, lnw_ref[...], lnb_ref[...])


def _post_mix(h1, y_ssd, outs, stats, dilations, wg, wso, wao, wmo, lnw, lnb, *, tm, seq, carry=None):
    m = h1.shape[0]
    steps = seq // tm
    row = lambda width: pl.BlockSpec((tm, width), lambda i: (i, 0))

    def pattern_spec(width, d):
        if d == 1:
            return row(width)
        return pl.BlockSpec((None, d, tm // d, width), lambda i: (i // steps, 0, i % steps, 0))

    in_specs = [row(D_MODEL), row(D_INNER)] + [pattern_spec(ATTN_WIDTH, d) for d in dilations]
    in_specs += [pattern_spec(LANES, d) for d in dilations[:len(stats)]]
    in_specs += [_resident(w.shape) for w in (wg, wso, wao, wmo, lnw, lnb)]
    scratch = []
    for d in dilations:
        if d > 1:
            scratch += [pltpu.VMEM((HEADS_PER_PATTERN, tm, ATTN_HEAD_DIM), f32), pltpu.VMEM((tm, LANES), f32)]
    return _call(functools.partial(_post_mix_kernel, dilations=tuple(dilations)), grid=(m // tm,), in_specs=in_specs,
                 out_specs=[row(D_MODEL)], out_shape=[jax.ShapeDtypeStruct((m, D_MODEL), f32)],
                 scratch_shapes=scratch, name="post_mix",
                 args=(h1, y_ssd, *outs, *stats, wg, wso, wao, wmo, lnw, lnb), carry=carry)


SHIFT_CHUNK = 256
SHIFT_BLOCK_BYTES = 8 * 1024 * 1024


def _kv_shift_kernel(c_ref, new_ref, o_ref, *, key_rows):
    bb, rows, _ = c_ref.shape
    keep = rows - key_rows
    n_chunks = pl.cdiv(keep, SHIFT_CHUNK)
    for b in range(bb):
        def body(k, carry):
            r0 = pl.multiple_of(jnp.minimum(k * SHIFT_CHUNK, keep - SHIFT_CHUNK), 8)
            o_ref[b, pl.ds(r0, SHIFT_CHUNK), :] = c_ref[b, pl.ds(r0 + key_rows, SHIFT_CHUNK), :]
            return carry
        lax.fori_loop(0, n_chunks, body, 0)
        o_ref[b, keep:rows, :] = new_ref[b]


def _kv_shift(cache, new):
    n, w = cache.shape[:2]
    key_rows = 2 * HEADS_PER_PATTERN * ATTN_HEAD_DIM // LANES
    rows = w * key_rows
    assert key_rows % 8 == 0 and rows - key_rows >= SHIFT_CHUNK
    bb = max(1, min(n, SHIFT_BLOCK_BYTES // (rows * LANES * 4)))
    assert n % bb == 0
    out = pl.pallas_call(
        functools.partial(_kv_shift_kernel, key_rows=key_rows),
        grid=(n // bb,),
        in_specs=[pl.BlockSpec((bb, rows, LANES), lambda i: (i, 0, 0)),
                  pl.BlockSpec((bb, key_rows, LANES), lambda i: (i, 0, 0))],
        out_specs=pl.BlockSpec((bb, rows, LANES), lambda i: (i, 0, 0)),
        out_shape=jax.ShapeDtypeStruct((n, rows, LANES), cache.dtype),
        compiler_params=_cparams(1),
        name=f"kv_shift_w{w}",
    )(cache.reshape(n, rows, LANES), new.reshape(n, key_rows, LANES))
    return out.reshape(cache.shape)


def _alibi_slopes():
    n = N_PATTERNS * HEADS_PER_PATTERN
    e = jnp.arange(1, n + 1, dtype=f32)
    return (2.0 ** (-ALIBI_MAX_EXP * e / n)).reshape(N_PATTERNS, HEADS_PER_PATTERN)


def _prepare_weights(ln_w, ln_b, ffn_w13, ffn_w2, w_in, conv_w, conv_b, dt_bias, a_log, d_skip, ssd_norm_w,
                     w_ssd_out, w_attn_out, w_mix_out, w_pe_gate, w_pe_proj):
    cuts = [0]
    for s in IN_SPLITS:
        cuts.append(cuts[-1] + s)
    wz, wxbc, wdt, wq, wk, wv = (w_in[:, cuts[i]:cuts[i + 1]] for i in range(6))
    pad_heads = LANES - N_SSD_HEADS
    head_of_col = jnp.arange(D_INNER, dtype=jnp.int32) // SSD_HEAD_DIM
    return dict(
        ln_w=[ln_w[i:i + 1] for i in range(3)], ln_b=[ln_b[i:i + 1] for i in range(3)],
        w13=[ffn_w13[i].astype(bf16) for i in range(2)], w2=[ffn_w2[i].astype(bf16) for i in range(2)],
        wz=wz.astype(bf16), wxbc=wxbc.astype(bf16),
        wdt=jnp.pad(wdt, ((0, 0), (0, pad_heads))).astype(bf16),
        wqkv=jnp.concatenate([m[:, g * ATTN_WIDTH:(g + 1) * ATTN_WIDTH] for g in range(N_PATTERNS)
                              for m in (wq, wk, wv)], axis=1).astype(bf16),
        wg=w_in[:, cuts[6]:].astype(bf16),
        cw=conv_w, cb=conv_b[None, :],
        dtb=jnp.pad(dt_bias, (0, pad_heads))[None, :], alog=jnp.pad(a_log, (0, pad_heads))[None, :],
        dsk=jnp.repeat(d_skip, SSD_HEAD_DIM)[None, :], nw=ssd_norm_w[None, :],
        expand=(jnp.arange(LANES, dtype=jnp.int32)[:, None] == head_of_col[None, :]).astype(f32),
        wso=w_ssd_out.astype(bf16), wao=w_attn_out.astype(bf16), wmo=w_mix_out.astype(bf16),
        wpg=w_pe_gate.astype(bf16), wpp=w_pe_proj.astype(bf16),
    )


def _ssd_args(w):
    return (w["wdt"], w["cw"], w["cb"], w["dtb"], w["alog"], w["dsk"], w["nw"], w["expand"])


class _CacheCarrier:
    def __init__(self, cache, new):
        n, w = cache.shape[:2]
        self.shape = cache.shape
        self.key_rows = 2 * HEADS_PER_PATTERN * ATTN_HEAD_DIM // LANES
        self.cache = cache.reshape(n, w * self.key_rows, LANES)
        self.new = new.reshape(n, self.key_rows, LANES)
        self.out, self.done = None, 0

    def take(self, steps, parts):
        samples = min(steps // parts, self.cache.shape[0] - self.done)
        if samples <= 0:
            return None
        return (self.cache, self.new, self.out, self.done * parts, samples * parts, parts)

    def gave(self, carry, out):
        if carry is not None:
            self.out, self.done = out, self.done + carry[4] // carry[5]

    def result(self):
        assert self.done == self.cache.shape[0]
        return self.out.reshape(self.shape)


def _prompt_layer(x, p, w, big_carrier, mid_carrier):
    batch, seq, _ = x.shape
    assert seq % SSD_CHUNK == 0 and all(seq % (ATTN_BLOCK * d) == 0 and seq >= win for win, d in ATTN_PATTERNS)
    tm = 512
    x2 = x.reshape(batch * seq, D_MODEL)

    def carried(carrier, n_steps, parts, fn, *args, **kwargs):
        carry = carrier.take(n_steps, parts) if carrier is not None else None
        outs, cache_out = fn(*args, carry=carry, **kwargs)
        if carrier is not None:
            carrier.gave(carry, cache_out)
        return outs

    (h1,), _ = _ffn_ln(x2, w["w13"][0], w["w2"][0], w["ln_w"][0], w["ln_b"][0], tm=tm)
    z, xc, xbc_tail = carried(big_carrier, batch * seq // PROJ_ZX_ROWS, 1, _proj_zx, h1, w["wz"], w["wxbc"], w["cw"],
                              w["cb"], batch=batch, seq=seq, tm=PROJ_ZX_ROWS)
    qkv = carried(big_carrier, batch * seq // tm, 1, _qkv_proj, h1, w["wqkv"], batch=batch, seq=seq, tm=tm)
    y_ssd, new_ssm = carried(mid_carrier, batch * seq // SSD_CHUNK, 1, _ssd_prompt, z, xc, h1, w["wdt"], w["dtb"],
                             w["alog"], w["dsk"], w["nw"], batch=batch)
    slopes = _alibi_slopes()
    dilations = [d for _, d in ATTN_PATTERNS]
    outs, stats, new_kv = [], [], []
    for g, (win, d) in enumerate(ATTN_PATTERNS):
        o, st = _attn_prompt(qkv[g], jnp.broadcast_to(slopes[g][:, None], (HEADS_PER_PATTERN, LANES)), g)
        if d == 1:
            o, st = o.reshape(batch * seq, ATTN_WIDTH), st.reshape(batch * seq, LANES)
        outs.append(o)
        stats.append(st)
        tail = qkv[g][:, :, seq // d - win // d:, ATTN_WIDTH:]
        new_kv.append(jnp.swapaxes(tail, 1, 2).reshape(batch, win, 2, HEADS_PER_PATTERN, ATTN_HEAD_DIM).astype(f32))
    (h2,), _ = _post_mix(h1, y_ssd, outs, stats, dilations, w["wg"], w["wso"], w["wao"], w["wmo"], w["ln_w"][1],
                         w["ln_b"][1], tm=tm, seq=seq)
    y, = carried(big_carrier, batch * seq // tm, 1, _ffn_ln, h2, w["w13"][1], w["w2"][1], w["ln_w"][2], w["ln_b"][2],
                 pe=(p.reshape(batch * seq, PLE_DIM), w["wpg"], w["wpp"]), tm=tm)
    new_conv = xbc_tail[:, 8 - (CONV_WIDTH - 1):]
    return (y.reshape(batch, seq, D_MODEL), new_ssm.reshape(batch, N_SSD_HEADS, SSD_HEAD_DIM, D_STATE),
            new_conv, new_kv)


def _decode_layer(x, p, ssm_state, conv_state, caches, w):
    n, t, _ = x.shape
    assert t == 1
    tm = n
    bb = 8
    (h1,), _ = _ffn_ln(x.reshape(n, D_MODEL), w["w13"][0], w["w2"][0], w["ln_w"][0], w["ln_b"][0], tm=tm)
    z = _matmul(h1, w["wz"], f32, tm=tm)
    xbc = _matmul(h1, w["wxbc"], f32, tm=tm)
    qkv = _matmul(h1, w["wqkv"], f32, tm=tm)
    y_ssd, new_ssm = _ssd_decode(z, xbc, h1, conv_state, ssm_state.reshape(n, D_INNER, D_STATE), *_ssd_args(w), bb=bb)
    slopes = _alibi_slopes()
    steps_back = (ATTN_BLOCK - jnp.arange(ATTN_BLOCK, dtype=jnp.int32))
    bias = jnp.stack([-slopes[g][None, :] * (d * steps_back).astype(f32)[:, None]
                      for g, (_, d) in enumerate(ATTN_PATTERNS)])
    bias = jnp.broadcast_to(bias[..., None], bias.shape + (1,))
    qkv5 = qkv.reshape(n, N_PATTERNS, 3, HEADS_PER_PATTERN, ATTN_HEAD_DIM)
    attn = _attn_decode(qkv5, caches, bias, bb=bb).reshape(n, ATTN_WIDTH)
    (h2,), _ = _post_mix(h1, y_ssd, [attn], [], [1], w["wg"], w["wso"], w["wao"], w["wmo"], w["ln_w"][1],
                         w["ln_b"][1], tm=tm, seq=n)
    (y,), _ = _ffn_ln(h2, w["w13"][1], w["w2"][1], w["ln_w"][2], w["ln_b"][2],
                      pe=(p.reshape(n, PLE_DIM), w["wpg"], w["wpp"]), tm=tm)
    new_conv = jnp.concatenate([conv_state[:, 1:], xbc[:, None, :]], axis=1)
    new_rows = [qkv5[:, g, 1:3] for g in range(N_PATTERNS)]
    return y.reshape(n, 1, D_MODEL), new_ssm.reshape(n, N_SSD_HEADS, SSD_HEAD_DIM, D_STATE), new_conv, new_rows


PROJ_ZX_ROWS = 256


def _carrier_capacity(batch, seq):
    return batch * seq // PROJ_ZX_ROWS + 2 * (batch * seq // 512), batch * seq // SSD_CHUNK


def kernel(x_prompt, x_sample, state_ssm, state_conv, cache_kv_w128, cache_kv_w512, cache_kv_w2048, p_prompt, p_sample, ln_w, ln_b, ffn_w13, ffn_w2, w_in, conv_w, conv_b, dt_bias, a_log, d_skip, ssd_norm_w, w_ssd_out, w_attn_out, w_mix_out, w_pe_gate, w_pe_proj):
    assert ln_w.shape[0] == DEPTH
    w = _prepare_weights(ln_w[0], ln_b[0], ffn_w13[0], ffn_w2[0], w_in[0], conv_w[0], conv_b[0], dt_bias[0], a_log[0],
                         d_skip[0], ssd_norm_w[0], w_ssd_out[0], w_attn_out[0], w_mix_out[0], w_pe_gate[0],
                         w_pe_proj[0])
    caches = (cache_kv_w128[0], cache_kv_w512[0], cache_kv_w2048[0])
    y_s, ssm_s, conv_s, new_rows = _decode_layer(x_sample, p_sample[0], state_ssm[0], state_conv[0], caches, w)
    order = sorted(range(N_PATTERNS), key=lambda g: -caches[g].shape[1])
    big, mid = order[0], order[1]
    cap_big, cap_mid = _carrier_capacity(x_prompt.shape[0], x_prompt.shape[1])
    big_carrier = _CacheCarrier(caches[big], new_rows[big]) if caches[big].shape[0] <= cap_big else None
    mid_carrier = _CacheCarrier(caches[mid], new_rows[mid]) if caches[mid].shape[0] <= cap_mid else None
    y_p, ssm_p, conv_p, kv_p = _prompt_layer(x_prompt, p_prompt[0], w, big_carrier, mid_carrier)
    carriers = {big: big_carrier, mid: mid_carrier}
    kv_s = [carriers[g].result() if carriers.get(g) is not None else _kv_shift(caches[g], new_rows[g])
            for g in range(N_PATTERNS)]
    lead = lambda a: a[None]
    return (y_p, y_s, lead(ssm_p), lead(conv_p), lead(kv_p[0]), lead(kv_p[1]), lead(kv_p[2]),
            lead(ssm_s), lead(conv_s), lead(kv_s[0]), lead(kv_s[1]), lead(kv_s[2]))
```

```python
import functools

import jax
import jax.numpy as jnp
from jax import lax
from jax.experimental import pallas as pl
from jax.experimental.pallas import tpu as pltpu

f32 = jnp.float32
bf16 = jnp.bfloat16

D_MODEL = 1024
D_INNER = 2 * D_MODEL
SSD_HEAD_DIM = 64
N_SSD_HEADS = D_INNER // SSD_HEAD_DIM
N_SSD_GROUPS = 4
HEADS_PER_GROUP = N_SSD_HEADS // N_SSD_GROUPS
GROUP_WIDTH = HEADS_PER_GROUP * SSD_HEAD_DIM
D_STATE = 128
CONV_WIDTH = 4
CONV_DIM = D_INNER + 2 * N_SSD_GROUPS * D_STATE
SSD_CHUNK = 128
ATTN_PATTERNS = ((128, 1), (512, 4), (2048, 16))
N_PATTERNS = 3
HEADS_PER_PATTERN = 4
ATTN_HEAD_DIM = 128
ATTN_WIDTH = HEADS_PER_PATTERN * ATTN_HEAD_DIM
ATTN_QKV_WIDTH = N_PATTERNS * ATTN_WIDTH
PATTERN_QKV_WIDTH = 3 * ATTN_WIDTH
ATTN_BLOCK = 128
ALIBI_MAX_EXP = 8.0
D_FF = 2816
PLE_DIM = 256
LN_EPS = 1e-5
RMS_EPS = 1e-5
DEPTH = 1
DEEPNORM_ALPHA = (2.0 * DEPTH) ** 0.25
IN_SPLITS = (D_INNER, CONV_DIM, N_SSD_HEADS, ATTN_QKV_WIDTH, ATTN_QKV_WIDTH, ATTN_QKV_WIDTH)

LANES = 128
VMEM_LIMIT = 56 * 1024 * 1024
NEG_INF = float("-inf")


def _cparams(n_axes):
    return pltpu.CompilerParams(dimension_semantics=("arbitrary",) * n_axes, vmem_limit_bytes=VMEM_LIMIT)


def _resident(shape):
    nd = len(shape)
    return pl.BlockSpec(shape, lambda *_: (0,) * nd, pipeline_mode=pl.Buffered(1))


def _layer_norm(y, w, b):
    mu = jnp.mean(y, axis=-1, keepdims=True)
    yc = y - mu
    var = jnp.mean(yc * yc, axis=-1, keepdims=True)
    return yc * lax.rsqrt(var + LN_EPS) * w + b


def _silu(x):
    return x * jax.nn.sigmoid(x)


def _softplus(x):
    return jnp.maximum(x, 0.0) + jnp.log1p(jnp.exp(-jnp.abs(x)))


def _dot(a, b):
    return jnp.dot(a, b, preferred_element_type=f32)


def _dot_nt(a, b):
    return lax.dot_general(a, b, (((1,), (1,)), ((), ())), preferred_element_type=f32)


def _dot_tn(a, b):
    return lax.dot_general(a, b, (((0,), (0,)), ((), ())), preferred_element_type=f32)


def _dot_exact(a, b):
    return jnp.dot(a, b, precision=lax.Precision.HIGHEST, preferred_element_type=f32)


def _carry_where(chunk, base, parts, cache_ref, new_ref, out_ref, buf_ref, sems):
    c = base + chunk
    part = lax.rem(c, parts)
    return lax.rem(chunk, 2), c // parts, part, pl.multiple_of(part * buf_ref.shape[1], 8)


def _carry_in(chunk, op, *refs):
    _, parts, cache_ref, new_ref, _, buf_ref, sems = refs
    key_rows, part_rows = new_ref.shape[1], buf_ref.shape[1]
    slot, b, part, r0 = _carry_where(chunk, *refs)

    @pl.when(part < parts - 1)
    def _():
        op(pltpu.make_async_copy(cache_ref.at[b, pl.ds(r0 + key_rows, part_rows)], buf_ref.at[slot], sems.at[0, slot]))

    @pl.when(part == parts - 1)
    def _():
        op(pltpu.make_async_copy(cache_ref.at[b, pl.ds(r0 + key_rows, part_rows - key_rows)],
                                 buf_ref.at[slot, pl.ds(0, part_rows - key_rows)], sems.at[0, slot]))
        op(pltpu.make_async_copy(new_ref.at[b], buf_ref.at[slot, pl.ds(part_rows - key_rows, key_rows)],
                                 sems.at[1, slot]))


def _carry_out(chunk, op, *refs):
    _, _, _, _, out_ref, buf_ref, sems = refs
    slot, b, _, r0 = _carry_where(chunk, *refs)
    op(pltpu.make_async_copy(buf_ref.at[slot], out_ref.at[b, pl.ds(r0, buf_ref.shape[1])], sems.at[2, slot]))


def _start(copy):
    copy.start()


def _wait(copy):
    copy.wait()


def _carry_begin(step, n_chunks, *refs):
    @pl.when(step == 0)
    def _():
        _carry_in(step, _start, *refs)

    @pl.when(step < n_chunks)
    def _():
        _carry_in(step, _wait, *refs)
        _carry_out(step, _start, *refs)

        @pl.when(step > 0)
        def _():
            _carry_out(step - 1, _wait, *refs)

        @pl.when(step + 1 < n_chunks)
        def _():
            _carry_in(step + 1, _start, *refs)


def _carry_end(step, n_chunks, *refs):
    @pl.when(step == n_chunks - 1)
    def _():
        _carry_out(step, _wait, *refs)


def _call(body, *, grid, in_specs, out_specs, out_shape, scratch_shapes=(), name, args, carry=None):
    out_specs, out_shape = list(out_specs), list(out_shape)
    if carry is None:
        res = pl.pallas_call(body, grid=grid, in_specs=list(in_specs), out_specs=out_specs, out_shape=out_shape,
                             scratch_shapes=list(scratch_shapes), compiler_params=_cparams(len(grid)), name=name)(*args)
        return list(res), None
    cache, new, prev, base, count, parts = carry
    part_rows = cache.shape[1] // parts
    assert cache.shape[1] % parts == 0 and part_rows % 8 == 0 and part_rows > new.shape[1]
    assert len(grid) == 1 and 0 < count <= grid[0]
    n_in, n_out, n_scr = len(in_specs), len(out_specs), len(scratch_shapes)
    n_cin = 2 if prev is None else 3

    def kernel(*refs):
        ins = refs[:n_in]
        cache_ref, new_ref = refs[n_in:n_in + 2]
        outs = refs[n_in + n_cin:n_in + n_cin + n_out]
        out_cache_ref = refs[n_in + n_cin + n_out]
        scratch = refs[n_in + n_cin + n_out + 1:n_in + n_cin + n_out + 1 + n_scr]
        buf_ref, sems = refs[-2:]
        step = pl.program_id(0)
        carry_refs = (base, parts, cache_ref, new_ref, out_cache_ref, buf_ref, sems)
        body(*ins, *outs, *scratch)
        _carry_begin(step, count, *carry_refs)
        _carry_end(step, count, *carry_refs)

    any_spec = pl.BlockSpec(memory_space=pl.ANY)
    cargs = (cache, new) if prev is None else (cache, new, prev)
    res = pl.pallas_call(
        kernel, grid=grid,
        in_specs=list(in_specs) + [any_spec] * n_cin,
        out_specs=out_specs + [any_spec],
        out_shape=out_shape + [jax.ShapeDtypeStruct(cache.shape, cache.dtype)],
        scratch_shapes=list(scratch_shapes) + [pltpu.VMEM((2, part_rows, cache.shape[2]), cache.dtype),
                                               pltpu.SemaphoreType.DMA((3, 2))],
        input_output_aliases={} if prev is None else {n_in + 2: n_out},
        compiler_params=_cparams(1), name=name + "_carry")(*args, *cargs)
    return list(res[:n_out]), res[n_out]


FFN_CHUNKS = (512, 512, 512, 512, 512, 256)


def _ffn_ln_kernel(*refs, with_pe):
    if with_pe:
        x_ref, w13_ref, w2_ref, lnw_ref, lnb_ref, p_ref, wpg_ref, wpp_ref, o_ref = refs
    else:
        x_ref, w13_ref, w2_ref, lnw_ref, lnb_ref, o_ref = refs
    x = x_ref[...]
    xb = x.astype(bf16)
    acc = None
    c0 = 0
    for width in FFN_CHUNKS:
        gate = _dot(xb, w13_ref[:, c0:c0 + width])
        up = _dot(xb, w13_ref[:, D_FF + c0:D_FF + c0 + width])
        part = _dot((_silu(gate) * up).astype(bf16), w2_ref[c0:c0 + width, :])
        acc = part if acc is None else acc + part
        c0 += width
    h = _layer_norm(DEEPNORM_ALPHA * x + 0.5 * acc, lnw_ref[...], lnb_ref[...])
    if with_pe:
        pe_gate = jax.nn.sigmoid(_dot(h.astype(bf16), wpg_ref[...]))
        h = h + pe_gate * _dot(p_ref[...].astype(bf16), wpp_ref[...])
    o_ref[...] = h


def _ffn_ln(x, w13, w2, lnw, lnb, pe=None, *, tm, carry=None):
    m = x.shape[0]
    assert sum(FFN_CHUNKS) == D_FF
    row = lambda width: pl.BlockSpec((tm, width), lambda i: (i, 0))
    in_specs = [row(D_MODEL), _resident(w13.shape), _resident(w2.shape), _resident(lnw.shape), _resident(lnb.shape)]
    args = [x, w13, w2, lnw, lnb]
    if pe is not None:
        p, wpg, wpp = pe
        in_specs += [row(PLE_DIM), _resident(wpg.shape), _resident(wpp.shape)]
        args += [p, wpg, wpp]
    return _call(functools.partial(_ffn_ln_kernel, with_pe=pe is not None), grid=(m // tm,), in_specs=in_specs,
                 out_specs=[row(D_MODEL)], out_shape=[jax.ShapeDtypeStruct((m, D_MODEL), f32)],
                 name="ffn_ln_pe" if pe is not None else "ffn_ln", args=args, carry=carry)


def _matmul_kernel(x_ref, w_ref, o_ref):
    o_ref[...] = _dot(x_ref[...].astype(bf16), w_ref[...]).astype(o_ref.dtype)


def _matmul(x, w, out_dtype, *, tm):
    m, k = x.shape
    n = w.shape[1]
    return pl.pallas_call(
        _matmul_kernel,
        grid=(m // tm,),
        in_specs=[pl.BlockSpec((tm, k), lambda i: (i, 0)), _resident((k, n))],
        out_specs=pl.BlockSpec((tm, n), lambda i: (i, 0)),
        out_shape=jax.ShapeDtypeStruct((m, n), out_dtype),
        compiler_params=_cparams(1),
        name="proj_matmul",
    )(x, w)


def _qkv_proj_kernel(x_ref, w_ref, *rest):
    outs, res_ref = rest[:N_PATTERNS], rest[N_PATTERNS]
    tm = x_ref.shape[0]
    xb = x_ref[...].astype(bf16)
    wide = 2 * LANES
    for c2 in range(N_PATTERNS * PATTERN_QKV_WIDTH // wide):
        res = _dot(xb, w_ref[:, c2 * wide:(c2 + 1) * wide])
        res_ref[2 * c2] = res[:, :LANES]
        res_ref[2 * c2 + 1] = res[:, LANES:]
    blocks = PATTERN_QKV_WIDTH // LANES
    for g, (_, d) in enumerate(ATTN_PATTERNS):
        for cb in range(blocks):
            cs = slice(cb * LANES, (cb + 1) * LANES)
            for r in range(d):
                outs[g][r, :, cs] = res_ref[g * blocks + cb, pl.ds(r, tm // d, stride=d), :].astype(bf16)


def _qkv_proj(x, w, *, batch, seq, tm, carry=None):
    m, k = x.shape
    steps = seq // tm
    return _call(
        _qkv_proj_kernel,
        grid=(m // tm,),
        in_specs=[pl.BlockSpec((tm, k), lambda i: (i, 0)), _resident(w.shape)],
        out_specs=[pl.BlockSpec((None, d, tm // d, PATTERN_QKV_WIDTH), lambda i: (i // steps, 0, i % steps, 0))
                   for _, d in ATTN_PATTERNS],
        out_shape=[jax.ShapeDtypeStruct((batch, d, seq // d, PATTERN_QKV_WIDTH), bf16) for _, d in ATTN_PATTERNS],
        scratch_shapes=[pltpu.VMEM((N_PATTERNS * PATTERN_QKV_WIDTH // LANES, tm, LANES), f32)],
        name="qkv_proj", args=(x, w), carry=carry)


CONV_ROWS = 64


def _proj_zx_kernel(x_ref, wz_ref, wx_ref, cw_ref, cb_ref, z_ref, xc_ref, tail_ref, xpad_ref, *, steps):
    i = pl.program_id(0)
    tm = x_ref.shape[0]
    halo = 8
    xb = x_ref[...].astype(bf16)
    z_ref[...] = _dot(xb, wz_ref[...]).astype(z_ref.dtype)

    @pl.when(i % steps == 0)
    def _():
        xpad_ref[0:halo, :] = jnp.zeros((halo, CONV_DIM), f32)

    for j in range(CONV_DIM // 512):
        xpad_ref[halo:halo + tm, j * 512:(j + 1) * 512] = _dot(xb, wx_ref[:, j * 512:(j + 1) * 512])
        for cb in range(2 * j, 2 * j + 2):
            cs = slice(cb * 256, (cb + 1) * 256)
            taps = [cw_ref[w:w + 1, cs] for w in range(CONV_WIDTH)]
            bias = cb_ref[:, cs]
            for r0 in range(0, tm, CONV_ROWS):
                blk = xpad_ref[r0:r0 + halo + CONV_ROWS, cs]
                acc = bias + blk[halo:] * taps[CONV_WIDTH - 1]
                for k in range(1, CONV_WIDTH):
                    acc = acc + pltpu.roll(blk, k, 0)[halo:] * taps[CONV_WIDTH - 1 - k]
                xc_ref[r0:r0 + CONV_ROWS, cs] = _silu(acc).astype(xc_ref.dtype)
    tail_ref[...] = xpad_ref[tm:tm + halo, :]
    xpad_ref[0:halo, :] = xpad_ref[tm:tm + halo, :]


def _proj_zx(x, wz, wx, cw, cb, *, batch, seq, tm, carry=None):
    m, k = x.shape
    steps = seq // tm
    return _call(
        functools.partial(_proj_zx_kernel, steps=steps),
        grid=(m // tm,),
        in_specs=[pl.BlockSpec((tm, k), lambda i: (i, 0)), _resident(wz.shape), _resident(wx.shape),
                  _resident(cw.shape), _resident(cb.shape)],
        out_specs=[pl.BlockSpec((tm, D_INNER), lambda i: (i, 0)), pl.BlockSpec((tm, CONV_DIM), lambda i: (i, 0)),
                   pl.BlockSpec((None, 8, CONV_DIM), lambda i: (i // steps, 0, 0))],
        out_shape=[jax.ShapeDtypeStruct((m, D_INNER), bf16), jax.ShapeDtypeStruct((m, CONV_DIM), bf16),
                   jax.ShapeDtypeStruct((batch, 8, CONV_DIM), f32)],
        scratch_shapes=[pltpu.VMEM((tm + 8, CONV_DIM), f32)],
        name="proj_zx", args=(x, wz, wx, cw, cb), carry=carry)


def _split3(x):
    hi = x.astype(bf16)
    r = x - hi.astype(f32)
    mid = r.astype(bf16)
    return hi, mid, (r - mid.astype(f32)).astype(bf16)


def _ssd_prompt_kernel(z_ref, xc_ref, h1_ref, wdt_ref, dtb_ref, alog_ref, dsk_ref, nw_ref, y_ref, so_ref,
                       st_ref, dt_ref, cum_ref, cumt_ref, xw_ref, yacc_ref, *, nc):
    c = pl.program_id(0) % nc
    q = SSD_CHUNK

    @pl.when(c == 0)
    def _():
        st_ref[...] = jnp.zeros_like(st_ref)

    lane = lax.broadcasted_iota(jnp.int32, (q, LANES), 1)
    row = lax.broadcasted_iota(jnp.int32, (q, LANES), 0)
    dt = _softplus(_dot(h1_ref[...].astype(bf16), wdt_ref[...]) + dtb_ref[...])
    dt = jnp.where(lane < N_SSD_HEADS, dt, 0.0)
    d_a = dt * (-jnp.exp(alog_ref[...]))
    causal = row >= lane
    c3 = _dot(causal.astype(bf16), jnp.concatenate(_split3(d_a), axis=1))
    cum = c3[:, :LANES] + c3[:, LANES:2 * LANES] + c3[:, 2 * LANES:]
    dt_ref[...] = dt
    cum_ref[...] = cum
    cumt_ref[...] = cum.T

    lo_half = lane < SSD_HEAD_DIM
    lo_half_row = lo_half[0:1, :]
    for g in range(N_SSD_GROUPS):
        b_g = xc_ref[:, D_INNER + g * D_STATE:D_INNER + (g + 1) * D_STATE]
        c_g = xc_ref[:, D_INNER + (N_SSD_GROUPS + g) * D_STATE:D_INNER + (N_SSD_GROUPS + g + 1) * D_STATE]
        cb = _dot_nt(c_g, b_g)
        y_off = _dot(c_g, st_ref[g].astype(bf16))
        chunk_decay = []
        for jj in range(HEADS_PER_GROUP // 2):
            j = g * (HEADS_PER_GROUP // 2) + jj
            cs = slice(j * LANES, (j + 1) * LANES)
            ha, hb = 2 * j, 2 * j + 1
            x_p = xc_ref[:, cs].astype(f32)
            cum_a, cum_b = cum_ref[:, ha:ha + 1], cum_ref[:, hb:hb + 1]
            cume = jnp.where(lo_half, cum_a, cum_b)
            cume_last = jnp.where(lo_half_row, cum_a[q - 1:q, :], cum_b[q - 1:q, :])
            xdt = x_p * jnp.where(lo_half, dt_ref[:, ha:ha + 1], dt_ref[:, hb:hb + 1])
            ls = []
            for h, cum_h in ((ha, cum_a), (hb, cum_b)):
                seg = cum_h - cumt_ref[h:h + 1, :]
                ls.append(jnp.exp(jnp.where(causal, seg, NEG_INF)) * cb)
            l2 = jnp.concatenate(ls, axis=1).astype(bf16)
            x2 = jnp.concatenate([jnp.where(lo_half, xdt, 0.0), jnp.where(lo_half, 0.0, xdt)], axis=0).astype(bf16)
            y = _dot(l2, x2) + y_off[:, jj * LANES:(jj + 1) * LANES] * jnp.exp(cume) + dsk_ref[:, cs] * x_p
            yacc_ref[:, cs] = y * _silu(z_ref[:, cs].astype(f32))
            xw_ref[:, cs] = (xdt * jnp.exp(cume_last - cume)).astype(bf16)
            chunk_decay.append(jnp.exp(cume_last))
        gs = slice(g * GROUP_WIDTH, (g + 1) * GROUP_WIDTH)
        st_ref[g] = st_ref[g] * jnp.concatenate(chunk_decay, axis=1) + _dot_tn(b_g, xw_ref[:, gs])

    y = yacc_ref[...]
    y = y * lax.rsqrt(jnp.mean(y * y, axis=-1, keepdims=True) + RMS_EPS) * nw_ref[...]
    y_ref[...] = y.astype(y_ref.dtype)

    @pl.when(c == nc - 1)
    def _():
        for g in range(N_SSD_GROUPS):
            so_ref[g * GROUP_WIDTH:(g + 1) * GROUP_WIDTH, :] = st_ref[g].T


def _ssd_prompt(z, xc, h1, wdt, dtb, alog, dsk, nw, *, batch, carry=None):
    m = z.shape[0]
    nc = m // batch // SSD_CHUNK
    q = SSD_CHUNK
    row_map = lambda i: (i, 0)
    return _call(
        functools.partial(_ssd_prompt_kernel, nc=nc),
        grid=(batch * nc,),
        in_specs=[pl.BlockSpec((q, D_INNER), row_map), pl.BlockSpec((q, CONV_DIM), row_map),
                  pl.BlockSpec((q, D_MODEL), row_map),
                  _resident(wdt.shape), _resident(dtb.shape), _resident(alog.shape), _resident(dsk.shape),
                  _resident(nw.shape)],
        out_specs=[pl.BlockSpec((q, D_INNER), row_map),
                   pl.BlockSpec((None, D_INNER, D_STATE), lambda i: (i // nc, 0, 0))],
        out_shape=[jax.ShapeDtypeStruct((m, D_INNER), bf16),
                   jax.ShapeDtypeStruct((batch, D_INNER, D_STATE), f32)],
        scratch_shapes=[pltpu.VMEM((N_SSD_GROUPS, D_STATE, GROUP_WIDTH), f32),
                        pltpu.VMEM((q, LANES), f32), pltpu.VMEM((q, LANES), f32), pltpu.VMEM((LANES, q), f32),
                        pltpu.VMEM((q, D_INNER), bf16), pltpu.VMEM((q, D_INNER), f32)],
        name="ssd_prompt", args=(z, xc, h1, wdt, dtb, alog, dsk, nw), carry=carry)


def _ssd_decode_kernel(z_ref, xbc_ref, h1_ref, cst_ref, s_ref, wdt_ref, cw_ref, cb_ref, dtb_ref, alog_ref, dsk_ref,
                       nw_ref, e_ref, y_ref, so_ref, xdtt_ref, yt_ref, *, bb):
    acc = cb_ref[...] + xbc_ref[...] * cw_ref[CONV_WIDTH - 1:CONV_WIDTH, :]
    for w in range(CONV_WIDTH - 1):
        acc = acc + cst_ref[w] * cw_ref[w:w + 1, :]
    xc = _silu(acc)
    xs = xc[:, :D_INNER]
    lane_h = lax.broadcasted_iota(jnp.int32, (bb, LANES), 1)
    dt = _softplus(_dot(h1_ref[...].astype(bf16), wdt_ref[...]) + dtb_ref[...])
    dt = jnp.where(lane_h < N_SSD_HEADS, dt, 0.0)
    decay = jnp.exp(dt * (-jnp.exp(alog_ref[...])))
    xdt = xs * _dot_exact(dt, e_ref[...])
    pad = lambda a: jnp.concatenate([a, jnp.zeros((LANES - bb, a.shape[1]), f32)], axis=0)
    xdtt_ref[...] = pad(xdt).T.astype(bf16)
    row = lax.broadcasted_iota(jnp.int32, (LANES, LANES), 0)
    lane = lax.broadcasted_iota(jnp.int32, (LANES, LANES), 1)
    for g in range(N_SSD_GROUPS):
        gs = slice(g * GROUP_WIDTH, (g + 1) * GROUP_WIDTH)
        b_pad = pad(xc[:, D_INNER + g * D_STATE:D_INNER + (g + 1) * D_STATE])
        c_pad_t = pad(xc[:, D_INNER + (N_SSD_GROUPS + g) * D_STATE:D_INNER + (N_SSD_GROUPS + g + 1) * D_STATE]).T
        y_t = jnp.zeros((GROUP_WIDTH, LANES), f32)
        for j in range(bb):
            outer = _dot(xdtt_ref[gs, :], jnp.where(row == j, b_pad, 0.0).astype(bf16))
            pieces = []
            for hh in range(HEADS_PER_GROUP):
                h = g * HEADS_PER_GROUP + hh
                rs = slice(hh * SSD_HEAD_DIM, (hh + 1) * SSD_HEAD_DIM)
                new = s_ref[j, g * GROUP_WIDTH + hh * SSD_HEAD_DIM:g * GROUP_WIDTH + (hh + 1) * SSD_HEAD_DIM, :] \
                    * decay[j:j + 1, h:h + 1] + outer[rs, :]
                so_ref[j, g * GROUP_WIDTH + hh * SSD_HEAD_DIM:g * GROUP_WIDTH + (hh + 1) * SSD_HEAD_DIM, :] = new
                pieces.append(new.astype(bf16))
            y_t = y_t + _dot(jnp.concatenate(pieces, axis=0), jnp.where(lane == j, c_pad_t, 0.0).astype(bf16))
        yt_ref[gs, :] = y_t
    y = yt_ref[...].T[:bb, :] + dsk_ref[...] * xs
    y = y * _silu(z_ref[...])
    y = y * lax.rsqrt(jnp.mean(y * y, axis=-1, keepdims=True) + RMS_EPS) * nw_ref[...]
    y_ref[...] = y


def _ssd_decode(z, xbc, h1, conv_state, ssm_state, wdt, cw, cb, dtb, alog, dsk, nw, e, *, bb):
    n = z.shape[0]
    return pl.pallas_call(
        functools.partial(_ssd_decode_kernel, bb=bb),
        grid=(n // bb,),
        in_specs=[pl.BlockSpec((bb, D_INNER), lambda i: (i, 0)), pl.BlockSpec((bb, CONV_DIM), lambda i: (i, 0)),
                  pl.BlockSpec((bb, D_MODEL), lambda i: (i, 0)),
                  pl.BlockSpec((CONV_WIDTH - 1, bb, CONV_DIM), lambda i: (0, i, 0)),
                  pl.BlockSpec((bb, D_INNER, D_STATE), lambda i: (i, 0, 0)),
                  _resident(wdt.shape), _resident(cw.shape), _resident(cb.shape), _resident(dtb.shape),
                  _resident(alog.shape), _resident(dsk.shape), _resident(nw.shape), _resident(e.shape)],
        out_specs=[pl.BlockSpec((bb, D_INNER), lambda i: (i, 0)),
                   pl.BlockSpec((bb, D_INNER, D_STATE), lambda i: (i, 0, 0))],
        out_shape=[jax.ShapeDtypeStruct((n, D_INNER), f32),
                   jax.ShapeDtypeStruct((n, D_INNER, D_STATE), f32)],
        scratch_shapes=[pltpu.VMEM((D_INNER, LANES), bf16), pltpu.VMEM((D_INNER, LANES), f32)],
        compiler_params=_cparams(1),
        name="ssd_decode",
    )(z, xbc, h1, jnp.swapaxes(conv_state, 0, 1), ssm_state, wdt, cw, cb, dtb, alog, dsk, nw, e)


def _attn_prompt_kernel(slope_ref, q_ref, k_ref, v_ref, o_ref, st_ref, kprev_ref, vprev_ref, *, dilation, rows):
    n = pl.program_id(2)
    blk = ATTN_BLOCK
    nb = rows // blk

    @pl.when(n == 0)
    def _():
        kprev_ref[...] = jnp.zeros_like(kprev_ref)
        vprev_ref[...] = jnp.zeros_like(vprev_ref)

    a = lax.broadcasted_iota(jnp.int32, (blk, blk), 0)
    c = lax.broadcasted_iota(jnp.int32, (blk, blk), 1)
    dist_prev = (dilation * (blk + a - c)).astype(f32)
    dist_cur = (dilation * (a - c)).astype(f32)
    valid_cur = c <= a
    valid_prev = c >= a
    valid_prev_first = (c - a + jnp.where(n > 0, 0, -2 * blk)) >= 0
    bias_cur, bias_prev, bias_prev_first = [], [], []
    for h in range(HEADS_PER_PATTERN):
        slope = slope_ref[h:h + 1, :]
        bias_cur.append(jnp.where(valid_cur, -slope * dist_cur, NEG_INF))
        bias_prev.append(jnp.where(valid_prev, -slope * dist_prev, NEG_INF))
        bias_prev_first.append(jnp.where(valid_prev_first, -slope * dist_prev, NEG_INF))

    qs, kcs, vcs, kps, vps, bcs, bps = [], [], [], [], [], [], []
    for i in range(nb):
        rs = slice(i * blk, (i + 1) * blk)
        ps = slice((i - 1) * blk, i * blk)
        for h in range(HEADS_PER_PATTERN):
            hs = slice(h * ATTN_HEAD_DIM, (h + 1) * ATTN_HEAD_DIM)
            qs.append(q_ref[rs, hs])
            kcs.append(k_ref[rs, hs])
            vcs.append(v_ref[rs, hs])
            kps.append(k_ref[ps, hs] if i > 0 else kprev_ref[:, hs])
            vps.append(v_ref[ps, hs] if i > 0 else vprev_ref[:, hs])
            bcs.append(bias_cur[h])
            bps.append(bias_prev[h] if i > 0 else bias_prev_first[h])
    q3 = jnp.stack(qs)
    scale = ATTN_HEAD_DIM ** -0.5
    qk = lambda x, y: jnp.einsum("bqd,bkd->bqk", x, y, preferred_element_type=f32)
    pv = lambda x, y: jnp.einsum("bqk,bkd->bqd", x, y, preferred_element_type=f32)
    s_p = qk(q3, jnp.stack(kps)) * scale + jnp.stack(bps)
    s_c = qk(q3, jnp.stack(kcs)) * scale + jnp.stack(bcs)
    m = jnp.maximum(jnp.max(s_p, axis=-1, keepdims=True), jnp.max(s_c, axis=-1, keepdims=True))
    p_p = jnp.exp(s_p - m)
    p_c = jnp.exp(s_c - m)
    l = jnp.sum(p_p, axis=-1, keepdims=True) + jnp.sum(p_c, axis=-1, keepdims=True)
    o = (pv(p_p.astype(bf16), jnp.stack(vps)) + pv(p_c.astype(bf16), jnp.stack(vcs))) / l
    lse = m + jnp.log(l)
    lane = lax.broadcasted_iota(jnp.int32, (blk, LANES), 1)
    for i in range(nb):
        rs = slice(i * blk, (i + 1) * blk)
        stat = jnp.zeros((blk, LANES), f32)
        for h in range(HEADS_PER_PATTERN):
            idx = i * HEADS_PER_PATTERN + h
            o_ref[rs, h * ATTN_HEAD_DIM:(h + 1) * ATTN_HEAD_DIM] = o[idx].astype(o_ref.dtype)
            stat = jnp.where(lane == h, lse[idx], stat)
        st_ref[rs, :] = stat
    kprev_ref[...] = k_ref[rows - blk:rows, :]
    vprev_ref[...] = v_ref[rows - blk:rows, :]


ATTN_ROWS = 1024


def _attn_prompt(qkv, slopes, g):
    window, dilation = ATTN_PATTERNS[g]
    assert window // dilation == ATTN_BLOCK
    batch, _, length, _ = qkv.shape
    rows = min(length, ATTN_ROWS)
    spec = lambda width, off: pl.BlockSpec((None, None, rows, width), lambda b, r, n: (b, r, n, off))
    return pl.pallas_call(
        functools.partial(_attn_prompt_kernel, dilation=dilation, rows=rows),
        grid=(batch, dilation, length // rows),
        in_specs=[_resident(slopes.shape), spec(ATTN_WIDTH, 0), spec(ATTN_WIDTH, 1), spec(ATTN_WIDTH, 2)],
        out_specs=[spec(ATTN_WIDTH, 0), spec(LANES, 0)],
        out_shape=[jax.ShapeDtypeStruct((batch, dilation, length, ATTN_WIDTH), bf16),
                   jax.ShapeDtypeStruct((batch, dilation, length, LANES), f32)],
        scratch_shapes=[pltpu.VMEM((ATTN_BLOCK, ATTN_WIDTH), bf16), pltpu.VMEM((ATTN_BLOCK, ATTN_WIDTH), bf16)],
        compiler_params=_cparams(3),
        name=f"attn_prompt_d{dilation}",
    )(slopes, qkv, qkv, qkv)


def _attn_decode_kernel(bias_ref, qkv_ref, c0_ref, c1_ref, c2_ref, o_ref):
    scale = ATTN_HEAD_DIM ** -0.5
    outs, ms, ls = [], [], []
    for g, c_ref in enumerate((c0_ref, c1_ref, c2_ref)):
        q = qkv_ref[:, g, 0]
        k_new = qkv_ref[:, g, 1]
        v_new = qkv_ref[:, g, 2]
        s = jnp.sum(c_ref[:, :, 0] * q[:, None], axis=-1, keepdims=True) * scale + bias_ref[g]
        s_new = jnp.sum(k_new * q, axis=-1, keepdims=True) * scale
        m = jnp.maximum(jnp.max(s, axis=1), s_new)
        p = jnp.exp(s - m[:, None])
        p_new = jnp.exp(s_new - m)
        l = jnp.sum(p, axis=1) + p_new
        o = (jnp.sum(p * c_ref[:, :, 1], axis=1) + p_new * v_new) / l
        outs.append(o)
        ms.append(m)
        ls.append(l)
    m_all = jnp.maximum(jnp.maximum(ms[0], ms[1]), ms[2])
    wgts = [l * jnp.exp(m - m_all) for m, l in zip(ms, ls)]
    num = wgts[0] * outs[0] + wgts[1] * outs[1] + wgts[2] * outs[2]
    o_ref[...] = num / (wgts[0] + wgts[1] + wgts[2])


def _attn_decode(qkv, caches, bias, *, bb):
    n = qkv.shape[0]
    in_specs = [_resident(bias.shape),
                pl.BlockSpec((bb, N_PATTERNS, 3, HEADS_PER_PATTERN, ATTN_HEAD_DIM), lambda i: (i, 0, 0, 0, 0))]
    views = []
    for g, (window, dilation) in enumerate(ATTN_PATTERNS):
        assert caches[g].shape[1] == window and window // dilation == ATTN_BLOCK
        views.append(caches[g].reshape(n, ATTN_BLOCK, dilation, 2, HEADS_PER_PATTERN, ATTN_HEAD_DIM))
        in_specs.append(pl.BlockSpec((bb, ATTN_BLOCK, None, 2, HEADS_PER_PATTERN, ATTN_HEAD_DIM),
                                     lambda i: (i, 0, 0, 0, 0, 0)))
    return pl.pallas_call(
        _attn_decode_kernel,
        grid=(n // bb,),
        in_specs=in_specs,
        out_specs=pl.BlockSpec((bb, HEADS_PER_PATTERN, ATTN_HEAD_DIM), lambda i: (i, 0, 0)),
        out_shape=jax.ShapeDtypeStruct((n, HEADS_PER_PATTERN, ATTN_HEAD_DIM), f32),
        compiler_params=_cparams(1),
        name="attn_decode",
    )(bias, qkv, *views)


def _post_mix_kernel(*refs, dilations):
    n_pat = len(dilations)
    merged = n_pat > 1
    h1_ref, y_ref = refs[0], refs[1]
    o_refs = refs[2:2 + n_pat]
    st_refs = refs[2 + n_pat:2 + 2 * n_pat] if merged else ()
    k = 2 + n_pat + len(st_refs)
    wg_ref, wso_ref, wao_ref, wmo_ref, lnw_ref, lnb_ref, out_ref = refs[k:k + 7]
    scratch = refs[k + 7:]
    tm = h1_ref.shape[0]
    h1 = h1_ref[...]
    gates = jax.nn.sigmoid(_dot(h1.astype(bf16), wg_ref[...]))
    if merged:
        o_heads, lses = [], []
        for g, d in enumerate(dilations):
            if d == 1:
                o_heads.append([o_refs[g][:, h * ATTN_HEAD_DIM:(h + 1) * ATTN_HEAD_DIM].astype(f32)
                                for h in range(HEADS_PER_PATTERN)])
                lses.append(st_refs[g])
                continue
            o_scr, st_scr = scratch[0], scratch[1]
            scratch = scratch[2:]
            for r in range(d):
                st_scr[pl.ds(r, tm // d, stride=d), :] = st_refs[g][r]
                for h in range(HEADS_PER_PATTERN):
                    o_scr[h, pl.ds(r, tm // d, stride=d), :] = (
                        o_refs[g][r, :, h * ATTN_HEAD_DIM:(h + 1) * ATTN_HEAD_DIM].astype(f32))
            o_heads.append([o_scr[h] for h in range(HEADS_PER_PATTERN)])
            lses.append(st_scr)
        heads = []
        for h in range(HEADS_PER_PATTERN):
            lse = [st[:, h:h + 1] for st in lses]
            top = functools.reduce(jnp.maximum, lse)
            wgt = [jnp.exp(v - top) for v in lse]
            num = sum(w * o[h] for w, o in zip(wgt, o_heads))
            heads.append(num / sum(wgt))
        attn = jnp.concatenate(heads, axis=1)
    else:
        attn = o_refs[0][...]
    t_ssd = _dot(y_ref[...].astype(bf16), wso_ref[...])
    t_attn = _dot(attn.astype(bf16), wao_ref[...])
    u = gates[:, :D_MODEL] * t_ssd + gates[:, D_MODEL:] * t_attn
    mix = _dot(u.astype(bf16), wmo_ref[...])
    out_ref[...] = _layer_norm(DEEPNORM_ALPHA * h1 + mix, lnw_ref[...], lnb_ref[...])


def _post_mix(h1, y_ssd, outs, stats, dilations, wg, wso, wao, wmo, lnw, lnb, *, tm, seq, carry=None):
    m = h1.shape[0]
    steps = seq // tm
    row = lambda width: pl.BlockSpec((tm, width), lambda i: (i, 0))

    def pattern_spec(width, d):
        if d == 1:
            return row(width)
        return pl.BlockSpec((None, d, tm // d, width), lambda i: (i // steps, 0, i % steps, 0))

    in_specs = [row(D_MODEL), row(D_INNER)] + [pattern_spec(ATTN_WIDTH, d) for d in dilations]
    in_specs += [pattern_spec(LANES, d) for d in dilations[:len(stats)]]
    in_specs += [_resident(w.shape) for w in (wg, wso, wao, wmo, lnw, lnb)]
    scratch = []
    for d in dilations:
        if d > 1:
            scratch += [pltpu.VMEM((HEADS_PER_PATTERN, tm, ATTN_HEAD_DIM), f32), pltpu.VMEM((tm, LANES), f32)]
    return _call(functools.partial(_post_mix_kernel, dilations=tuple(dilations)), grid=(m // tm,), in_specs=in_specs,
                 out_specs=[row(D_MODEL)], out_shape=[jax.ShapeDtypeStruct((m, D_MODEL), f32)],
                 scratch_shapes=scratch, name="post_mix",
                 args=(h1, y_ssd, *outs, *stats, wg, wso, wao, wmo, lnw, lnb), carry=carry)


SHIFT_CHUNK = 256
SHIFT_BLOCK_BYTES = 8 * 1024 * 1024


def _kv_shift_kernel(c_ref, new_ref, o_ref, *, key_rows):
    bb, rows, _ = c_ref.shape
    keep = rows - key_rows
    n_chunks = pl.cdiv(keep, SHIFT_CHUNK)
    for b in range(bb):
        def body(k, carry):
            r0 = pl.multiple_of(jnp.minimum(k * SHIFT_CHUNK, keep - SHIFT_CHUNK), 8)
            o_ref[b, pl.ds(r0, SHIFT_CHUNK), :] = c_ref[b, pl.ds(r0 + key_rows, SHIFT_CHUNK), :]
            return carry
        lax.fori_loop(0, n_chunks, body, 0)
        o_ref[b, keep:rows, :] = new_ref[b]


def _kv_shift(cache, new):
    n, w = cache.shape[:2]
    key_rows = 2 * HEADS_PER_PATTERN * ATTN_HEAD_DIM // LANES
    rows = w * key_rows
    assert key_rows % 8 == 0 and rows - key_rows >= SHIFT_CHUNK
    bb = max(1, min(n, SHIFT_BLOCK_BYTES // (rows * LANES * 4)))
    assert n % bb == 0
    out = pl.pallas_call(
        functools.partial(_kv_shift_kernel, key_rows=key_rows),
        grid=(n // bb,),
        in_specs=[pl.BlockSpec((bb, rows, LANES), lambda i: (i, 0, 0)),
                  pl.BlockSpec((bb, key_rows, LANES), lambda i: (i, 0, 0))],
        out_specs=pl.BlockSpec((bb, rows, LANES), lambda i: (i, 0, 0)),
        out_shape=jax.ShapeDtypeStruct((n, rows, LANES), cache.dtype),
        compiler_params=_cparams(1),
        name=f"kv_shift_w{w}",
    )(cache.reshape(n, rows, LANES), new.reshape(n, key_rows, LANES))
    return out.reshape(cache.shape)


def _alibi_slopes():
    n = N_PATTERNS * HEADS_PER_PATTERN
    e = jnp.arange(1, n + 1, dtype=f32)
    return (2.0 ** (-ALIBI_MAX_EXP * e / n)).reshape(N_PATTERNS, HEADS_PER_PATTERN)


def _prepare_weights(ln_w, ln_b, ffn_w13, ffn_w2, w_in, conv_w, conv_b, dt_bias, a_log, d_skip, ssd_norm_w,
                     w_ssd_out, w_attn_out, w_mix_out, w_pe_gate, w_pe_proj):
    cuts = [0]
    for s in IN_SPLITS:
        cuts.append(cuts[-1] + s)
    wz, wxbc, wdt, wq, wk, wv = (w_in[:, cuts[i]:cuts[i + 1]] for i in range(6))
    pad_heads = LANES - N_SSD_HEADS
    head_of_col = jnp.arange(D_INNER, dtype=jnp.int32) // SSD_HEAD_DIM
    return dict(
        ln_w=[ln_w[i:i + 1] for i in range(3)], ln_b=[ln_b[i:i + 1] for i in range(3)],
        w13=[ffn_w13[i].astype(bf16) for i in range(2)], w2=[ffn_w2[i].astype(bf16) for i in range(2)],
        wz=wz.astype(bf16), wxbc=wxbc.astype(bf16),
        wdt=jnp.pad(wdt, ((0, 0), (0, pad_heads))).astype(bf16),
        wqkv=jnp.concatenate([m[:, g * ATTN_WIDTH:(g + 1) * ATTN_WIDTH] for g in range(N_PATTERNS)
                              for m in (wq, wk, wv)], axis=1).astype(bf16),
        wg=w_in[:, cuts[6]:].astype(bf16),
        cw=conv_w, cb=conv_b[None, :],
        dtb=jnp.pad(dt_bias, (0, pad_heads))[None, :], alog=jnp.pad(a_log, (0, pad_heads))[None, :],
        dsk=jnp.repeat(d_skip, SSD_HEAD_DIM)[None, :], nw=ssd_norm_w[None, :],
        expand=(jnp.arange(LANES, dtype=jnp.int32)[:, None] == head_of_col[None, :]).astype(f32),
        wso=w_ssd_out.astype(bf16), wao=w_attn_out.astype(bf16), wmo=w_mix_out.astype(bf16),
        wpg=w_pe_gate.astype(bf16), wpp=w_pe_proj.astype(bf16),
    )


def _ssd_args(w):
    return (w["wdt"], w["cw"], w["cb"], w["dtb"], w["alog"], w["dsk"], w["nw"], w["expand"])


class _CacheCarrier:
    def __init__(self, cache, new):
        n, w = cache.shape[:2]
        self.shape = cache.shape
        self.key_rows = 2 * HEADS_PER_PATTERN * ATTN_HEAD_DIM // LANES
        self.cache = cache.reshape(n, w * self.key_rows, LANES)
        self.new = new.reshape(n, self.key_rows, LANES)
        self.out, self.done = None, 0

    def take(self, steps, parts):
        samples = min(steps // parts, self.cache.shape[0] - self.done)
        if samples <= 0:
            return None
        return (self.cache, self.new, self.out, self.done * parts, samples * parts, parts)

    def gave(self, carry, out):
        if carry is not None:
            self.out, self.done = out, self.done + carry[4] // carry[5]

    def result(self):
        assert self.done == self.cache.shape[0]
        return self.out.reshape(self.shape)


def _prompt_layer(x, p, w, big_carrier, mid_carrier):
    batch, seq, _ = x.shape
    assert seq % SSD_CHUNK == 0 and all(seq % (ATTN_BLOCK * d) == 0 and seq >= win for win, d in ATTN_PATTERNS)
    tm = 512
    x2 = x.reshape(batch * seq, D_MODEL)

    def carried(carrier, n_steps, parts, fn, *args, **kwargs):
        carry = carrier.take(n_steps, parts) if carrier is not None else None
        outs, cache_out = fn(*args, carry=carry, **kwargs)
        if carrier is not None:
            carrier.gave(carry, cache_out)
        return outs

    (h1,), _ = _ffn_ln(x2, w["w13"][0], w["w2"][0], w["ln_w"][0], w["ln_b"][0], tm=tm)
    z, xc, xbc_tail = carried(big_carrier, batch * seq // PROJ_ZX_ROWS, 1, _proj_zx, h1, w["wz"], w["wxbc"], w["cw"],
                              w["cb"], batch=batch, seq=seq, tm=PROJ_ZX_ROWS)
    qkv = carried(big_carrier, batch * seq // tm, 1, _qkv_proj, h1, w["wqkv"], batch=batch, seq=seq, tm=tm)
    y_ssd, new_ssm = carried(mid_carrier, batch * seq // SSD_CHUNK, 1, _ssd_prompt, z, xc, h1, w["wdt"], w["dtb"],
                             w["alog"], w["dsk"], w["nw"], batch=batch)
    slopes = _alibi_slopes()
    dilations = [d for _, d in ATTN_PATTERNS]
    outs, stats, new_kv = [], [], []
    for g, (win, d) in enumerate(ATTN_PATTERNS):
        o, st = _attn_prompt(qkv[g], jnp.broadcast_to(slopes[g][:, None], (HEADS_PER_PATTERN, LANES)), g)
        if d == 1:
            o, st = o.reshape(batch * seq, ATTN_WIDTH), st.reshape(batch * seq, LANES)
        outs.append(o)
        stats.append(st)
        tail = qkv[g][:, :, seq // d - win // d:, ATTN_WIDTH:]
        new_kv.append(jnp.swapaxes(tail, 1, 2).reshape(batch, win, 2, HEADS_PER_PATTERN, ATTN_HEAD_DIM).astype(f32))
    (h2,), _ = _post_mix(h1, y_ssd, outs, stats, dilations, w["wg"], w["wso"], w["wao"], w["wmo"], w["ln_w"][1],
                         w["ln_b"][1], tm=tm, seq=seq)
    y, = carried(big_carrier, batch * seq // tm, 1, _ffn_ln, h2, w["w13"][1], w["w2"][1], w["ln_w"][2], w["ln_b"][2],
                 pe=(p.reshape(batch * seq, PLE_DIM), w["wpg"], w["wpp"]), tm=tm)
    new_conv = xbc_tail[:, 8 - (CONV_WIDTH - 1):]
    return (y.reshape(batch, seq, D_MODEL), new_ssm.reshape(batch, N_SSD_HEADS, SSD_HEAD_DIM, D_STATE),
            new_conv, new_kv)


def _decode_layer(x, p, ssm_state, conv_state, caches, w):
    n, t, _ = x.shape
    assert t == 1
    tm = n
    bb = 8
    (h1,), _ = _ffn_ln(x.reshape(n, D_MODEL), w["w13"][0], w["w2"][0], w["ln_w"][0], w["ln_b"][0], tm=tm)
    z = _matmul(h1, w["wz"], f32, tm=tm)
    xbc = _matmul(h1, w["wxbc"], f32, tm=tm)
    qkv = _matmul(h1, w["wqkv"], f32, tm=tm)
    y_ssd, new_ssm = _ssd_decode(z, xbc, h1, conv_state, ssm_state.reshape(n, D_INNER, D_STATE), *_ssd_args(w), bb=bb)
    slopes = _alibi_slopes()
    steps_back = (ATTN_BLOCK - jnp.arange(ATTN_BLOCK, dtype=jnp.int32))
    bias = jnp.stack([-slopes[g][None, :] * (d * steps_back).astype(f32)[:, None]
                      for g, (_, d) in enumerate(ATTN_PATTERNS)])
    bias = jnp.broadcast_to(bias[..., None], bias.shape + (1,))
    qkv5 = qkv.reshape(n, N_PATTERNS, 3, HEADS_PER_PATTERN, ATTN_HEAD_DIM)
    attn = _attn_decode(qkv5, caches, bias, bb=bb).reshape(n, ATTN_WIDTH)
    (h2,), _ = _post_mix(h1, y_ssd, [attn], [], [1], w["wg"], w["wso"], w["wao"], w["wmo"], w["ln_w"][1],
                         w["ln_b"][1], tm=tm, seq=n)
    (y,), _ = _ffn_ln(h2, w["w13"][1], w["w2"][1], w["ln_w"][2], w["ln_b"][2],
                      pe=(p.reshape(n, PLE_DIM), w["wpg"], w["wpp"]), tm=tm)
    new_conv = jnp.concatenate([conv_state[:, 1:], xbc[:, None, :]], axis=1)
    new_rows = [qkv5[:, g, 1:3] for g in range(N_PATTERNS)]
    return y.reshape(n, 1, D_MODEL), new_ssm.reshape(n, N_SSD_HEADS, SSD_HEAD_DIM, D_STATE), new_conv, new_rows


PROJ_ZX_ROWS = 256


def _carrier_capacity(batch, seq):
    return batch * seq // PROJ_ZX_ROWS + 2 * (batch * seq // 512), batch * seq // SSD_CHUNK


def kernel(x_prompt, x_sample, state_ssm, state_conv, cache_kv_w128, cache_kv_w512, cache_kv_w2048, p_prompt, p_sample, ln_w, ln_b, ffn_w13, ffn_w2, w_in, conv_w, conv_b, dt_bias, a_log, d_skip, ssd_norm_w, w_ssd_out, w_attn_out, w_mix_out, w_pe_gate, w_pe_proj):
    assert ln_w.shape[0] == DEPTH
    w = _prepare_weights(ln_w[0], ln_b[0], ffn_w13[0], ffn_w2[0], w_in[0], conv_w[0], conv_b[0], dt_bias[0], a_log[0],
                         d_skip[0], ssd_norm_w[0], w_ssd_out[0], w_attn_out[0], w_mix_out[0], w_pe_gate[0],
                         w_pe_proj[0])
    caches = (cache_kv_w128[0], cache_kv_w512[0], cache_kv_w2048[0])
    y_s, ssm_s, conv_s, new_rows = _decode_layer(x_sample, p_sample[0], state_ssm[0], state_conv[0], caches, w)
    order = sorted(range(N_PATTERNS), key=lambda g: -caches[g].shape[1])
    big, mid = order[0], order[1]
    cap_big, cap_mid = _carrier_capacity(x_prompt.shape[0], x_prompt.shape[1])
    big_carrier = _CacheCarrier(caches[big], new_rows[big]) if caches[big].shape[0] <= cap_big else None
    mid_carrier = _CacheCarrier(caches[mid], new_rows[mid]) if caches[mid].shape[0] <= cap_mid else None
    y_p, ssm_p, conv_p, kv_p = _prompt_layer(x_prompt, p_prompt[0], w, big_carrier, mid_carrier)
    carriers = {big: big_carrier, mid: mid_carrier}
    kv_s = [carriers[g].result() if carriers.get(g) is not None else _kv_shift(caches[g], new_rows[g])
            for g in range(N_PATTERNS)]
    lead = lambda a: a[None]
    return (y_p, y_s, lead(ssm_p), lead(conv_p), lead(kv_p[0]), lead(kv_p[1]), lead(kv_p[2]),
            lead(ssm_s), lead(conv_s), lead(kv_s[0]), lead(kv_s[1]), lead(kv_s[2]))
```
